```python
import math
import jax, jax.numpy as jnp
from jax import lax
import numpy as np

D_MODEL = 2048
BATCH = 16
SEQ = 256
DEPTH = 1
DEC_BATCH = 2
DEC_SEQ = 4096
PAST_LEN = 256

GRID_W = 64
N_DIR = 2
EPS = 1e-6
MIX_WIDTH = D_MODEL
GLA_WIDTH = MIX_WIDTH // 2
GLA_HEADS = 4
GLA_KEY_WIDTH = GLA_WIDTH // 2
GLA_DK = GLA_KEY_WIDTH // GLA_HEADS
GLA_DV = GLA_WIDTH // GLA_HEADS
GLA_GATE_RANK = 16
GLA_GATE_TAU = 16.0
GLA_CHUNK = 64
SSD_WIDTH = MIX_WIDTH - GLA_WIDTH
SSD_HEAD_DIM = 64
SSD_HEADS = SSD_WIDTH // SSD_HEAD_DIM
SSD_GROUPS = 2
SSD_HPG = SSD_HEADS // SSD_GROUPS
SSD_STATE = 128
SSD_CONV_W = 3
SSD_CHUNK = 128
SSD_CONV_CH = SSD_WIDTH + 2 * SSD_GROUPS * SSD_STATE
IN_SPLITS = (GLA_KEY_WIDTH, GLA_KEY_WIDTH, GLA_WIDTH, GLA_WIDTH, N_DIR * GLA_GATE_RANK,
             SSD_WIDTH, SSD_CONV_CH, N_DIR * SSD_HEADS)
IN_PROJ_WIDTH = sum(IN_SPLITS)
N_EXPERTS = 256
TOP_K = 8
N_EXPERT_GROUPS = 8
TOPK_GROUPS = 4
EXPERT_FF = 512
SHARED_FF = 512
ROUTED_SCALE = 2.5
MOE_BLOCK = 128

kernel_name = "hybrid_gla_ssd_moe_diffusion_step"


def rms_norm(x, g):
    xf = x.astype(jnp.float32)
    y = xf * lax.rsqrt(jnp.mean(xf * xf, axis=-1, keepdims=True) + EPS)
    return (y * g.astype(jnp.float32)).astype(x.dtype)


def ada_mod(cvec, w_ada, b_ada):
    m = (jax.nn.silu(cvec) @ w_ada + b_ada).reshape(-1, 1, 6 * D_MODEL)
    return jnp.split(m, 6, axis=-1)


def grid_pos_embed(n_tokens, dtype):
    rows = n_tokens // GRID_W
    half = D_MODEL // 2
    quarter = half // 2
    omega = 1.0 / (10000.0 ** (jnp.arange(quarter, dtype=jnp.float32) / quarter))

    def axis_embed(pos):
        ang = pos.astype(jnp.float32)[:, None] * omega
        return jnp.concatenate([jnp.sin(ang), jnp.cos(ang)], axis=-1)

    e_row = axis_embed(jnp.arange(rows))
    e_col = axis_embed(jnp.arange(GRID_W))
    emb = jnp.concatenate([jnp.broadcast_to(e_row[:, None], (rows, GRID_W, half)),
                           jnp.broadcast_to(e_col[None], (rows, GRID_W, half))], axis=-1)
    return emb.reshape(rows * GRID_W, D_MODEL).astype(dtype)


def to_chunks(t, chunk):
    b, l = t.shape[:2]
    return jnp.moveaxis(t.reshape(b, l // chunk, chunk, *t.shape[2:]), 1, 0).astype(jnp.float32)


def from_chunks(t):
    t = jnp.moveaxis(t, 0, 1)
    return t.reshape(t.shape[0], t.shape[1] * t.shape[2], *t.shape[3:])


def gla_scan(q, k, v, log_a, s0):
    mask = jnp.tril(jnp.ones((GLA_CHUNK, GLA_CHUNK), bool))

    def step(S, inp):
        qc, kc, vc, gc = inp
        b = jnp.cumsum(gc, axis=1)
        b_end = b[:, -1]
        q_e = qc * jnp.exp(b)
        k_e = kc * jnp.exp(-b)
        att = jnp.where(mask, jnp.einsum('bthk,bshk->bhts', q_e, k_e), 0.0)
        o = jnp.einsum('bhts,bshv->bthv', att, vc) + jnp.einsum('bthk,bhkv->bthv', q_e, S)
        k_end = kc * jnp.exp(b_end[:, None] - b)
        S = S * jnp.exp(b_end)[..., None] + jnp.einsum('bshk,bshv->bhkv', k_end, vc)
        return S, o

    S, o = lax.scan(step, s0.astype(jnp.float32),
                    (to_chunks(q, GLA_CHUNK), to_chunks(k, GLA_CHUNK),
                     to_chunks(v, GLA_CHUNK), to_chunks(log_a, GLA_CHUNK)))
    return from_chunks(o).astype(v.dtype), S.astype(v.dtype)


def gla_group(q, k, v, g_out, gate_lr, w_gk_up, b_gk, g_norm, s0):
    bsz, n = q.shape[:2]
    q = q.reshape(bsz, n, GLA_HEADS, GLA_DK) * (GLA_DK ** -0.5)
    k = k.reshape(bsz, n, GLA_HEADS, GLA_DK)
    v = v.reshape(bsz, n, GLA_HEADS, GLA_DV)
    gk = jnp.einsum('bldr,drk->bldk', gate_lr.reshape(bsz, n, N_DIR, GLA_GATE_RANK), w_gk_up) + b_gk
    log_a = (jax.nn.log_sigmoid(gk.astype(jnp.float32)) / GLA_GATE_TAU).reshape(
        bsz, n, N_DIR, GLA_HEADS, GLA_DK)
    o_f, s_f = gla_scan(q, k, v, log_a[:, :, 0], s0[:, 0])
    o_b, s_b = gla_scan(q[:, ::-1], k[:, ::-1], v[:, ::-1], log_a[:, ::-1, 1], s0[:, 1])
    o = o_f + o_b[:, ::-1]
    o = rms_norm(o, g_norm) * jax.nn.silu(g_out.reshape(bsz, n, GLA_HEADS, GLA_DV))
    return o.reshape(bsz, n, GLA_WIDTH), jnp.stack([s_f, s_b], axis=1)


def dwconv_centred(x, w, b):
    pad = SSD_CONV_W // 2
    y = lax.conv_general_dilated(x, w[:, None, :].astype(x.dtype), window_strides=(1,),
                                 padding=((pad, pad),), dimension_numbers=('NWC', 'WIO', 'NWC'),
                                 feature_group_count=x.shape[-1])
    return y + b


def ssd_scan(x, a, bm, cm, s0):
    mask = jnp.tril(jnp.ones((SSD_CHUNK, SSD_CHUNK), bool))[None, :, :, None, None]

    def step(S, inp):
        xc, ac, bc, cc = inp
        cs = jnp.cumsum(ac, axis=1)
        seg = cs[:, :, None] - cs[:, None, :]
        lm = jnp.exp(jnp.where(mask, seg, -jnp.inf))
        cb = jnp.einsum('btgn,bsgn->btsg', cc, bc)
        y = jnp.einsum('btsgr,bsgrp->btgrp', cb[..., None] * lm, xc)
        y = y + jnp.einsum('btgn,bgrpn->btgrp', cc, S) * jnp.exp(cs)[..., None]
        w_end = jnp.exp(cs[:, -1:] - cs)
        S = S * jnp.exp(cs[:, -1])[..., None, None] + jnp.einsum('bsgn,bsgr,bsgrp->bgrpn', bc, w_end, xc)
        return S, y

    S, y = lax.scan(step, s0.astype(jnp.float32),
                    (to_chunks(x, SSD_CHUNK), to_chunks(a, SSD_CHUNK),
                     to_chunks(bm, SSD_CHUNK), to_chunks(cm, SSD_CHUNK)))
    return from_chunks(y), S


def ssd_group(z, xbc, dt_raw, conv_w, conv_b, dt_bias, a_log, d_skip, g_norm, s0):
    bsz, n = z.shape[:2]
    xbc = jax.nn.silu(dwconv_centred(xbc, conv_w, conv_b))
    xs, bm, cm = jnp.split(xbc, [SSD_WIDTH, SSD_WIDTH + SSD_GROUPS * SSD_STATE], axis=-1)
    xs = xs.reshape(bsz, n, SSD_GROUPS, SSD_HPG, SSD_HEAD_DIM)
    bm = bm.reshape(bsz, n, SSD_GROUPS, SSD_STATE)
    cm = cm.reshape(bsz, n, SSD_GROUPS, SSD_STATE)
    dt = jax.nn.softplus(dt_raw.astype(jnp.float32).reshape(bsz, n, N_DIR, SSD_HEADS)
                         + dt_bias.astype(jnp.float32))
    log_dec = (dt * -jnp.exp(a_log.astype(jnp.float32))).reshape(bsz, n, N_DIR, SSD_GROUPS, SSD_HPG)
    dt = dt.reshape(bsz, n, N_DIR, SSD_GROUPS, SSD_HPG)
    s0 = s0.reshape(bsz, N_DIR, SSD_GROUPS, SSD_HPG, SSD_HEAD_DIM, SSD_STATE)
    y_f, s_f = ssd_scan(xs * dt[:, :, 0, ..., None], log_dec[:, :, 0], bm, cm, s0[:, 0])
    y_b, s_b = ssd_scan((xs * dt[:, :, 1, ..., None])[:, ::-1], log_dec[:, ::-1, 1],
                        bm[:, ::-1], cm[:, ::-1], s0[:, 1])
    y = y_f + y_b[:, ::-1] + xs * d_skip.reshape(SSD_GROUPS, SSD_HPG, 1)
    y = (y.reshape(bsz, n, SSD_WIDTH) * jax.nn.silu(z)).astype(z.dtype)
    y = rms_norm(y.reshape(bsz, n, SSD_GROUPS, SSD_WIDTH // SSD_GROUPS),
                 g_norm.reshape(SSD_GROUPS, SSD_WIDTH // SSD_GROUPS))
    s_new = jnp.stack([s_f, s_b], axis=1).reshape(bsz, N_DIR, SSD_HEADS, SSD_HEAD_DIM, SSD_STATE)
    return y.reshape(bsz, n, SSD_WIDTH), s_new.astype(z.dtype)


def mixer_sublayer(x, shift, scale, gate, p, s_gla, s_ssd):
    h = rms_norm(x, p['g_mix_pre']) * (1 + scale) + shift
    proj = h @ p['w_in']
    q, k, v, g_out, gate_lr, z, xbc, dt_raw = jnp.split(proj, list(np.cumsum(IN_SPLITS)[:-1]), axis=-1)
    o_gla, s_gla = gla_group(q, k, v, g_out, gate_lr, p['w_gk_up'], p['b_gk'], p['g_gla_norm'], s_gla)
    o_ssd, s_ssd = ssd_group(z, xbc, dt_raw, p['conv_w'], p['conv_b'], p['dt_bias'], p['a_log'],
                             p['d_skip'], p['g_ssd_norm'], s_ssd)
    o = jnp.concatenate([o_gla, o_ssd], axis=-1) @ p['w_out']
    return x + gate * rms_norm(o, p['g_mix_post']), s_gla, s_ssd


def swiglu(h, wg, wu, wd):
    return (jax.nn.silu(h @ wg) * (h @ wu)) @ wd


def routed_experts(h, idx, wts, w_g, w_u, w_d):
    n_tok, d = h.shape
    nk = n_tok * TOP_K
    n_blocks = -(-nk // MOE_BLOCK) + N_EXPERTS
    flat_e = idx.reshape(-1)
    flat_tok = jnp.repeat(jnp.arange(n_tok, dtype=jnp.int32), TOP_K)
    order = jnp.argsort(flat_e)
    e_sorted = flat_e[order]
    counts = jnp.bincount(flat_e, length=N_EXPERTS)
    padded = (counts + MOE_BLOCK - 1) // MOE_BLOCK * MOE_BLOCK
    pad_end = jnp.cumsum(padded)
    pad_start = pad_end - padded
    start = jnp.cumsum(counts) - counts
    dest = pad_start[e_sorted] + jnp.arange(nk) - start[e_sorted]
    n_slots = n_blocks * MOE_BLOCK
    slot_tok = jnp.full((n_slots,), n_tok, jnp.int32).at[dest].set(flat_tok[order])
    slot_w = jnp.zeros((n_slots,), h.dtype).at[dest].set(wts.reshape(-1)[order])
    block_e = jnp.minimum(jnp.searchsorted(pad_end, jnp.arange(n_blocks) * MOE_BLOCK, side='right'),
                          N_EXPERTS - 1)
    h_pad = jnp.concatenate([h, jnp.zeros((1, d), h.dtype)], axis=0)

    def block(args):
        tok, wt, e = args
        xb = h_pad[tok]
        return swiglu(xb, w_g[e], w_u[e], w_d[e]) * wt[:, None]

    out = lax.map(block, (slot_tok.reshape(n_blocks, MOE_BLOCK), slot_w.reshape(n_blocks, MOE_BLOCK), block_e))
    return jnp.zeros((n_tok + 1, d), h.dtype).at[slot_tok].add(out.reshape(n_slots, d))[:n_tok]


def moe(h, p):
    n_tok = h.shape[0]
    scores = jax.nn.sigmoid((h @ p['w_router']).astype(jnp.float32))
    sel = scores + p['b_router'].astype(jnp.float32)
    grp_score = lax.top_k(sel.reshape(n_tok, N_EXPERT_GROUPS, N_EXPERTS // N_EXPERT_GROUPS), 2)[0].sum(-1)
    _, top_g = lax.top_k(grp_score, TOPK_GROUPS)
    gmask = jnp.any(top_g[:, :, None] == jnp.arange(N_EXPERT_GROUPS), axis=1)
    sel = jnp.where(jnp.repeat(gmask, N_EXPERTS // N_EXPERT_GROUPS, axis=1), sel, -jnp.inf)
    _, idx = lax.top_k(sel, TOP_K)
    w = jnp.take_along_axis(scores, idx, axis=1)
    w = (w / jnp.sum(w, axis=-1, keepdims=True) * ROUTED_SCALE).astype(h.dtype)
    routed = routed_experts(h, idx, w, p['w_exp_gate'], p['w_exp_up'], p['w_exp_down'])
    return routed + swiglu(h, p['w_sh_gate'], p['w_sh_up'], p['w_sh_down'])


def ffn_sublayer(x, shift, scale, gate, p):
    bsz, n, d = x.shape
    h = rms_norm(x, p['g_ffn_pre']) * (1 + scale) + shift
    y = moe(h.reshape(bsz * n, d), p).reshape(bsz, n, d)
    return x + gate * rms_norm(y, p['g_ffn_post'])


def setup_inputs(seed: int = 0) -> dict:
    key = jax.random.key(seed)
    ks = list(jax.random.split(key, 40))
    L, D = DEPTH, D_MODEL

    def nrm(shape, s):
        return jax.random.normal(ks.pop(), shape, jnp.float32) * s

    dt0 = jnp.exp(jax.random.uniform(ks.pop(), (L, N_DIR, SSD_HEADS), jnp.float32,
                                     math.log(1e-3), math.log(1e-1)))
    dt_bias = dt0 + jnp.log(-jnp.expm1(-dt0))
    a_log = jnp.log(jax.random.uniform(ks.pop(), (L, N_DIR, SSD_HEADS), jnp.float32, 1.0, 16.0))
    return {
        'x_prompt': nrm((BATCH, SEQ, D), 1.0),
        'x_sample': nrm((DEC_BATCH, DEC_SEQ, D), 1.0),
        'state_gla': nrm((DEC_BATCH, L, N_DIR, GLA_HEADS, GLA_DK, GLA_DV), 1.0),
        'state_ssd': nrm((DEC_BATCH, L, N_DIR, SSD_HEADS, SSD_HEAD_DIM, SSD_STATE), 0.2),
        'c': nrm((DEC_BATCH, D), 1.0),
        'c_ctx': nrm((D,), 1.0),
        'w_ada': nrm((L, D, 6 * D), 0.5 * D ** -0.5),
        'b_ada': nrm((L, 6 * D), 0.02),
        'g_mix_pre': 1.0 + nrm((L, D), 0.1),
        'g_mix_post': 1.0 + nrm((L, D), 0.1),
        'w_in': nrm((L, D, IN_PROJ_WIDTH), D ** -0.5),
        'w_gk_up': nrm((L, N_DIR, GLA_GATE_RANK, GLA_KEY_WIDTH), GLA_GATE_RANK ** -0.5),
        'b_gk': nrm((L, N_DIR, GLA_KEY_WIDTH), 0.1),
        'g_gla_norm': 1.0 + nrm((L, GLA_DV), 0.1),
        'conv_w': nrm((L, SSD_CONV_W, SSD_CONV_CH), SSD_CONV_W ** -0.5),
        'conv_b': nrm((L, SSD_CONV_CH), 0.02),
        'dt_bias': dt_bias,
        'a_log': a_log,
        'd_skip': 1.0 + nrm((L, SSD_HEADS), 0.1),
        'g_ssd_norm': 1.0 + nrm((L, SSD_WIDTH), 0.1),
        'w_out': nrm((L, MIX_WIDTH, D), MIX_WIDTH ** -0.5),
        'g_ffn_pre': 1.0 + nrm((L, D), 0.1),
        'g_ffn_post': 1.0 + nrm((L, D), 0.1),
        'w_router': nrm((L, D, N_EXPERTS), D ** -0.5),
        'b_router': nrm((L, N_EXPERTS), 0.01),
        'w_exp_gate': nrm((L, N_EXPERTS, D, EXPERT_FF), D ** -0.5),
        'w_exp_up': nrm((L, N_EXPERTS, D, EXPERT_FF), D ** -0.5),
        'w_exp_down': nrm((L, N_EXPERTS, EXPERT_FF, D), EXPERT_FF ** -0.5),
        'w_sh_gate': nrm((L, D, SHARED_FF), D ** -0.5),
        'w_sh_up': nrm((L, D, SHARED_FF), D ** -0.5),
        'w_sh_down': nrm((L, SHARED_FF, D), SHARED_FF ** -0.5),
    }


def reference(x_prompt, x_sample, state_gla, state_ssd, c, c_ctx, w_ada, b_ada, g_mix_pre, g_mix_post,
              w_in, w_gk_up, b_gk, g_gla_norm, conv_w, conv_b, dt_bias, a_log, d_skip, g_ssd_norm,
              w_out, g_ffn_pre, g_ffn_post, w_router, b_router, w_exp_gate, w_exp_up, w_exp_down,
              w_sh_gate, w_sh_up, w_sh_down):
    xp = x_prompt
    xs = x_sample + grid_pos_embed(x_sample.shape[1], x_sample.dtype)
    bp = x_prompt.shape[0]
    zero_gla = jnp.zeros((bp, N_DIR, GLA_HEADS, GLA_DK, GLA_DV), x_prompt.dtype)
    zero_ssd = jnp.zeros((bp, N_DIR, SSD_HEADS, SSD_HEAD_DIM, SSD_STATE), x_prompt.dtype)
    new_gla, new_ssd = [], []
    for l in range(DEPTH):
        p = {'g_mix_pre': g_mix_pre[l], 'g_mix_post': g_mix_post[l], 'w_in': w_in[l],
             'w_gk_up': w_gk_up[l], 'b_gk': b_gk[l], 'g_gla_norm': g_gla_norm[l],
             'conv_w': conv_w[l], 'conv_b': conv_b[l], 'dt_bias': dt_bias[l], 'a_log': a_log[l],
             'd_skip': d_skip[l], 'g_ssd_norm': g_ssd_norm[l], 'w_out': w_out[l],
             'g_ffn_pre': g_ffn_pre[l], 'g_ffn_post': g_ffn_post[l], 'w_router': w_router[l],
             'b_router': b_router[l], 'w_exp_gate': w_exp_gate[l], 'w_exp_up': w_exp_up[l],
             'w_exp_down': w_exp_down[l], 'w_sh_gate': w_sh_gate[l], 'w_sh_up': w_sh_up[l],
             'w_sh_down': w_sh_down[l]}
        m_ctx = ada_mod(c_ctx, w_ada[l], b_ada[l])
        m_lat = ada_mod(c, w_ada[l], b_ada[l])
        xp, s_gla, s_ssd = mixer_sublayer(xp, m_ctx[0], m_ctx[1], m_ctx[2], p, zero_gla, zero_ssd)
        xp = ffn_sublayer(xp, m_ctx[3], m_ctx[4], m_ctx[5], p)
        new_gla.append(s_gla)
        new_ssd.append(s_ssd)
        xs, _, _ = mixer_sublayer(xs, m_lat[0], m_lat[1], m_lat[2], p, state_gla[:, l], state_ssd[:, l])
        xs = ffn_sublayer(xs, m_lat[3], m_lat[4], m_lat[5], p)
    new_state_gla = jnp.stack(new_gla, axis=1)
    new_state_ssd = jnp.stack(new_ssd, axis=1)
    return (xp, xs, new_state_gla, new_state_ssd)
```

```python
import functools

import numpy as np
import jax
import jax.numpy as jnp
from jax import lax
from jax.experimental import pallas as pl
from jax.experimental.pallas import tpu as pltpu

F32 = jnp.float32
BF16 = jnp.bfloat16

D_MODEL = 2048
N_CTX_SEQ = 16
CTX_LEN = 256
N_LAT_SEQ = 2
LAT_LEN = 4096
GRID_W = 64
EPS = 1e-6
N_CTX_TOK = N_CTX_SEQ * CTX_LEN
M_TOK = N_CTX_TOK + N_LAT_SEQ * LAT_LEN
SEQ_LENS = (CTX_LEN,) * N_CTX_SEQ + (LAT_LEN,) * N_LAT_SEQ
ROWS_PER_MOD = 4096
N_MOD = M_TOK // ROWS_PER_MOD

GLA_HEADS = 4
GLA_DK = 128
GLA_DV = 256
GLA_KEY_WIDTH = GLA_HEADS * GLA_DK
GLA_WIDTH = GLA_HEADS * GLA_DV
GLA_GATE_RANK = 16
GLA_GATE_TAU = 16.0
GLA_CHUNK = 64
GLA_ROWS = 256

SSD_HEADS = 16
SSD_HEAD_DIM = 64
SSD_GROUPS = 2
SSD_HPG = SSD_HEADS // SSD_GROUPS
SSD_STATE = 128
SSD_WIDTH = SSD_HEADS * SSD_HEAD_DIM
SSD_GROUP_WIDTH = SSD_WIDTH // SSD_GROUPS
SSD_CHUNK = 128
SSD_CONV_CH = SSD_WIDTH + 2 * SSD_GROUPS * SSD_STATE
N_DIR = 2

N_EXPERTS = 256
TOP_K = 8
N_EXPERT_GROUPS = 8
GROUP_SIZE = N_EXPERTS // N_EXPERT_GROUPS
TOPK_GROUPS = 4
EXPERT_FF = 512
ROUTED_SCALE = 2.5

C_GOUT = 0
C_Z = C_GOUT + GLA_WIDTH
C_V = C_Z + SSD_WIDTH
C_XBC = C_V + GLA_WIDTH
C_Q = C_XBC + SSD_CONV_CH
C_K = C_Q + GLA_KEY_WIDTH
MAIN_WIDTH = C_K + GLA_KEY_WIDTH
LANES = 128
SUBLANES = 8
SM_LR = 0
SM_DT = N_DIR * GLA_GATE_RANK

TM_PROJ = 512
TN_PROJ = 512
TM_OUT = 256
TM_ROUTE = 256
MOE_BLK = 128
N_PAIRS = M_TOK * TOP_K
N_MOE_BLOCKS = N_PAIRS // MOE_BLK + N_EXPERTS
N_SLOTS = N_MOE_BLOCKS * MOE_BLK
HALF_D = D_MODEL // 2
T_DISPATCH = 512
T_COMBINE = 128
ADA_TN = 1024
VMEM_LIMIT = 52 * 1024 * 1024


def _mm(a, b):
    return jnp.dot(a, b, preferred_element_type=F32)


def _mm_nt(a, b):
    return lax.dot_general(a, b, (((1,), (1,)), ((), ())), preferred_element_type=F32)


def _mm_tn(a, b):
    return lax.dot_general(a, b, (((0,), (0,)), ((), ())), preferred_element_type=F32)


def _split2(x):
    hi = x.astype(BF16)
    lo = (x - hi.astype(F32)).astype(BF16)
    return hi, lo


def _split3(x):
    hi = x.astype(BF16)
    r = x - hi.astype(F32)
    mid = r.astype(BF16)
    lo = (r - mid.astype(F32)).astype(BF16)
    return hi, mid, lo


def _mm_x3(a, b):
    a_hi, a_lo = _split2(a)
    b_hi, b_lo = _split2(b)
    return _mm(a_hi, b_hi) + _mm(a_lo, b_hi) + _mm(a_hi, b_lo)


def _mm_sel_left(sel_bf, x):
    hi, mid, lo = _split3(x)
    return _mm(sel_bf, hi) + _mm(sel_bf, mid) + _mm(sel_bf, lo)


def _mm_sel_right(x, sel_bf):
    hi, mid, lo = _split3(x)
    return _mm(hi, sel_bf) + _mm(mid, sel_bf) + _mm(lo, sel_bf)


def _sigmoid(x):
    return 1.0 / (1.0 + jnp.exp(-x))


def _silu(x):
    return x * _sigmoid(x)


def _softplus(x):
    return jnp.maximum(x, 0.0) + jnp.log1p(jnp.exp(-jnp.abs(x)))


def _rms(x, g):
    return x * lax.rsqrt(jnp.mean(x * x, axis=-1, keepdims=True) + EPS) * g


def _pack_bf16_pair(x):
    n = x.shape[1] // 2
    hi = lax.bitcast_convert_type(x[:, :n].astype(BF16).astype(F32), jnp.uint32)
    lo = lax.bitcast_convert_type(x[:, n:].astype(BF16).astype(F32), jnp.uint32)
    return hi | (lo >> 16)


def _unpack_bf16_pair(w):
    hi = lax.bitcast_convert_type(w & jnp.uint32(0xFFFF0000), F32).astype(BF16)
    lo = lax.bitcast_convert_type(w << 16, F32).astype(BF16)
    return hi, lo


def _params(sem, vmem=VMEM_LIMIT):
    return pltpu.CompilerParams(dimension_semantics=sem, vmem_limit_bytes=vmem)


def _ada_kernel(c_ref, w_ref, b_ref, o_ref):
    o_ref[...] = _mm_x3(_silu(c_ref[...]), w_ref[...]) + b_ref[...]


def _ada_mod(cvecs, w_ada, b_ada):
    n_out = w_ada.shape[1]
    return pl.pallas_call(
        _ada_kernel,
        out_shape=jax.ShapeDtypeStruct((SUBLANES, n_out), F32),
        grid=(n_out // ADA_TN,),
        in_specs=[pl.BlockSpec((SUBLANES, D_MODEL), lambda j: (0, 0)),
                  pl.BlockSpec((D_MODEL, ADA_TN), lambda j: (0, j)),
                  pl.BlockSpec((1, ADA_TN), lambda j: (0, j))],
        out_specs=pl.BlockSpec((SUBLANES, ADA_TN), lambda j: (0, j)),
        compiler_params=_params(("arbitrary",)),
        name="ada_mod",
    )(cvecs, w_ada, b_ada)


def _inproj_kernel(x_ref, mod_ref, g_ref, w_ref, ws_ref, o_ref, os_ref, h_scr):
    @pl.when(pl.program_id(1) == 0)
    def _():
        h = _rms(x_ref[...], g_ref[...]) * (1.0 + mod_ref[0, 1:2, :]) + mod_ref[0, 0:1, :]
        h_hi, h_lo = _split2(h)
        h_scr[...] = h_hi
        ws_hi, ws_lo = _split2(ws_ref[...])
        os_ref[...] = _mm(h_hi, ws_hi) + _mm(h_lo, ws_hi) + _mm(h_hi, ws_lo)

    o_ref[...] = _mm(h_scr[...], w_ref[...])


def _in_proj(x_all, mods, g_pre, w_main, w_small):
    tm, tn = TM_PROJ, TN_PROJ
    return pl.pallas_call(
        _inproj_kernel,
        out_shape=(jax.ShapeDtypeStruct((M_TOK, MAIN_WIDTH), F32),
                   jax.ShapeDtypeStruct((M_TOK, LANES), F32)),
        grid=(M_TOK // tm, MAIN_WIDTH // tn),
        in_specs=[pl.BlockSpec((tm, D_MODEL), lambda i, j: (i, 0)),
                  pl.BlockSpec((1, SUBLANES, D_MODEL), lambda i, j: (i * tm // ROWS_PER_MOD, 0, 0)),
                  pl.BlockSpec((1, D_MODEL), lambda i, j: (0, 0)),
                  pl.BlockSpec((D_MODEL, tn), lambda i, j: (0, j)),
                  pl.BlockSpec((D_MODEL, LANES), lambda i, j: (0, 0))],
        out_specs=(pl.BlockSpec((tm, tn), lambda i, j: (i, j)),
                   pl.BlockSpec((tm, LANES), lambda i, j: (i, 0))),
        scratch_shapes=[pltpu.VMEM((tm, D_MODEL), BF16)],
        compiler_params=_params(("arbitrary", "arbitrary")),
        name="in_proj",
    )(x_all, mods, g_pre, w_main, w_small)


def _scan_schedule(rows_per_step, reverse):
    blk, flag, s0i, soi, emit, has_prev, has_next = [], [], [], [], [], [], []
    start = 0
    for s, length in enumerate(SEQ_LENS):
        nb = length // rows_per_step
        is_ctx = s < N_CTX_SEQ
        order = range(nb - 1, -1, -1) if reverse else range(nb)
        for n, b in enumerate(order):
            blk.append(start + b)
            flag.append((1 if is_ctx else 2) if n == 0 else 0)
            s0i.append(0 if is_ctx else s - N_CTX_SEQ)
            soi.append(s if is_ctx else N_CTX_SEQ - 1)
            emit.append(1 if (is_ctx and n == nb - 1) else 0)
            has_prev.append(1 if b > 0 else 0)
            has_next.append(1 if b < nb - 1 else 0)
        start += nb
    return tuple(jnp.asarray(np.array(a, np.int32)) for a in (blk, flag, s0i, soi, emit, has_prev, has_next))


def _gla_kernel(blk_ref, flag_ref, s0i_ref, soi_ref, emit_ref,
                q_ref, k_ref, v_ref, sm_ref, wgk_ref, bgk_ref, s0_ref,
                o_ref, so_ref, st_scr, *, reverse):
    i = pl.program_id(1)
    flag = flag_ref[i]

    @pl.when(flag == 1)
    def _():
        st_scr[...] = jnp.zeros_like(st_scr)

    @pl.when(flag == 2)
    def _():
        st_scr[...] = s0_ref[0, 0].T

    c = GLA_CHUNK
    r_id = lax.broadcasted_iota(jnp.int32, (c, c), 0)
    c_id = lax.broadcasted_iota(jnp.int32, (c, c), 1)
    tri = (c_id >= r_id) if reverse else (c_id <= r_id)
    tri_bf = jnp.where(tri, 1.0, 0.0).astype(BF16)

    gk = _mm_x3(sm_ref[...], wgk_ref[...]) + bgk_ref[...]
    log_a = (jnp.minimum(gk, 0.0) - jnp.log1p(jnp.exp(-jnp.abs(gk)))) * (1.0 / GLA_GATE_TAU)

    n_chunks = GLA_ROWS // c
    for ci in (range(n_chunks - 1, -1, -1) if reverse else range(n_chunks)):
        lo = ci * c
        b = _mm_sel_left(tri_bf, log_a[lo:lo + c])
        b_end = b[0:1] if reverse else b[c - 1:c]
        q = q_ref[lo:lo + c, :] * (GLA_DK ** -0.5)
        k = k_ref[lo:lo + c, :]
        v = v_ref[lo:lo + c, :].astype(BF16)
        q_e = (q * jnp.exp(b)).astype(BF16)
        k_e = (k * jnp.exp(-b)).astype(BF16)
        att = jnp.where(tri, _mm_nt(q_e, k_e), 0.0).astype(BF16)
        st = st_scr[...]
        o_ref[lo:lo + c, :] = _mm(att, v) + _mm_nt(q_e, st.astype(BF16))
        k_end = (k * jnp.exp(b_end - b)).astype(BF16)
        st_scr[...] = st * jnp.exp(b_end) + _mm_tn(v, k_end)

    @pl.when(emit_ref[i] == 1)
    def _():
        so_ref[0, 0] = st_scr[...].T


def _gla_scan(proj_main, proj_small, wgk_pad, bgk, s0, reverse):
    sched = _scan_schedule(GLA_ROWS, reverse)[:5]
    n_steps = M_TOK // GLA_ROWS
    t = GLA_ROWS
    q_blk0, k_blk0, v_blk0 = C_Q // GLA_DK, C_K // GLA_DK, C_V // GLA_DV
    return pl.pallas_call(
        functools.partial(_gla_kernel, reverse=reverse),
        out_shape=(jax.ShapeDtypeStruct((M_TOK, GLA_WIDTH), F32),
                   jax.ShapeDtypeStruct((N_CTX_SEQ, GLA_HEADS, GLA_DK, GLA_DV), F32)),
        grid_spec=pltpu.PrefetchScalarGridSpec(
            num_scalar_prefetch=5,
            grid=(GLA_HEADS, n_steps),
            in_specs=[
                pl.BlockSpec((t, GLA_DK), lambda h, i, blk, *_: (blk[i], q_blk0 + h)),
                pl.BlockSpec((t, GLA_DK), lambda h, i, blk, *_: (blk[i], k_blk0 + h)),
                pl.BlockSpec((t, GLA_DV), lambda h, i, blk, *_: (blk[i], v_blk0 + h)),
                pl.BlockSpec((t, LANES), lambda h, i, blk, *_: (blk[i], 0)),
                pl.BlockSpec((LANES, GLA_DK), lambda h, i, *_: (0, h)),
                pl.BlockSpec((1, GLA_DK), lambda h, i, *_: (0, h)),
                pl.BlockSpec((1, 1, GLA_DK, GLA_DV), lambda h, i, blk, flag, s0i, *_: (s0i[i], h, 0, 0)),
            ],
            out_specs=(
                pl.BlockSpec((t, GLA_DV), lambda h, i, blk, *_: (blk[i], h)),
                pl.BlockSpec((1, 1, GLA_DK, GLA_DV), lambda h, i, blk, flag, s0i, soi, *_: (soi[i], h, 0, 0)),
            ),
            scratch_shapes=[pltpu.VMEM((GLA_DV, GLA_DK), F32)]),
        compiler_params=_params(("arbitrary", "arbitrary")),
        name="gla_bwd" if reverse else "gla_fwd",
    )(*sched, proj_main, proj_main, proj_main, proj_small, wgk_pad, bgk, s0)


def _ssd_kernel(blk_ref, flag_ref, s0i_ref, soi_ref, emit_ref, hp_ref, hn_ref,
                xbc_ref, xprev_ref, xnext_ref, sm_ref, cw_ref, cb_ref,
                dtb_r_ref, nea_r_ref, dtb_c_ref, nea_c_ref, dsk_ref, s0_ref,
                y_ref, so_ref, st_scr, *, reverse, lane0, add_skip):
    i = pl.program_id(0)
    flag = flag_ref[i]
    t = SSD_CHUNK

    @pl.when(flag == 1)
    def _():
        st_scr[...] = jnp.zeros_like(st_scr)

    @pl.when(flag == 2)
    def _():
        for g in range(SSD_GROUPS):
            st_scr[g] = s0_ref[0, g].T

    xbc = xbc_ref[...]
    prev = jnp.where(hp_ref[i] == 1, xprev_ref[SUBLANES - 1:SUBLANES, :], 0.0)
    nxt = jnp.where(hn_ref[i] == 1, xnext_ref[0:1, :], 0.0)
    row = lax.broadcasted_iota(jnp.int32, xbc.shape, 0)
    x_m1 = jnp.where(row == 0, prev, pltpu.roll(xbc, 1, 0))
    x_p1 = jnp.where(row == t - 1, nxt, pltpu.roll(xbc, t - 1, 0))
    act = _silu(x_m1 * cw_ref[0:1, :] + xbc * cw_ref[1:2, :] + x_p1 * cw_ref[2:3, :] + cb_ref[...])
    xs = act[:, :SSD_WIDTH]
    bm = act[:, SSD_WIDTH:SSD_WIDTH + SSD_GROUPS * SSD_STATE]
    cm = act[:, SSD_WIDTH + SSD_GROUPS * SSD_STATE:]

    sm = sm_ref[...]
    dt = _softplus(sm + dtb_r_ref[...])
    a = dt * nea_r_ref[...]
    a_t = _softplus(sm.T + dtb_c_ref[...]) * nea_c_ref[...]

    r_id = lax.broadcasted_iota(jnp.int32, (t, t), 0)
    c_id = lax.broadcasted_iota(jnp.int32, (t, t), 1)
    tri = (c_id >= r_id) if reverse else (c_id <= r_id)
    tri_bf = jnp.where(tri, 1.0, 0.0).astype(BF16)
    tri_t_bf = jnp.where((r_id >= c_id) if reverse else (r_id <= c_id), 1.0, 0.0).astype(BF16)
    cs = _mm_sel_left(tri_bf, a)
    cs_t = _mm_sel_right(a_t, tri_t_bf)

    e_r = lax.broadcasted_iota(jnp.int32, (LANES, SSD_WIDTH), 0)
    e_c = lax.broadcasted_iota(jnp.int32, (LANES, SSD_WIDTH), 1)
    expand = jnp.where(e_r - lane0 == e_c // SSD_HEAD_DIM, 1.0, 0.0).astype(BF16)
    dt_x = _mm_sel_right(dt, expand)
    cs_x = _mm_sel_right(cs, expand)
    cs_end_x = cs_x[0:1] if reverse else cs_x[t - 1:t]
    x_in = xs * dt_x
    x_bf = x_in.astype(BF16)
    x_w = (x_in * jnp.exp(cs_end_x - cs_x)).astype(BF16)
    decay_in = jnp.exp(cs_x)
    decay_end = jnp.exp(cs_end_x)

    for g in range(SSD_GROUPS):
        gc = slice(g * SSD_STATE, (g + 1) * SSD_STATE)
        gw = slice(g * SSD_GROUP_WIDTH, (g + 1) * SSD_GROUP_WIDTH)
        c_g = cm[:, gc].astype(BF16)
        b_g = bm[:, gc].astype(BF16)
        cb = _mm_nt(c_g, b_g)
        st = st_scr[g]
        y_state = _mm(c_g, st.astype(BF16)) * decay_in[:, gw]
        for r in range(SSD_HPG):
            hh = g * SSD_HPG + r
            hc = slice(hh * SSD_HEAD_DIM, (hh + 1) * SSD_HEAD_DIM)
            lane = lane0 + hh
            seg = cs[:, lane:lane + 1] - cs_t[lane:lane + 1, :]
            lm = jnp.exp(jnp.where(tri, seg, -jnp.inf))
            y_h = _mm((cb * lm).astype(BF16), x_bf[:, hc]) + y_state[:, r * SSD_HEAD_DIM:(r + 1) * SSD_HEAD_DIM]
            if add_skip:
                y_h = y_h + xs[:, hc] * dsk_ref[:, hc]
            y_ref[:, hc] = y_h
        st_scr[g] = st * decay_end[:, gw] + _mm_tn(b_g, x_w[:, gw])

    @pl.when(emit_ref[i] == 1)
    def _():
        for g in range(SSD_GROUPS):
            so_ref[0, g] = st_scr[g].T


def _ssd_scan(proj_main, proj_small, conv_w, conv_b, dtb_r, nea_r, dtb_c, nea_c, d_skip_x, s0,
              reverse, direction):
    sched = _scan_schedule(SSD_CHUNK, reverse)
    t = SSD_CHUNK
    n_steps = M_TOK // t
    xbc_blk = C_XBC // SSD_CONV_CH
    rb = t // SUBLANES
    n_rb = M_TOK // SUBLANES
    state_blk = (1, SSD_GROUPS, SSD_GROUP_WIDTH, SSD_STATE)
    return pl.pallas_call(
        functools.partial(_ssd_kernel, reverse=reverse, lane0=SM_DT + direction * SSD_HEADS,
                          add_skip=not reverse),
        out_shape=(jax.ShapeDtypeStruct((M_TOK, SSD_WIDTH), F32),
                   jax.ShapeDtypeStruct((N_CTX_SEQ,) + state_blk[1:], F32)),
        grid_spec=pltpu.PrefetchScalarGridSpec(
            num_scalar_prefetch=7,
            grid=(n_steps,),
            in_specs=[
                pl.BlockSpec((t, SSD_CONV_CH), lambda i, blk, *_: (blk[i], xbc_blk)),
                pl.BlockSpec((SUBLANES, SSD_CONV_CH),
                             lambda i, blk, *_: (jnp.maximum(blk[i] * rb - 1, 0), xbc_blk)),
                pl.BlockSpec((SUBLANES, SSD_CONV_CH),
                             lambda i, blk, *_: (jnp.minimum((blk[i] + 1) * rb, n_rb - 1), xbc_blk)),
                pl.BlockSpec((t, LANES), lambda i, blk, *_: (blk[i], 0)),
                pl.BlockSpec((SUBLANES, SSD_CONV_CH), lambda i, *_: (0, 0)),
                pl.BlockSpec((1, SSD_CONV_CH), lambda i, *_: (0, 0)),
                pl.BlockSpec((1, LANES), lambda i, *_: (0, 0)),
                pl.BlockSpec((1, LANES), lambda i, *_: (0, 0)),
                pl.BlockSpec((LANES, 1), lambda i, *_: (0, 0)),
                pl.BlockSpec((LANES, 1), lambda i, *_: (0, 0)),
                pl.BlockSpec((1, SSD_WIDTH), lambda i, *_: (0, 0)),
                pl.BlockSpec(state_blk, lambda i, blk, flag, s0i, *_: (s0i[i], 0, 0, 0)),
            ],
            out_specs=(
                pl.BlockSpec((t, SSD_WIDTH), lambda i, blk, *_: (blk[i], 0)),
                pl.BlockSpec(state_blk, lambda i, blk, flag, s0i, soi, *_: (soi[i], 0, 0, 0)),
            ),
            scratch_shapes=[pltpu.VMEM((SSD_GROUPS, SSD_STATE, SSD_GROUP_WIDTH), F32)]),
        compiler_params=_params(("arbitrary",)),
        name="ssd_bwd" if reverse else "ssd_fwd",
    )(*sched, proj_main, proj_main, proj_main, proj_small, conv_w, conv_b,
      dtb_r, nea_r, dtb_c, nea_c, d_skip_x, s0)


def _outproj_kernel(of_ref, ob_ref, gout_ref, yf_ref, yb_ref, z_ref, x_ref, mod_ref,
                    ggla_ref, gssd_ref, gpost_ref, w_ref, o_ref):
    o = of_ref[...] + ob_ref[...]
    gate = _silu(gout_ref[...])
    parts = []
    for h in range(GLA_HEADS):
        hc = slice(h * GLA_DV, (h + 1) * GLA_DV)
        parts.append((_rms(o[:, hc], ggla_ref[...]) * gate[:, hc]).astype(BF16))
    y = (yf_ref[...] + yb_ref[...]) * _silu(z_ref[...])
    for g in range(SSD_GROUPS):
        gw = slice(g * SSD_GROUP_WIDTH, (g + 1) * SSD_GROUP_WIDTH)
        parts.append(_rms(y[:, gw], gssd_ref[:, gw]).astype(BF16))
    acc = None
    col = 0
    for p in parts:
        term = _mm(p, w_ref[col:col + p.shape[1], :])
        acc = term if acc is None else acc + term
        col += p.shape[1]
    o_ref[...] = x_ref[...] + mod_ref[0, 2:3, :] * _rms(acc, gpost_ref[...])


def _out_proj(o_f, o_b, y_f, y_b, proj_main, x_all, mods, g_gla, g_ssd, g_post, w_out):
    tm = TM_OUT
    row = lambda i: (i, 0)
    const = lambda i: (0, 0)
    return pl.pallas_call(
        _outproj_kernel,
        out_shape=jax.ShapeDtypeStruct((M_TOK, D_MODEL), F32),
        grid=(M_TOK // tm,),
        in_specs=[pl.BlockSpec((tm, GLA_WIDTH), row),
                  pl.BlockSpec((tm, GLA_WIDTH), row),
                  pl.BlockSpec((tm, GLA_WIDTH), lambda i: (i, C_GOUT // GLA_WIDTH)),
                  pl.BlockSpec((tm, SSD_WIDTH), row),
                  pl.BlockSpec((tm, SSD_WIDTH), row),
                  pl.BlockSpec((tm, SSD_WIDTH), lambda i: (i, C_Z // SSD_WIDTH)),
                  pl.BlockSpec((tm, D_MODEL), row),
                  pl.BlockSpec((1, SUBLANES, D_MODEL), lambda i: (i * tm // ROWS_PER_MOD, 0, 0)),
                  pl.BlockSpec((1, GLA_DV), const),
                  pl.BlockSpec((1, SSD_WIDTH), const),
                  pl.BlockSpec((1, D_MODEL), const),
                  pl.BlockSpec((D_MODEL, D_MODEL), const)],
        out_specs=pl.BlockSpec((tm, D_MODEL), row),
        compiler_params=_params(("arbitrary",)),
        name="out_proj",
    )(o_f, o_b, proj_main, y_f, y_b, proj_main, x_all, mods, g_gla, g_ssd, g_post, w_out)


def _router_kernel(x_ref, mod_ref, g_ref, wr_hi_ref, wr_lo_ref, br_ref,
                   hp_ref, idx_ref, wt_ref, rank_ref, cnt_ref, cnt_scr):
    i = pl.program_id(0)
    tm = TM_ROUTE

    @pl.when(i == 0)
    def _():
        cnt_scr[...] = jnp.zeros_like(cnt_scr)

    h = _rms(x_ref[...], g_ref[...]) * (1.0 + mod_ref[0, 4:5, :]) + mod_ref[0, 3:4, :]
    hp_ref[...] = _pack_bf16_pair(h)
    h_hi, h_lo = _split2(h)
    wr_hi = wr_hi_ref[...]
    logits = _mm_nt(wr_hi, h_hi) + _mm_nt(wr_hi, h_lo) + _mm_nt(wr_lo_ref[...], h_hi)
    scores = _sigmoid(logits)
    sel = scores + br_ref[...]
    neg = -jnp.inf

    def first_argmax(x, ids, n):
        m = jnp.max(x, axis=0, keepdims=True)
        return m, jnp.min(jnp.where(x == m, ids, float(n)), axis=0, keepdims=True)

    ids_g = lax.broadcasted_iota(jnp.int32, (GROUP_SIZE, tm), 0).astype(F32)
    grp = []
    for g in range(N_EXPERT_GROUPS):
        xg = sel[g * GROUP_SIZE:(g + 1) * GROUP_SIZE]
        m1, a1 = first_argmax(xg, ids_g, GROUP_SIZE)
        m2 = jnp.max(jnp.where(ids_g == a1, neg, xg), axis=0, keepdims=True)
        grp.append(m1 + m2)
    gsc = jnp.concatenate(grp, axis=0)
    ids_8 = lax.broadcasted_iota(jnp.int32, (N_EXPERT_GROUPS, tm), 0).astype(F32)
    keep = jnp.zeros((N_EXPERT_GROUPS, tm), F32)
    for _ in range(TOPK_GROUPS):
        _, a = first_argmax(gsc, ids_8, N_EXPERT_GROUPS)
        pick = ids_8 == a
        keep = jnp.where(pick, 1.0, keep)
        gsc = jnp.where(pick, neg, gsc)
    selm = jnp.concatenate(
        [jnp.where(keep[g:g + 1] > 0.5, sel[g * GROUP_SIZE:(g + 1) * GROUP_SIZE], neg)
         for g in range(N_EXPERT_GROUPS)], axis=0)

    ids_e = lax.broadcasted_iota(jnp.int32, (N_EXPERTS, tm), 0).astype(F32)
    picks, wts = [], []
    chosen = jnp.zeros((N_EXPERTS, tm), F32)
    for _ in range(TOP_K):
        _, a = first_argmax(selm, ids_e, N_EXPERTS)
        hit = ids_e == a
        picks.append(a)
        wts.append(jnp.sum(jnp.where(hit, scores, 0.0), axis=0, keepdims=True))
        chosen = jnp.where(hit, 1.0, chosen)
        selm = jnp.where(hit, neg, selm)
    w = jnp.concatenate(wts, axis=0)
    w = w / jnp.sum(w, axis=0, keepdims=True) * ROUTED_SCALE
    idx_ref[...] = jnp.concatenate(picks, axis=0).astype(jnp.int32)
    wt_ref[...] = jnp.concatenate([w, jnp.zeros((LANES - TOP_K, tm), F32)], axis=0).T

    t_r = lax.broadcasted_iota(jnp.int32, (tm, tm), 0)
    t_c = lax.broadcasted_iota(jnp.int32, (tm, tm), 1)
    before = jnp.where(t_r < t_c, 1.0, 0.0).astype(BF16)
    base = _mm(chosen.astype(BF16), before) + cnt_scr[...]
    rank_ref[...] = jnp.concatenate(
        [jnp.sum(jnp.where(ids_e == a, base, 0.0), axis=0, keepdims=True) for a in picks],
        axis=0).astype(jnp.int32)
    cnt_scr[...] = cnt_scr[...] + jnp.sum(chosen, axis=1, keepdims=True)
    cnt_ref[...] = jnp.broadcast_to(cnt_scr[...], cnt_ref.shape)


def _router(x1, mods, g_pre, wr_hi, wr_lo, b_router):
    tm = TM_ROUTE
    const = lambda i: (0, 0)
    return pl.pallas_call(
        _router_kernel,
        out_shape=(jax.ShapeDtypeStruct((M_TOK, HALF_D), jnp.uint32),
                   jax.ShapeDtypeStruct((TOP_K, M_TOK), jnp.int32),
                   jax.ShapeDtypeStruct((M_TOK, LANES), F32),
                   jax.ShapeDtypeStruct((TOP_K, M_TOK), jnp.int32),
                   jax.ShapeDtypeStruct((N_EXPERTS, LANES), F32)),
        grid=(M_TOK // tm,),
        in_specs=[pl.BlockSpec((tm, D_MODEL), lambda i: (i, 0)),
                  pl.BlockSpec((1, SUBLANES, D_MODEL), lambda i: (i * tm // ROWS_PER_MOD, 0, 0)),
                  pl.BlockSpec((1, D_MODEL), const),
                  pl.BlockSpec((N_EXPERTS, D_MODEL), const),
                  pl.BlockSpec((N_EXPERTS, D_MODEL), const),
                  pl.BlockSpec((N_EXPERTS, 1), const)],
        out_specs=(pl.BlockSpec((tm, HALF_D), lambda i: (i, 0)),
                   pl.BlockSpec((TOP_K, tm), lambda i: (0, i)),
                   pl.BlockSpec((tm, LANES), lambda i: (i, 0)),
                   pl.BlockSpec((TOP_K, tm), lambda i: (0, i)),
                   pl.BlockSpec((N_EXPERTS, LANES), const)),
        scratch_shapes=[pltpu.VMEM((N_EXPERTS, 1), F32)],
        compiler_params=_params(("arbitrary",)),
        name="router",
    )(x1, mods, g_pre, wr_hi, wr_lo, b_router)


def _dispatch_kernel(fill_start_ref, fill_len_ref, nb_ref, dest_ref, hp_hbm, xs_hbm, zero_scr, sem, zsem):
    base = pl.program_id(0) * T_DISPATCH

    @pl.when(pl.program_id(0) == 0)
    def _():
        zero_scr[...] = jnp.zeros_like(zero_scr)

        def for_each_fill(act):
            def pad_body(e, carry):
                start = fill_start_ref[e]
                length = fill_len_ref[e]
                head = jnp.minimum((-start) & (SUBLANES - 1), length)
                for j in range(SUBLANES - 1):
                    @pl.when(j < head)
                    def _():
                        act(pltpu.make_async_copy(zero_scr.at[pl.ds(0, 1)],
                                                  xs_hbm.at[pl.ds(start + j, 1)], zsem))
                body_start = start + head
                body_len = length - head
                for bit in (64, 32, 16, 8):
                    @pl.when((body_len & bit) != 0)
                    def _():
                        off = pl.multiple_of(body_start + (body_len & jnp.int32(~(2 * bit - 1))), SUBLANES)
                        act(pltpu.make_async_copy(zero_scr.at[pl.ds(0, bit)],
                                                  xs_hbm.at[pl.ds(off, bit)], zsem))
                return carry

            def tail_body(b, carry):
                act(pltpu.make_async_copy(zero_scr, xs_hbm.at[pl.ds(b * MOE_BLK, MOE_BLK)], zsem))
                return carry

            lax.fori_loop(0, N_EXPERTS, pad_body, 0)
            lax.fori_loop(nb_ref[0], N_MOE_BLOCKS, tail_body, 0)

        for_each_fill(lambda cp: cp.start())
        for_each_fill(lambda cp: cp.wait())

    def body(t, carry):
        for k in range(TOP_K):
            pltpu.make_async_copy(hp_hbm.at[pl.ds(base + t, 1)],
                                  xs_hbm.at[pl.ds(dest_ref[k, t], 1)], sem).start()
        return carry

    lax.fori_loop(0, T_DISPATCH, body, 0)
    pltpu.make_async_copy(xs_hbm.at[pl.ds(0, T_DISPATCH * TOP_K)],
                          xs_hbm.at[pl.ds(0, T_DISPATCH * TOP_K)], sem).wait()


def _dispatch(fill_start, fill_len, n_blk, dest, h_packed):
    return pl.pallas_call(
        _dispatch_kernel,
        out_shape=jax.ShapeDtypeStruct((N_SLOTS, HALF_D), jnp.uint32),
        grid_spec=pltpu.PrefetchScalarGridSpec(
            num_scalar_prefetch=3,
            grid=(M_TOK // T_DISPATCH,),
            in_specs=[pl.BlockSpec((TOP_K, T_DISPATCH), lambda i, *_: (0, i), memory_space=pltpu.SMEM),
                      pl.BlockSpec(memory_space=pl.ANY)],
            out_specs=pl.BlockSpec(memory_space=pl.ANY),
            scratch_shapes=[pltpu.VMEM((MOE_BLK, HALF_D), jnp.uint32),
                            pltpu.SemaphoreType.DMA, pltpu.SemaphoreType.DMA]),
        compiler_params=_params(("arbitrary",)),
        name="moe_dispatch",
    )(fill_start, fill_len, n_blk, dest, h_packed)


def _expert_kernel(be_ref, bs_ref, nb_ref, x_ref, wg_ref, wu_ref, wd_ref, y_ref,
                   wg_scr, wu_scr, wd_scr):
    b = pl.program_id(0)

    @pl.when(b < nb_ref[0])
    def _():
        @pl.when((b == 0) | (be_ref[b] != be_ref[jnp.maximum(b - 1, 0)]))
        def _():
            wg_scr[...] = wg_ref[0].astype(BF16)
            wu_scr[...] = wu_ref[0].astype(BF16)
            wd_scr[...] = wd_ref[0].astype(BF16)

        x_a, x_b = _unpack_bf16_pair(x_ref[...])
        gate = _mm(x_a, wg_scr[:HALF_D, :]) + _mm(x_b, wg_scr[HALF_D:, :])
        up = _mm(x_a, wu_scr[:HALF_D, :]) + _mm(x_b, wu_scr[HALF_D:, :])
        y_ref[...] = _mm((_silu(gate) * up).astype(BF16), wd_scr[...])

    @pl.when(b >= nb_ref[0])
    def _():
        y_ref[...] = jnp.zeros_like(y_ref)


def _experts(blk_expert, blk_src, n_blk, x_sorted, w_g, w_u, w_d):
    return pl.pallas_call(
        _expert_kernel,
        out_shape=jax.ShapeDtypeStruct((N_SLOTS, D_MODEL), F32),
        grid_spec=pltpu.PrefetchScalarGridSpec(
            num_scalar_prefetch=3,
            grid=(N_MOE_BLOCKS,),
            in_specs=[pl.BlockSpec((MOE_BLK, HALF_D), lambda b, be, bs, nb: (bs[b], 0)),
                      pl.BlockSpec((1, D_MODEL, EXPERT_FF), lambda b, be, bs, nb: (be[b], 0, 0)),
                      pl.BlockSpec((1, D_MODEL, EXPERT_FF), lambda b, be, bs, nb: (be[b], 0, 0)),
                      pl.BlockSpec((1, EXPERT_FF, D_MODEL), lambda b, be, bs, nb: (be[b], 0, 0))],
            out_specs=pl.BlockSpec((MOE_BLK, D_MODEL), lambda b, be, bs, nb: (b, 0)),
            scratch_shapes=[pltpu.VMEM((D_MODEL, EXPERT_FF), BF16),
                            pltpu.VMEM((D_MODEL, EXPERT_FF), BF16),
                            pltpu.VMEM((EXPERT_FF, D_MODEL), BF16)]),
        compiler_params=_params(("arbitrary",)),
        name="moe_experts",
    )(blk_expert, blk_src, n_blk, x_sorted, w_g, w_u, w_d)


def _combine_copy(y_hbm, buf, sem, slot, k, t, src_row):
    return pltpu.make_async_copy(y_hbm.at[pl.ds(src_row, 1)], buf.at[slot, k, pl.ds(t, 1)], sem.at[slot])


def _combine_kernel(dest_ref, dest_next_ref, y_hbm, wt_ref, hp_ref, x_ref, mod_ref, g_ref,
                    wsg_ref, wsu_ref, wsd_ref, o_ref, buf, sem):
    i = pl.program_id(0)
    n = pl.num_programs(0)
    slot = i % 2

    def issue(d_ref, s):
        def body(t, carry):
            for k in range(TOP_K):
                _combine_copy(y_hbm, buf, sem, s, k, t, d_ref[k, t]).start()
            return carry
        lax.fori_loop(0, T_COMBINE, body, 0)

    @pl.when(i == 0)
    def _():
        issue(dest_ref, 0)

    @pl.when(i + 1 < n)
    def _():
        issue(dest_next_ref, 1 - slot)

    pltpu.make_async_copy(buf.at[slot], buf.at[slot], sem.at[slot]).wait()

    wt = wt_ref[...]
    acc = buf[slot, 0] * wt[:, 0:1]
    for k in range(1, TOP_K):
        acc = acc + buf[slot, k] * wt[:, k:k + 1]
    h_a, h_b = _unpack_bf16_pair(hp_ref[...])
    gate = _mm(h_a, wsg_ref[:HALF_D, :]) + _mm(h_b, wsg_ref[HALF_D:, :])
    up = _mm(h_a, wsu_ref[:HALF_D, :]) + _mm(h_b, wsu_ref[HALF_D:, :])
    y = acc + _mm((_silu(gate) * up).astype(BF16), wsd_ref[...])
    o_ref[...] = x_ref[...] + mod_ref[0, 5:6, :] * _rms(y, g_ref[...])


def _combine(dest, y_sorted, wt_tok, h_packed, x1, mods, g_post, ws_g, ws_u, ws_d):
    tc = T_COMBINE
    n = M_TOK // tc
    row = lambda i: (i, 0)
    const = lambda i: (0, 0)
    return pl.pallas_call(
        _combine_kernel,
        out_shape=jax.ShapeDtypeStruct((M_TOK, D_MODEL), F32),
        grid=(n,),
        in_specs=[pl.BlockSpec((TOP_K, tc), lambda i: (0, i), memory_space=pltpu.SMEM),
                  pl.BlockSpec((TOP_K, tc), lambda i: (0, jnp.minimum(i + 1, n - 1)),
                               memory_space=pltpu.SMEM),
                  pl.BlockSpec(memory_space=pl.ANY),
                  pl.BlockSpec((tc, LANES), row),
                  pl.BlockSpec((tc, HALF_D), row),
                  pl.BlockSpec((tc, D_MODEL), row),
                  pl.BlockSpec((1, SUBLANES, D_MODEL), lambda i: (i * tc // ROWS_PER_MOD, 0, 0)),
                  pl.BlockSpec((1, D_MODEL), const),
                  pl.BlockSpec((D_MODEL, EXPERT_FF), const),
                  pl.BlockSpec((D_MODEL, EXPERT_FF), const),
                  pl.BlockSpec((EXPERT_FF, D_MODEL), const)],
        out_specs=pl.BlockSpec((tc, D_MODEL), row),
        scratch_shapes=[pltpu.VMEM((2, TOP_K, tc, D_MODEL), F32),
                        pltpu.SemaphoreType.DMA((2,))],
        compiler_params=_params(("arbitrary",)),
        name="moe_combine",
    )(dest, dest, y_sorted, wt_tok, h_packed, x1, mods, g_post, ws_g, ws_u, ws_d)


def _grid_pos_embed(n_tokens):
    rows = n_tokens // GRID_W
    half = D_MODEL // 2
    quarter = half // 2
    omega = 1.0 / (10000.0 ** (jnp.arange(quarter, dtype=F32) / quarter))

    def axis_embed(pos):
        ang = pos.astype(F32)[:, None] * omega
        return jnp.concatenate([jnp.sin(ang), jnp.cos(ang)], axis=-1)

    e_row = axis_embed(jnp.arange(rows))
    e_col = axis_embed(jnp.arange(GRID_W))
    emb = jnp.concatenate([jnp.broadcast_to(e_row[:, None], (rows, GRID_W, half)),
                           jnp.broadcast_to(e_col[None], (rows, GRID_W, half))], axis=-1)
    return emb.reshape(rows * GRID_W, D_MODEL)


def _lane_row(v, lane0):
    return jnp.zeros((1, LANES), F32).at[0, lane0:lane0 + v.shape[0]].set(v)


def kernel(x_prompt, x_sample, state_gla, state_ssd, c, c_ctx, w_ada, b_ada, g_mix_pre, g_mix_post, w_in, w_gk_up, b_gk, g_gla_norm, conv_w, conv_b, dt_bias, a_log, d_skip, g_ssd_norm, w_out, g_ffn_pre, g_ffn_post, w_router, b_router, w_exp_gate, w_exp_up, w_exp_down, w_sh_gate, w_sh_up, w_sh_down):
    assert x_prompt.shape == (N_CTX_SEQ, CTX_LEN, D_MODEL) and x_sample.shape == (N_LAT_SEQ, LAT_LEN, D_MODEL)
    assert w_ada.shape[0] == 1, "single layer"
    l = 0

    w_in_l = w_in[l]
    o_q = 0
    o_k = o_q + GLA_KEY_WIDTH
    o_v = o_k + GLA_KEY_WIDTH
    o_gout = o_v + GLA_WIDTH
    o_lr = o_gout + GLA_WIDTH
    o_z = o_lr + N_DIR * GLA_GATE_RANK
    o_xbc = o_z + SSD_WIDTH
    o_dt = o_xbc + SSD_CONV_CH
    o_end = o_dt + N_DIR * SSD_HEADS
    assert o_end == w_in_l.shape[1]
    w_main = jnp.concatenate([w_in_l[:, o_gout:o_lr], w_in_l[:, o_z:o_xbc], w_in_l[:, o_v:o_gout],
                              w_in_l[:, o_xbc:o_dt], w_in_l[:, o_q:o_k], w_in_l[:, o_k:o_v]],
                             axis=1).astype(BF16)
    w_small = jnp.concatenate([w_in_l[:, o_lr:o_z], w_in_l[:, o_dt:o_end],
                               jnp.zeros((D_MODEL, LANES - SM_DT - N_DIR * SSD_HEADS), F32)], axis=1)
    w_out_bf = w_out[l].astype(BF16)
    wr_t = w_router[l].T
    wr_hi = wr_t.astype(BF16)
    wr_lo = (wr_t - wr_hi.astype(F32)).astype(BF16)
    conv_w8 = jnp.zeros((SUBLANES, SSD_CONV_CH), F32).at[:conv_w.shape[1]].set(conv_w[l])
    d_skip_x = jnp.repeat(d_skip[l], SSD_HEAD_DIM)[None, :]
    g_gla = g_gla_norm[l][None, :]
    g_ssd = g_ssd_norm[l][None, :]

    cvecs = jnp.zeros((SUBLANES, D_MODEL), F32).at[0].set(c_ctx).at[1:1 + N_LAT_SEQ].set(c)
    mod_flat = _ada_mod(cvecs, w_ada[l], b_ada[l][None, :])
    mods = jnp.zeros((N_MOD, SUBLANES, D_MODEL), F32).at[:, :6].set(
        mod_flat[:N_MOD].reshape(N_MOD, 6, D_MODEL))

    x_all = jnp.concatenate([x_prompt.reshape(N_CTX_TOK, D_MODEL),
                             (x_sample + _grid_pos_embed(LAT_LEN)).reshape(-1, D_MODEL)], axis=0)

    proj_main, proj_small = _in_proj(x_all, mods, g_mix_pre[l][None, :], w_main, w_small)

    o_dir, s_gla_dir, y_dir, s_ssd_dir = [], [], [], []
    for d in range(N_DIR):
        rev = d == 1
        wgk_pad = jnp.zeros((LANES, GLA_KEY_WIDTH), F32).at[
            SM_LR + d * GLA_GATE_RANK:SM_LR + (d + 1) * GLA_GATE_RANK].set(w_gk_up[l, d])
        o_d, s_d = _gla_scan(proj_main, proj_small, wgk_pad, b_gk[l, d][None, :],
                             state_gla[:, l, d], rev)
        o_dir.append(o_d)
        s_gla_dir.append(s_d)
        lane0 = SM_DT + d * SSD_HEADS
        dtb_r = _lane_row(dt_bias[l, d], lane0)
        nea_r = _lane_row(-jnp.exp(a_log[l, d]), lane0)
        s0_ssd = state_ssd[:, l, d].reshape(N_LAT_SEQ, SSD_GROUPS, SSD_GROUP_WIDTH, SSD_STATE)
        y_d, t_d = _ssd_scan(proj_main, proj_small, conv_w8, conv_b[l][None, :],
                             dtb_r, nea_r, dtb_r.T, nea_r.T, d_skip_x, s0_ssd, rev, d)
        y_dir.append(y_d)
        s_ssd_dir.append(t_d.reshape(N_CTX_SEQ, SSD_HEADS, SSD_HEAD_DIM, SSD_STATE))
    new_state_gla = jnp.stack(s_gla_dir, axis=1)[:, None]
    new_state_ssd = jnp.stack(s_ssd_dir, axis=1)[:, None]

    x1 = _out_proj(o_dir[0], o_dir[1], y_dir[0], y_dir[1], proj_main, x_all, mods,
                   g_gla, g_ssd, g_mix_post[l][None, :], w_out_bf)

    h_packed, idx, wt_tok, rank, cnt = _router(x1, mods, g_ffn_pre[l][None, :], wr_hi, wr_lo,
                                               b_router[l][:, None])
    counts = cnt[:, 0].astype(jnp.int32)
    padded = (counts + MOE_BLK - 1) // MOE_BLK * MOE_BLK
    pad_end = jnp.cumsum(padded)
    pad_start = pad_end - padded
    dest = pad_start[idx] + rank
    n_blk = pad_end[-1] // MOE_BLK
    blk_ids = jnp.arange(N_MOE_BLOCKS, dtype=jnp.int32)
    blk_src = jnp.minimum(blk_ids, n_blk - 1)
    blk_expert = jnp.minimum(jnp.searchsorted(pad_end, blk_src * MOE_BLK, side='right'),
                             N_EXPERTS - 1).astype(jnp.int32)
    n_blk_arr = n_blk.astype(jnp.int32)[None]
    x_sorted = _dispatch((pad_start + counts).astype(jnp.int32), (padded - counts).astype(jnp.int32),
                         n_blk_arr, dest, h_packed)
    y_sorted = _experts(blk_expert, blk_src.astype(jnp.int32), n_blk_arr,
                        x_sorted, w_exp_gate[l], w_exp_up[l], w_exp_down[l])
    out = _combine(dest, y_sorted, wt_tok, h_packed, x1, mods, g_ffn_post[l][None, :],
                   w_sh_gate[l].astype(BF16), w_sh_up[l].astype(BF16), w_sh_down[l].astype(BF16))

    y_prompt = out[:N_CTX_TOK].reshape(N_CTX_SEQ, CTX_LEN, D_MODEL)
    y_sample = out[N_CTX_TOK:].reshape(N_LAT_SEQ, LAT_LEN, D_MODEL)
    return (y_prompt, y_sample, new_state_gla, new_state_ssd)
```

```python
import functools

import numpy as np
import jax
import jax.numpy as jnp
from jax import lax
from jax.experimental import pallas as pl
from jax.experimental.pallas import tpu as pltpu

F32 = jnp.float32
BF16 = jnp.bfloat16

D_MODEL = 2048
N_CTX_SEQ = 16
CTX_LEN = 256
N_LAT_SEQ = 2
LAT_LEN = 4096
GRID_W = 64
EPS = 1e-6
N_CTX_TOK = N_CTX_SEQ * CTX_LEN
M_TOK = N_CTX_TOK + N_LAT_SEQ * LAT_LEN
SEQ_LENS = (CTX_LEN,) * N_CTX_SEQ + (LAT_LEN,) * N_LAT_SEQ
ROWS_PER_MOD = 4096
N_MOD = M_TOK // ROWS_PER_MOD

GLA_HEADS = 4
GLA_DK = 128
GLA_DV = 256
GLA_KEY_WIDTH = GLA_HEADS * GLA_DK
GLA_WIDTH = GLA_HEADS * GLA_DV
GLA_GATE_RANK = 16
GLA_GATE_TAU = 16.0
GLA_CHUNK = 64
GLA_ROWS = 256

SSD_HEADS = 16
SSD_HEAD_DIM = 64
SSD_GROUPS = 2
SSD_HPG = SSD_HEADS // SSD_GROUPS
SSD_STATE = 128
SSD_WIDTH = SSD_HEADS * SSD_HEAD_DIM
SSD_GROUP_WIDTH = SSD_WIDTH // SSD_GROUPS
SSD_CHUNK = 128
SSD_CONV_CH = SSD_WIDTH + 2 * SSD_GROUPS * SSD_STATE
N_DIR = 2

N_EXPERTS = 256
TOP_K = 8
N_EXPERT_GROUPS = 8
GROUP_SIZE = N_EXPERTS // N_EXPERT_GROUPS
TOPK_GROUPS = 4
EXPERT_FF = 512
ROUTED_SCALE = 2.5

C_GOUT = 0
C_Z = C_GOUT + GLA_WIDTH
C_V = C_Z + SSD_WIDTH
C_XBC = C_V + GLA_WIDTH
C_Q = C_XBC + SSD_CONV_CH
C_K = C_Q + GLA_KEY_WIDTH
MAIN_WIDTH = C_K + GLA_KEY_WIDTH
LANES = 128
SUBLANES = 8
SM_LR = 0
SM_DT = N_DIR * GLA_GATE_RANK

TM_PROJ = 1024
TN_PROJ = 512
TM_OUT = 256
TM_ROUTE = 256
MOE_BLK = 128
N_PAIRS = M_TOK * TOP_K
N_MOE_BLOCKS = N_PAIRS // MOE_BLK + N_EXPERTS
N_SLOTS = N_MOE_BLOCKS * MOE_BLK
T_DISPATCH = 512
T_COMBINE = 128
ADA_TN = 1024
VMEM_LIMIT = 52 * 1024 * 1024


def _mm(a, b):
    return jnp.dot(a, b, preferred_element_type=F32)


def _mm_nt(a, b):
    return lax.dot_general(a, b, (((1,), (1,)), ((), ())), preferred_element_type=F32)


def _mm_tn(a, b):
    return lax.dot_general(a, b, (((0,), (0,)), ((), ())), preferred_element_type=F32)


def _split2(x):
    hi = x.astype(BF16)
    lo = (x - hi.astype(F32)).astype(BF16)
    return hi, lo


def _split3(x):
    hi = x.astype(BF16)
    r = x - hi.astype(F32)
    mid = r.astype(BF16)
    lo = (r - mid.astype(F32)).astype(BF16)
    return hi, mid, lo


def _mm_x3(a, b):
    a_hi, a_lo = _split2(a)
    b_hi, b_lo = _split2(b)
    return _mm(a_hi, b_hi) + _mm(a_lo, b_hi) + _mm(a_hi, b_lo)


def _mm_sel_left(sel_bf, x):
    hi, mid, lo = _split3(x)
    return _mm(sel_bf, hi) + _mm(sel_bf, mid) + _mm(sel_bf, lo)


def _mm_sel_right(x, sel_bf):
    hi, mid, lo = _split3(x)
    return _mm(hi, sel_bf) + _mm(mid, sel_bf) + _mm(lo, sel_bf)


def _sigmoid(x):
    return 1.0 / (1.0 + jnp.exp(-x))


def _silu(x):
    return x * _sigmoid(x)


def _softplus(x):
    return jnp.maximum(x, 0.0) + jnp.log1p(jnp.exp(-jnp.abs(x)))


def _rms(x, g):
    return x * lax.rsqrt(jnp.mean(x * x, axis=-1, keepdims=True) + EPS) * g


def _params(sem, vmem=VMEM_LIMIT):
    return pltpu.CompilerParams(dimension_semantics=sem, vmem_limit_bytes=vmem)


def _ada_kernel(c_ref, w_ref, b_ref, o_ref):
    o_ref[...] = _mm_x3(_silu(c_ref[...]), w_ref[...]) + b_ref[...]


def _ada_mod(cvecs, w_ada, b_ada):
    n_out = w_ada.shape[1]
    return pl.pallas_call(
        _ada_kernel,
        out_shape=jax.ShapeDtypeStruct((SUBLANES, n_out), F32),
        grid=(n_out // ADA_TN,),
        in_specs=[pl.BlockSpec((SUBLANES, D_MODEL), lambda j: (0, 0)),
                  pl.BlockSpec((D_MODEL, ADA_TN), lambda j: (0, j)),
                  pl.BlockSpec((1, ADA_TN), lambda j: (0, j))],
        out_specs=pl.BlockSpec((SUBLANES, ADA_TN), lambda j: (0, j)),
        compiler_params=_params(("arbitrary",)),
        name="ada_mod",
    )(cvecs, w_ada, b_ada)


def _token_specs(tm, buffers=2):
    n_ctx = N_CTX_TOK // tm
    n_pos = LAT_LEN // tm
    mode = dict(pipeline_mode=pl.Buffered(buffers)) if buffers != 2 else {}
    ctx = pl.BlockSpec((tm, D_MODEL), lambda i, *_: (jnp.minimum(i, n_ctx - 1), 0), **mode)
    lat = pl.BlockSpec((tm, D_MODEL), lambda i, *_: (jnp.maximum(i - n_ctx, 0), 0), **mode)
    pos = pl.BlockSpec((tm, D_MODEL), lambda i, *_: (jnp.maximum(i - n_ctx, 0) % n_pos, 0), **mode)
    return n_ctx, [ctx, lat, pos]


def _inproj_kernel(xc_ref, xl_ref, pos_ref, mod_ref, g_ref, w_ref, ws_ref, o_ref, os_ref, h_scr, *, n_ctx):
    def prologue(x):
        h = _rms(x, g_ref[...]) * (1.0 + mod_ref[0, 1:2, :]) + mod_ref[0, 0:1, :]
        h_hi, h_lo = _split2(h)
        h_scr[...] = h_hi
        ws_hi, ws_lo = _split2(ws_ref[...])
        os_ref[...] = _mm(h_hi, ws_hi) + _mm(h_lo, ws_hi) + _mm(h_hi, ws_lo)

    first = pl.program_id(1) == 0
    is_ctx = pl.program_id(0) < n_ctx

    @pl.when(first & is_ctx)
    def _():
        prologue(xc_ref[...])

    @pl.when(first & jnp.logical_not(is_ctx))
    def _():
        prologue(xl_ref[...] + pos_ref[...])

    o_ref[...] = _mm(h_scr[...], w_ref[...])


def _in_proj(x_ctx, x_lat, pos, mods, g_pre, w_main, w_small):
    tm, tn = TM_PROJ, TN_PROJ
    n_ctx, tok_specs = _token_specs(tm, buffers=1)
    return pl.pallas_call(
        functools.partial(_inproj_kernel, n_ctx=n_ctx),
        out_shape=(jax.ShapeDtypeStruct((M_TOK, MAIN_WIDTH), F32),
                   jax.ShapeDtypeStruct((M_TOK, LANES), F32)),
        grid=(M_TOK // tm, MAIN_WIDTH // tn),
        in_specs=tok_specs + [
                  pl.BlockSpec((1, SUBLANES, D_MODEL), lambda i, j: (i * tm // ROWS_PER_MOD, 0, 0)),
                  pl.BlockSpec((1, D_MODEL), lambda i, j: (0, 0)),
                  pl.BlockSpec((D_MODEL, tn), lambda i, j: (0, j)),
                  pl.BlockSpec((D_MODEL, LANES), lambda i, j: (0, 0))],
        out_specs=(pl.BlockSpec((tm, tn), lambda i, j: (i, j)),
                   pl.BlockSpec((tm, LANES), lambda i, j: (i, 0))),
        scratch_shapes=[pltpu.VMEM((tm, D_MODEL), BF16)],
        compiler_params=_params(("arbitrary", "arbitrary")),
        name="in_proj",
    )(x_ctx, x_lat, pos, mods, g_pre, w_main, w_small)


def _scan_schedule(rows_per_step, reverse):
    blk, flag, s0i, soi, emit, has_prev, has_next = [], [], [], [], [], [], []
    start = 0
    for s, length in enumerate(SEQ_LENS):
        nb = length // rows_per_step
        is_ctx = s < N_CTX_SEQ
        order = range(nb - 1, -1, -1) if reverse else range(nb)
        for n, b in enumerate(order):
            blk.append(start + b)
            flag.append((1 if is_ctx else 2) if n == 0 else 0)
            s0i.append(0 if is_ctx else s - N_CTX_SEQ)
            soi.append(s if is_ctx else N_CTX_SEQ - 1)
            emit.append(1 if (is_ctx and n == nb - 1) else 0)
            has_prev.append(1 if b > 0 else 0)
            has_next.append(1 if b < nb - 1 else 0)
        start += nb
    return tuple(jnp.asarray(np.array(a, np.int32)) for a in (blk, flag, s0i, soi, emit, has_prev, has_next))


def _gla_kernel(blk_ref, flag_ref, s0i_ref, soi_ref, emit_ref,
                q_ref, k_ref, v_ref, sm_ref, wgk_ref, bgk_ref, s0_ref,
                o_ref, so_ref, st_scr, *, reverse):
    i = pl.program_id(0)
    flag = flag_ref[i]

    @pl.when(flag == 1)
    def _():
        st_scr[...] = jnp.zeros_like(st_scr)

    @pl.when(flag == 2)
    def _():
        for h in range(GLA_HEADS):
            st_scr[h] = s0_ref[0, h].T

    c = GLA_CHUNK
    r_id = lax.broadcasted_iota(jnp.int32, (c, c), 0)
    c_id = lax.broadcasted_iota(jnp.int32, (c, c), 1)
    tri = (c_id >= r_id) if reverse else (c_id <= r_id)
    tri_bf = jnp.where(tri, 1.0, 0.0).astype(BF16)

    gk = _mm_x3(sm_ref[...], wgk_ref[...]) + bgk_ref[...]
    log_a = (jnp.minimum(gk, 0.0) - jnp.log1p(jnp.exp(-jnp.abs(gk)))) * (1.0 / GLA_GATE_TAU)

    n_chunks = GLA_ROWS // c
    for ci in (range(n_chunks - 1, -1, -1) if reverse else range(n_chunks)):
        lo = ci * c
        b_all = _mm_sel_left(tri_bf, log_a[lo:lo + c])
        for h in range(GLA_HEADS):
            kc = slice(h * GLA_DK, (h + 1) * GLA_DK)
            vc = slice(h * GLA_DV, (h + 1) * GLA_DV)
            b = b_all[:, kc]
            b_end = b[0:1] if reverse else b[c - 1:c]
            q = q_ref[lo:lo + c, kc] * (GLA_DK ** -0.5)
            k = k_ref[lo:lo + c, kc]
            v = v_ref[lo:lo + c, vc].astype(BF16)
            q_e = (q * jnp.exp(b)).astype(BF16)
            k_e = (k * jnp.exp(-b)).astype(BF16)
            att = jnp.where(tri, _mm_nt(q_e, k_e), 0.0).astype(BF16)
            st = st_scr[h]
            o_ref[lo:lo + c, vc] = _mm(att, v) + _mm_nt(q_e, st.astype(BF16))
            k_end = (k * jnp.exp(b_end - b)).astype(BF16)
            st_scr[h] = st * jnp.exp(b_end) + _mm_tn(v, k_end)

    @pl.when(emit_ref[i] == 1)
    def _():
        for h in range(GLA_HEADS):
            so_ref[0, h] = st_scr[h].T


def _gla_scan(proj_main, proj_small, wgk_pad, bgk, s0, reverse):
    sched = _scan_schedule(GLA_ROWS, reverse)[:5]
    n_steps = M_TOK // GLA_ROWS
    t = GLA_ROWS
    q_blk, k_blk, v_blk = C_Q // GLA_KEY_WIDTH, C_K // GLA_KEY_WIDTH, C_V // GLA_WIDTH
    state_blk = (1, GLA_HEADS, GLA_DK, GLA_DV)
    return pl.pallas_call(
        functools.partial(_gla_kernel, reverse=reverse),
        out_shape=(jax.ShapeDtypeStruct((M_TOK, GLA_WIDTH), F32),
                   jax.ShapeDtypeStruct((N_CTX_SEQ,) + state_blk[1:], F32)),
        grid_spec=pltpu.PrefetchScalarGridSpec(
            num_scalar_prefetch=5,
            grid=(n_steps,),
            in_specs=[
                pl.BlockSpec((t, GLA_KEY_WIDTH), lambda i, blk, *_: (blk[i], q_blk)),
                pl.BlockSpec((t, GLA_KEY_WIDTH), lambda i, blk, *_: (blk[i], k_blk)),
                pl.BlockSpec((t, GLA_WIDTH), lambda i, blk, *_: (blk[i], v_blk)),
                pl.BlockSpec((t, LANES), lambda i, blk, *_: (blk[i], 0)),
                pl.BlockSpec((LANES, GLA_KEY_WIDTH), lambda i, *_: (0, 0)),
                pl.BlockSpec((1, GLA_KEY_WIDTH), lambda i, *_: (0, 0)),
                pl.BlockSpec(state_blk, lambda i, blk, flag, s0i, *_: (s0i[i], 0, 0, 0)),
            ],
            out_specs=(
                pl.BlockSpec((t, GLA_WIDTH), lambda i, blk, *_: (blk[i], 0)),
                pl.BlockSpec(state_blk, lambda i, blk, flag, s0i, soi, *_: (soi[i], 0, 0, 0)),
            ),
            scratch_shapes=[pltpu.VMEM((GLA_HEADS, GLA_DV, GLA_DK), F32)]),
        compiler_params=_params(("arbitrary",)),
        name="gla_bwd" if reverse else "gla_fwd",
    )(*sched, proj_main, proj_main, proj_main, proj_small, wgk_pad, bgk, s0)


def _ssd_kernel(blk_ref, flag_ref, s0i_ref, soi_ref, emit_ref, hp_ref, hn_ref,
                xbc_ref, xprev_ref, xnext_ref, sm_ref, cw_ref, cb_ref,
                dtb_r_ref, nea_r_ref, dtb_c_ref, nea_c_ref, dsk_ref, s0_ref,
                y_ref, so_ref, st_scr, *, reverse, lane0, add_skip):
    i = pl.program_id(0)
    flag = flag_ref[i]
    t = SSD_CHUNK

    @pl.when(flag == 1)
    def _():
        st_scr[...] = jnp.zeros_like(st_scr)

    @pl.when(flag == 2)
    def _():
        for g in range(SSD_GROUPS):
            st_scr[g] = s0_ref[0, g].T

    xbc = xbc_ref[...]
    prev = jnp.where(hp_ref[i] == 1, xprev_ref[SUBLANES - 1:SUBLANES, :], 0.0)
    nxt = jnp.where(hn_ref[i] == 1, xnext_ref[0:1, :], 0.0)
    row = lax.broadcasted_iota(jnp.int32, xbc.shape, 0)
    x_m1 = jnp.where(row == 0, prev, pltpu.roll(xbc, 1, 0))
    x_p1 = jnp.where(row == t - 1, nxt, pltpu.roll(xbc, t - 1, 0))
    act = _silu(x_m1 * cw_ref[0:1, :] + xbc * cw_ref[1:2, :] + x_p1 * cw_ref[2:3, :] + cb_ref[...])
    xs = act[:, :SSD_WIDTH]
    bm = act[:, SSD_WIDTH:SSD_WIDTH + SSD_GROUPS * SSD_STATE]
    cm = act[:, SSD_WIDTH + SSD_GROUPS * SSD_STATE:]

    sm = sm_ref[...]
    dt = _softplus(sm + dtb_r_ref[...])
    a = dt * nea_r_ref[...]
    a_t = _softplus(sm.T + dtb_c_ref[...]) * nea_c_ref[...]

    r_id = lax.broadcasted_iota(jnp.int32, (t, t), 0)
    c_id = lax.broadcasted_iota(jnp.int32, (t, t), 1)
    tri = (c_id >= r_id) if reverse else (c_id <= r_id)
    tri_bf = jnp.where(tri, 1.0, 0.0).astype(BF16)
    tri_t_bf = jnp.where((r_id >= c_id) if reverse else (r_id <= c_id), 1.0, 0.0).astype(BF16)
    cs = _mm_sel_left(tri_bf, a)
    cs_t = _mm_sel_right(a_t, tri_t_bf)

    e_r = lax.broadcasted_iota(jnp.int32, (LANES, SSD_WIDTH), 0)
    e_c = lax.broadcasted_iota(jnp.int32, (LANES, SSD_WIDTH), 1)
    expand = jnp.where(e_r - lane0 == e_c // SSD_HEAD_DIM, 1.0, 0.0).astype(BF16)
    dt_x = _mm_sel_right(dt, expand)
    cs_x = _mm_sel_right(cs, expand)
    cs_end_x = cs_x[0:1] if reverse else cs_x[t - 1:t]
    x_in = xs * dt_x
    x_bf = x_in.astype(BF16)
    x_w = (x_in * jnp.exp(cs_end_x - cs_x)).astype(BF16)
    decay_in = jnp.exp(cs_x)
    decay_end = jnp.exp(cs_end_x)

    for g in range(SSD_GROUPS):
        gc = slice(g * SSD_STATE, (g + 1) * SSD_STATE)
        gw = slice(g * SSD_GROUP_WIDTH, (g + 1) * SSD_GROUP_WIDTH)
        c_g = cm[:, gc].astype(BF16)
        b_g = bm[:, gc].astype(BF16)
        cb = _mm_nt(c_g, b_g)
        st = st_scr[g]
        y_state = _mm(c_g, st.astype(BF16)) * decay_in[:, gw]
        for r in range(SSD_HPG):
            hh = g * SSD_HPG + r
            hc = slice(hh * SSD_HEAD_DIM, (hh + 1) * SSD_HEAD_DIM)
            lane = lane0 + hh
            seg = cs[:, lane:lane + 1] - cs_t[lane:lane + 1, :]
            lm = jnp.exp(jnp.where(tri, seg, -jnp.inf))
            y_h = _mm((cb * lm).astype(BF16), x_bf[:, hc]) + y_state[:, r * SSD_HEAD_DIM:(r + 1) * SSD_HEAD_DIM]
            if add_skip:
                y_h = y_h + xs[:, hc] * dsk_ref[:, hc]
            y_ref[:, hc] = y_h
        st_scr[g] = st * decay_end[:, gw] + _mm_tn(b_g, x_w[:, gw])

    @pl.when(emit_ref[i] == 1)
    def _():
        for g in range(SSD_GROUPS):
            so_ref[0, g] = st_scr[g].T


def _ssd_scan(proj_main, proj_small, conv_w, conv_b, dtb_r, nea_r, dtb_c, nea_c, d_skip_x, s0,
              reverse, direction):
    sched = _scan_schedule(SSD_CHUNK, reverse)
    t = SSD_CHUNK
    n_steps = M_TOK // t
    xbc_blk = C_XBC // SSD_CONV_CH
    rb = t // SUBLANES
    n_rb = M_TOK // SUBLANES
    state_blk = (1, SSD_GROUPS, SSD_GROUP_WIDTH, SSD_STATE)
    return pl.pallas_call(
        functools.partial(_ssd_kernel, reverse=reverse, lane0=SM_DT + direction * SSD_HEADS,
                          add_skip=not reverse),
        out_shape=(jax.ShapeDtypeStruct((M_TOK, SSD_WIDTH), F32),
                   jax.ShapeDtypeStruct((N_CTX_SEQ,) + state_blk[1:], F32)),
        grid_spec=pltpu.PrefetchScalarGridSpec(
            num_scalar_prefetch=7,
            grid=(n_steps,),
            in_specs=[
                pl.BlockSpec((t, SSD_CONV_CH), lambda i, blk, *_: (blk[i], xbc_blk)),
                pl.BlockSpec((SUBLANES, SSD_CONV_CH),
                             lambda i, blk, *_: (jnp.maximum(blk[i] * rb - 1, 0), xbc_blk)),
                pl.BlockSpec((SUBLANES, SSD_CONV_CH),
                             lambda i, blk, *_: (jnp.minimum((blk[i] + 1) * rb, n_rb - 1), xbc_blk)),
                pl.BlockSpec((t, LANES), lambda i, blk, *_: (blk[i], 0)),
                pl.BlockSpec((SUBLANES, SSD_CONV_CH), lambda i, *_: (0, 0)),
                pl.BlockSpec((1, SSD_CONV_CH), lambda i, *_: (0, 0)),
                pl.BlockSpec((1, LANES), lambda i, *_: (0, 0)),
                pl.BlockSpec((1, LANES), lambda i, *_: (0, 0)),
                pl.BlockSpec((LANES, 1), lambda i, *_: (0, 0)),
                pl.BlockSpec((LANES, 1), lambda i, *_: (0, 0)),
                pl.BlockSpec((1, SSD_WIDTH), lambda i, *_: (0, 0)),
                pl.BlockSpec(state_blk, lambda i, blk, flag, s0i, *_: (s0i[i], 0, 0, 0)),
            ],
            out_specs=(
                pl.BlockSpec((t, SSD_WIDTH), lambda i, blk, *_: (blk[i], 0)),
                pl.BlockSpec(state_blk, lambda i, blk, flag, s0i, soi, *_: (soi[i], 0, 0, 0)),
            ),
            scratch_shapes=[pltpu.VMEM((SSD_GROUPS, SSD_STATE, SSD_GROUP_WIDTH), F32)]),
        compiler_params=_params(("arbitrary",)),
        name="ssd_bwd" if reverse else "ssd_fwd",
    )(*sched, proj_main, proj_main, proj_main, proj_small, conv_w, conv_b,
      dtb_r, nea_r, dtb_c, nea_c, d_skip_x, s0)


def _outproj_kernel(of_ref, ob_ref, gout_ref, yf_ref, yb_ref, z_ref, xc_ref, xl_ref, pos_ref, mod_ref,
                    ggla_ref, gssd_ref, gpost_ref, w_ref, o_ref, *, n_ctx):
    o = of_ref[...] + ob_ref[...]
    gate = _silu(gout_ref[...])
    parts = []
    for h in range(GLA_HEADS):
        hc = slice(h * GLA_DV, (h + 1) * GLA_DV)
        parts.append((_rms(o[:, hc], ggla_ref[...]) * gate[:, hc]).astype(BF16))
    y = (yf_ref[...] + yb_ref[...]) * _silu(z_ref[...])
    for g in range(SSD_GROUPS):
        gw = slice(g * SSD_GROUP_WIDTH, (g + 1) * SSD_GROUP_WIDTH)
        parts.append(_rms(y[:, gw], gssd_ref[:, gw]).astype(BF16))
    acc = None
    col = 0
    for p in parts:
        term = _mm(p, w_ref[col:col + p.shape[1], :])
        acc = term if acc is None else acc + term
        col += p.shape[1]
    delta = mod_ref[0, 2:3, :] * _rms(acc, gpost_ref[...])
    is_ctx = pl.program_id(0) < n_ctx

    @pl.when(is_ctx)
    def _():
        o_ref[...] = xc_ref[...] + delta

    @pl.when(jnp.logical_not(is_ctx))
    def _():
        o_ref[...] = xl_ref[...] + pos_ref[...] + delta


def _out_proj(o_f, o_b, y_f, y_b, proj_main, x_ctx, x_lat, pos, mods, g_gla, g_ssd, g_post, w_out):
    tm = TM_OUT
    row = lambda i: (i, 0)
    const = lambda i: (0, 0)
    n_ctx, tok_specs = _token_specs(tm)
    return pl.pallas_call(
        functools.partial(_outproj_kernel, n_ctx=n_ctx),
        out_shape=jax.ShapeDtypeStruct((M_TOK, D_MODEL), F32),
        grid=(M_TOK // tm,),
        in_specs=[pl.BlockSpec((tm, GLA_WIDTH), row),
                  pl.BlockSpec((tm, GLA_WIDTH), row),
                  pl.BlockSpec((tm, GLA_WIDTH), lambda i: (i, C_GOUT // GLA_WIDTH)),
                  pl.BlockSpec((tm, SSD_WIDTH), row),
                  pl.BlockSpec((tm, SSD_WIDTH), row),
                  pl.BlockSpec((tm, SSD_WIDTH), lambda i: (i, C_Z // SSD_WIDTH))] + tok_specs + [
                  pl.BlockSpec((1, SUBLANES, D_MODEL), lambda i: (i * tm // ROWS_PER_MOD, 0, 0)),
                  pl.BlockSpec((1, GLA_DV), const),
                  pl.BlockSpec((1, SSD_WIDTH), const),
                  pl.BlockSpec((1, D_MODEL), const),
                  pl.BlockSpec((D_MODEL, D_MODEL), const)],
        out_specs=pl.BlockSpec((tm, D_MODEL), row),
        compiler_params=_params(("arbitrary",)),
        name="out_proj",
    )(o_f, o_b, proj_main, y_f, y_b, proj_main, x_ctx, x_lat, pos, mods, g_gla, g_ssd, g_post, w_out)


def _router_kernel(x_ref, mod_ref, g_ref, wr_hi_ref, wr_lo_ref, br_ref,
                   h_ref, idx_ref, wt_ref, rank_ref, cnt_ref, cnt_scr):
    i = pl.program_id(0)
    tm = TM_ROUTE

    @pl.when(i == 0)
    def _():
        cnt_scr[...] = jnp.zeros_like(cnt_scr)

    h = _rms(x_ref[...], g_ref[...]) * (1.0 + mod_ref[0, 4:5, :]) + mod_ref[0, 3:4, :]
    h_ref[...] = h
    h_hi, h_lo = _split2(h)
    wr_hi = wr_hi_ref[...]
    logits = _mm_nt(wr_hi, h_hi) + _mm_nt(wr_hi, h_lo) + _mm_nt(wr_lo_ref[...], h_hi)
    scores = _sigmoid(logits)
    sel = scores + br_ref[...]
    neg = -jnp.inf

    def first_argmax(x, ids, n):
        m = jnp.max(x, axis=0, keepdims=True)
        return m, jnp.min(jnp.where(x == m, ids, float(n)), axis=0, keepdims=True)

    ids_g = lax.broadcasted_iota(jnp.int32, (GROUP_SIZE, tm), 0).astype(F32)
    grp = []
    for g in range(N_EXPERT_GROUPS):
        xg = sel[g * GROUP_SIZE:(g + 1) * GROUP_SIZE]
        m1, a1 = first_argmax(xg, ids_g, GROUP_SIZE)
        m2 = jnp.max(jnp.where(ids_g == a1, neg, xg), axis=0, keepdims=True)
        grp.append(m1 + m2)
    gsc = jnp.concatenate(grp, axis=0)
    ids_8 = lax.broadcasted_iota(jnp.int32, (N_EXPERT_GROUPS, tm), 0).astype(F32)
    keep = jnp.zeros((N_EXPERT_GROUPS, tm), F32)
    for _ in range(TOPK_GROUPS):
        _, a = first_argmax(gsc, ids_8, N_EXPERT_GROUPS)
        pick = ids_8 == a
        keep = jnp.where(pick, 1.0, keep)
        gsc = jnp.where(pick, neg, gsc)
    selm = jnp.concatenate(
        [jnp.where(keep[g:g + 1] > 0.5, sel[g * GROUP_SIZE:(g + 1) * GROUP_SIZE], neg)
         for g in range(N_EXPERT_GROUPS)], axis=0)

    ids_e = lax.broadcasted_iota(jnp.int32, (N_EXPERTS, tm), 0).astype(F32)
    picks, wts = [], []
    chosen = jnp.zeros((N_EXPERTS, tm), F32)
    for _ in range(TOP_K):
        _, a = first_argmax(selm, ids_e, N_EXPERTS)
        hit = ids_e == a
        picks.append(a)
        wts.append(jnp.sum(jnp.where(hit, scores, 0.0), axis=0, keepdims=True))
        chosen = jnp.where(hit, 1.0, chosen)
        selm = jnp.where(hit, neg, selm)
    w = jnp.concatenate(wts, axis=0)
    w = w / jnp.sum(w, axis=0, keepdims=True) * ROUTED_SCALE
    idx_ref[...] = jnp.concatenate(picks, axis=0).astype(jnp.int32)
    wt_ref[...] = jnp.concatenate([w, jnp.zeros((LANES - TOP_K, tm), F32)], axis=0).T

    t_r = lax.broadcasted_iota(jnp.int32, (tm, tm), 0)
    t_c = lax.broadcasted_iota(jnp.int32, (tm, tm), 1)
    before = jnp.where(t_r < t_c, 1.0, 0.0).astype(BF16)
    base = _mm(chosen.astype(BF16), before) + cnt_scr[...]
    rank_ref[...] = jnp.concatenate(
        [jnp.sum(jnp.where(ids_e == a, base, 0.0), axis=0, keepdims=True) for a in picks],
        axis=0).astype(jnp.int32)
    cnt_scr[...] = cnt_scr[...] + jnp.sum(chosen, axis=1, keepdims=True)
    cnt_ref[...] = jnp.broadcast_to(cnt_scr[...], cnt_ref.shape)


def _router(x1, mods, g_pre, wr_hi, wr_lo, b_router):
    tm = TM_ROUTE
    const = lambda i: (0, 0)
    return pl.pallas_call(
        _router_kernel,
        out_shape=(jax.ShapeDtypeStruct((M_TOK, D_MODEL), F32),
                   jax.ShapeDtypeStruct((TOP_K, M_TOK), jnp.int32),
                   jax.ShapeDtypeStruct((M_TOK, LANES), F32),
                   jax.ShapeDtypeStruct((TOP_K, M_TOK), jnp.int32),
                   jax.ShapeDtypeStruct((N_EXPERTS, LANES), F32)),
        grid=(M_TOK // tm,),
        in_specs=[pl.BlockSpec((tm, D_MODEL), lambda i: (i, 0)),
                  pl.BlockSpec((1, SUBLANES, D_MODEL), lambda i: (i * tm // ROWS_PER_MOD, 0, 0)),
                  pl.BlockSpec((1, D_MODEL), const),
                  pl.BlockSpec((N_EXPERTS, D_MODEL), const),
                  pl.BlockSpec((N_EXPERTS, D_MODEL), const),
                  pl.BlockSpec((N_EXPERTS, 1), const)],
        out_specs=(pl.BlockSpec((tm, D_MODEL), lambda i: (i, 0)),
                   pl.BlockSpec((TOP_K, tm), lambda i: (0, i)),
                   pl.BlockSpec((tm, LANES), lambda i: (i, 0)),
                   pl.BlockSpec((TOP_K, tm), lambda i: (0, i)),
                   pl.BlockSpec((N_EXPERTS, LANES), const)),
        scratch_shapes=[pltpu.VMEM((N_EXPERTS, 1), F32)],
        compiler_params=_params(("arbitrary",)),
        name="router",
    )(x1, mods, g_pre, wr_hi, wr_lo, b_router)


def _dispatch_kernel(fill_start_ref, fill_len_ref, nb_ref, dest_ref, h_ref, xs_hbm, zero_scr, sem, zsem):
    @pl.when(pl.program_id(0) == 0)
    def _():
        zero_scr[...] = jnp.zeros_like(zero_scr)

        def for_each_fill(act):
            def pad_body(e, carry):
                start = fill_start_ref[e]
                length = fill_len_ref[e]
                head = jnp.minimum((-start) & (SUBLANES - 1), length)
                for j in range(SUBLANES - 1):
                    @pl.when(j < head)
                    def _():
                        act(pltpu.make_async_copy(zero_scr.at[pl.ds(0, 1)],
                                                  xs_hbm.at[pl.ds(start + j, 1)], zsem))
                body_start = start + head
                body_len = length - head
                for bit in (64, 32, 16, 8):
                    @pl.when((body_len & bit) != 0)
                    def _():
                        off = pl.multiple_of(body_start + (body_len & jnp.int32(~(2 * bit - 1))), SUBLANES)
                        act(pltpu.make_async_copy(zero_scr.at[pl.ds(0, bit)],
                                                  xs_hbm.at[pl.ds(off, bit)], zsem))
                return carry

            def tail_body(b, carry):
                act(pltpu.make_async_copy(zero_scr, xs_hbm.at[pl.ds(b * MOE_BLK, MOE_BLK)], zsem))
                return carry

            lax.fori_loop(0, N_EXPERTS, pad_body, 0)
            lax.fori_loop(nb_ref[0], N_MOE_BLOCKS, tail_body, 0)

        for_each_fill(lambda cp: cp.start())
        for_each_fill(lambda cp: cp.wait())

    def body(t, carry):
        for k in range(TOP_K):
            pltpu.make_async_copy(h_ref.at[pl.ds(t, 1)],
                                  xs_hbm.at[pl.ds(dest_ref[k, t], 1)], sem).start()
        return carry

    lax.fori_loop(0, T_DISPATCH, body, 0)
    pltpu.make_async_copy(xs_hbm.at[pl.ds(0, T_DISPATCH * TOP_K)],
                          xs_hbm.at[pl.ds(0, T_DISPATCH * TOP_K)], sem).wait()


def _dispatch(fill_start, fill_len, n_blk, dest, h):
    return pl.pallas_call(
        _dispatch_kernel,
        out_shape=jax.ShapeDtypeStruct((N_SLOTS, D_MODEL), F32),
        grid_spec=pltpu.PrefetchScalarGridSpec(
            num_scalar_prefetch=3,
            grid=(M_TOK // T_DISPATCH,),
            in_specs=[pl.BlockSpec((TOP_K, T_DISPATCH), lambda i, *_: (0, i), memory_space=pltpu.SMEM),
                      pl.BlockSpec((T_DISPATCH, D_MODEL), lambda i, *_: (i, 0))],
            out_specs=pl.BlockSpec(memory_space=pl.ANY),
            scratch_shapes=[pltpu.VMEM((MOE_BLK, D_MODEL), F32),
                            pltpu.SemaphoreType.DMA, pltpu.SemaphoreType.DMA]),
        compiler_params=_params(("arbitrary",)),
        name="moe_dispatch",
    )(fill_start, fill_len, n_blk, dest, h)


def _expert_weight_copies(e, s, w_hbm, w_f32, sem):
    return [pltpu.make_async_copy(w_hbm[j].at[e], w_f32[j].at[s], sem.at[s, j]) for j in range(3)]


def _expert_kernel(be_ref, bs_ref, nb_ref, first_ref, next_ref, slot_ref,
                   x_ref, wg_hbm, wu_hbm, wd_hbm, y_ref,
                   wg_f32, wu_f32, wd_f32, wg_scr, wu_scr, wd_scr, sem):
    b = pl.program_id(0)
    w_hbm = (wg_hbm, wu_hbm, wd_hbm)
    w_f32 = (wg_f32, wu_f32, wd_f32)

    @pl.when(b == 0)
    def _():
        for cp in _expert_weight_copies(be_ref[0], 0, w_hbm, w_f32, sem):
            cp.start()

    @pl.when(b < nb_ref[0])
    def _():
        @pl.when(first_ref[b] == 1)
        def _():
            s = slot_ref[b]
            for cp in _expert_weight_copies(be_ref[b], s, w_hbm, w_f32, sem):
                cp.wait()

            @pl.when(next_ref[b] >= 0)
            def _():
                for cp in _expert_weight_copies(next_ref[b], 1 - s, w_hbm, w_f32, sem):
                    cp.start()

            wg_scr[...] = wg_f32[s].astype(BF16)
            wu_scr[...] = wu_f32[s].astype(BF16)
            wd_scr[...] = wd_f32[s].astype(BF16)

        x = x_ref[...].astype(BF16)
        act = _silu(_mm(x, wg_scr[...])) * _mm(x, wu_scr[...])
        y_ref[...] = _mm(act.astype(BF16), wd_scr[...])

    @pl.when(b >= nb_ref[0])
    def _():
        y_ref[...] = jnp.zeros_like(y_ref)


def _experts(blk_expert, blk_src, n_blk, blk_first, blk_next, blk_slot, x_sorted, w_g, w_u, w_d):
    hbm = pl.BlockSpec(memory_space=pl.ANY)
    return pl.pallas_call(
        _expert_kernel,
        out_shape=jax.ShapeDtypeStruct((N_SLOTS, D_MODEL), F32),
        grid_spec=pltpu.PrefetchScalarGridSpec(
            num_scalar_prefetch=6,
            grid=(N_MOE_BLOCKS,),
            in_specs=[pl.BlockSpec((MOE_BLK, D_MODEL), lambda b, be, bs, *_: (bs[b], 0)), hbm, hbm, hbm],
            out_specs=pl.BlockSpec((MOE_BLK, D_MODEL), lambda b, *_: (b, 0)),
            scratch_shapes=[pltpu.VMEM((2, D_MODEL, EXPERT_FF), F32),
                            pltpu.VMEM((2, D_MODEL, EXPERT_FF), F32),
                            pltpu.VMEM((2, EXPERT_FF, D_MODEL), F32),
                            pltpu.VMEM((D_MODEL, EXPERT_FF), BF16),
                            pltpu.VMEM((D_MODEL, EXPERT_FF), BF16),
                            pltpu.VMEM((EXPERT_FF, D_MODEL), BF16),
                            pltpu.SemaphoreType.DMA((2, 3))]),
        compiler_params=_params(("arbitrary",)),
        name="moe_experts",
    )(blk_expert, blk_src, n_blk, blk_first, blk_next, blk_slot, x_sorted, w_g, w_u, w_d)


def _combine_copy(y_hbm, buf, sem, slot, k, t, src_row):
    return pltpu.make_async_copy(y_hbm.at[pl.ds(src_row, 1)], buf.at[slot, k, pl.ds(t, 1)], sem.at[slot])


def _combine_kernel(dest_ref, dest_next_ref, y_hbm, wt_ref, h_ref, x_ref, mod_ref, g_ref,
                    wsg_ref, wsu_ref, wsd_ref, oc_ref, ol_ref, buf, sem, *, n_ctx):
    i = pl.program_id(0)
    n = pl.num_programs(0)
    slot = i % 2

    def issue(d_ref, s):
        def body(t, carry):
            for k in range(TOP_K):
                _combine_copy(y_hbm, buf, sem, s, k, t, d_ref[k, t]).start()
            return carry
        lax.fori_loop(0, T_COMBINE, body, 0)

    @pl.when(i == 0)
    def _():
        issue(dest_ref, 0)

    @pl.when(i + 1 < n)
    def _():
        issue(dest_next_ref, 1 - slot)

    pltpu.make_async_copy(buf.at[slot], buf.at[slot], sem.at[slot]).wait()

    wt = wt_ref[...]
    acc = buf[slot, 0] * wt[:, 0:1]
    for k in range(1, TOP_K):
        acc = acc + buf[slot, k] * wt[:, k:k + 1]
    h = h_ref[...].astype(BF16)
    act = _silu(_mm(h, wsg_ref[...])) * _mm(h, wsu_ref[...])
    y = acc + _mm(act.astype(BF16), wsd_ref[...])
    out = x_ref[...] + mod_ref[0, 5:6, :] * _rms(y, g_ref[...])

    @pl.when(i < n_ctx)
    def _():
        oc_ref[...] = out

    @pl.when(i >= n_ctx)
    def _():
        ol_ref[...] = out


def _combine(dest, y_sorted, wt_tok, h, x1, mods, g_post, ws_g, ws_u, ws_d):
    tc = T_COMBINE
    n = M_TOK // tc
    n_ctx = N_CTX_TOK // tc
    row = lambda i: (i, 0)
    const = lambda i: (0, 0)
    return pl.pallas_call(
        functools.partial(_combine_kernel, n_ctx=n_ctx),
        out_shape=(jax.ShapeDtypeStruct((N_CTX_TOK, D_MODEL), F32),
                   jax.ShapeDtypeStruct((M_TOK - N_CTX_TOK, D_MODEL), F32)),
        grid=(n,),
        in_specs=[pl.BlockSpec((TOP_K, tc), lambda i: (0, i), memory_space=pltpu.SMEM),
                  pl.BlockSpec((TOP_K, tc), lambda i: (0, jnp.minimum(i + 1, n - 1)),
                               memory_space=pltpu.SMEM),
                  pl.BlockSpec(memory_space=pl.ANY),
                  pl.BlockSpec((tc, LANES), row),
                  pl.BlockSpec((tc, D_MODEL), row),
                  pl.BlockSpec((tc, D_MODEL), row),
                  pl.BlockSpec((1, SUBLANES, D_MODEL), lambda i: (i * tc // ROWS_PER_MOD, 0, 0)),
                  pl.BlockSpec((1, D_MODEL), const),
                  pl.BlockSpec((D_MODEL, EXPERT_FF), const),
                  pl.BlockSpec((D_MODEL, EXPERT_FF), const),
                  pl.BlockSpec((EXPERT_FF, D_MODEL), const)],
        out_specs=(pl.BlockSpec((tc, D_MODEL), lambda i: (jnp.minimum(i, n_ctx - 1), 0)),
                   pl.BlockSpec((tc, D_MODEL), lambda i: (jnp.maximum(i - n_ctx, 0), 0))),
        scratch_shapes=[pltpu.VMEM((2, TOP_K, tc, D_MODEL), F32),
                        pltpu.SemaphoreType.DMA((2,))],
        compiler_params=_params(("arbitrary",)),
        name="moe_combine",
    )(dest, dest, y_sorted, wt_tok, h, x1, mods, g_post, ws_g, ws_u, ws_d)


def _grid_pos_embed(n_tokens):
    rows = n_tokens // GRID_W
    half = D_MODEL // 2
    quarter = half // 2
    omega = 1.0 / (10000.0 ** (jnp.arange(quarter, dtype=F32) / quarter))

    def axis_embed(pos):
        ang = pos.astype(F32)[:, None] * omega
        return jnp.concatenate([jnp.sin(ang), jnp.cos(ang)], axis=-1)

    e_row = axis_embed(jnp.arange(rows))
    e_col = axis_embed(jnp.arange(GRID_W))
    emb = jnp.concatenate([jnp.broadcast_to(e_row[:, None], (rows, GRID_W, half)),
                           jnp.broadcast_to(e_col[None], (rows, GRID_W, half))], axis=-1)
    return emb.reshape(rows * GRID_W, D_MODEL)


def _lane_row(v, lane0):
    return jnp.zeros((1, LANES), F32).at[0, lane0:lane0 + v.shape[0]].set(v)


def kernel(x_prompt, x_sample, state_gla, state_ssd, c, c_ctx, w_ada, b_ada, g_mix_pre, g_mix_post, w_in, w_gk_up, b_gk, g_gla_norm, conv_w, conv_b, dt_bias, a_log, d_skip, g_ssd_norm, w_out, g_ffn_pre, g_ffn_post, w_router, b_router, w_exp_gate, w_exp_up, w_exp_down, w_sh_gate, w_sh_up, w_sh_down):
    assert x_prompt.shape == (N_CTX_SEQ, CTX_LEN, D_MODEL) and x_sample.shape == (N_LAT_SEQ, LAT_LEN, D_MODEL)
    assert w_ada.shape[0] == 1, "single layer"
    l = 0

    w_in_l = w_in[l]
    o_q = 0
    o_k = o_q + GLA_KEY_WIDTH
    o_v = o_k + GLA_KEY_WIDTH
    o_gout = o_v + GLA_WIDTH
    o_lr = o_gout + GLA_WIDTH
    o_z = o_lr + N_DIR * GLA_GATE_RANK
    o_xbc = o_z + SSD_WIDTH
    o_dt = o_xbc + SSD_CONV_CH
    o_end = o_dt + N_DIR * SSD_HEADS
    assert o_end == w_in_l.shape[1]
    w_main = jnp.concatenate([w_in_l[:, o_gout:o_lr], w_in_l[:, o_z:o_xbc], w_in_l[:, o_v:o_gout],
                              w_in_l[:, o_xbc:o_dt], w_in_l[:, o_q:o_k], w_in_l[:, o_k:o_v]],
                             axis=1).astype(BF16)
    w_small = jnp.concatenate([w_in_l[:, o_lr:o_z], w_in_l[:, o_dt:o_end],
                               jnp.zeros((D_MODEL, LANES - SM_DT - N_DIR * SSD_HEADS), F32)], axis=1)
    w_out_bf = w_out[l].astype(BF16)
    wr_t = w_router[l].T
    wr_hi = wr_t.astype(BF16)
    wr_lo = (wr_t - wr_hi.astype(F32)).astype(BF16)
    conv_w8 = jnp.zeros((SUBLANES, SSD_CONV_CH), F32).at[:conv_w.shape[1]].set(conv_w[l])
    d_skip_x = jnp.repeat(d_skip[l], SSD_HEAD_DIM)[None, :]
    g_gla = g_gla_norm[l][None, :]
    g_ssd = g_ssd_norm[l][None, :]

    cvecs = jnp.zeros((SUBLANES, D_MODEL), F32).at[0].set(c_ctx).at[1:1 + N_LAT_SEQ].set(c)
    mod_flat = _ada_mod(cvecs, w_ada[l], b_ada[l][None, :])
    mods = jnp.zeros((N_MOD, SUBLANES, D_MODEL), F32).at[:, :6].set(
        mod_flat[:N_MOD].reshape(N_MOD, 6, D_MODEL))

    x_ctx = x_prompt.reshape(N_CTX_TOK, D_MODEL)
    x_lat = x_sample.reshape(N_LAT_SEQ * LAT_LEN, D_MODEL)
    pos = _grid_pos_embed(LAT_LEN)

    proj_main, proj_small = _in_proj(x_ctx, x_lat, pos, mods, g_mix_pre[l][None, :], w_main, w_small)

    o_dir, s_gla_dir, y_dir, s_ssd_dir = [], [], [], []
    for d in range(N_DIR):
        rev = d == 1
        wgk_pad = jnp.zeros((LANES, GLA_KEY_WIDTH), F32).at[
            SM_LR + d * GLA_GATE_RANK:SM_LR + (d + 1) * GLA_GATE_RANK].set(w_gk_up[l, d])
        o_d, s_d = _gla_scan(proj_main, proj_small, wgk_pad, b_gk[l, d][None, :],
                             state_gla[:, l, d], rev)
        o_dir.append(o_d)
        s_gla_dir.append(s_d)
        lane0 = SM_DT + d * SSD_HEADS
        dtb_r = _lane_row(dt_bias[l, d], lane0)
        nea_r = _lane_row(-jnp.exp(a_log[l, d]), lane0)
        s0_ssd = state_ssd[:, l, d].reshape(N_LAT_SEQ, SSD_GROUPS, SSD_GROUP_WIDTH, SSD_STATE)
        y_d, t_d = _ssd_scan(proj_main, proj_small, conv_w8, conv_b[l][None, :],
                             dtb_r, nea_r, dtb_r.T, nea_r.T, d_skip_x, s0_ssd, rev, d)
        y_dir.append(y_d)
        s_ssd_dir.append(t_d.reshape(N_CTX_SEQ, SSD_HEADS, SSD_HEAD_DIM, SSD_STATE))
    new_state_gla = jnp.stack(s_gla_dir, axis=1)[:, None]
    new_state_ssd = jnp.stack(s_ssd_dir, axis=1)[:, None]

    x1 = _out_proj(o_dir[0], o_dir[1], y_dir[0], y_dir[1], proj_main, x_ctx, x_lat, pos, mods,
                   g_gla, g_ssd, g_mix_post[l][None, :], w_out_bf)

    h_ffn, idx, wt_tok, rank, cnt = _router(x1, mods, g_ffn_pre[l][None, :], wr_hi, wr_lo,
                                               b_router[l][:, None])
    i32 = jnp.int32
    e_ids = jnp.arange(N_EXPERTS, dtype=i32)
    counts = cnt[:, 0].astype(i32)
    padded = (counts + MOE_BLK - 1) // MOE_BLK * MOE_BLK
    pad_end = jnp.sum(jnp.where(e_ids[None, :] <= e_ids[:, None], padded[None, :], 0), axis=1)
    pad_start = pad_end - padded
    slot0 = jnp.sum(jnp.where(idx[:, :, None] == e_ids, pad_start, 0), axis=-1)
    dest = slot0 + rank
    n_blk = pad_end[-1] // MOE_BLK
    blk_ids = jnp.arange(N_MOE_BLOCKS, dtype=i32)
    blk_src = jnp.minimum(blk_ids, n_blk - 1)
    blk_expert = jnp.minimum(jnp.sum((pad_end[None, :] <= (blk_src * MOE_BLK)[:, None]).astype(i32), axis=1),
                             N_EXPERTS - 1)
    blk_hot = blk_expert[:, None] == e_ids
    nonempty = counts > 0
    ordinal = jnp.sum(jnp.where((e_ids[None, :] < e_ids[:, None]) & nonempty[None, :], 1, 0), axis=1)
    next_e = jnp.min(jnp.where((e_ids[None, :] > e_ids[:, None]) & nonempty[None, :], e_ids[None, :],
                               N_EXPERTS), axis=1)
    next_e = jnp.where(next_e == N_EXPERTS, -1, next_e)
    blk_first = jnp.concatenate([jnp.ones((1,), i32), (blk_expert[1:] != blk_expert[:-1]).astype(i32)])
    blk_next = jnp.sum(jnp.where(blk_hot, next_e, 0), axis=1).astype(i32)
    blk_slot = (jnp.sum(jnp.where(blk_hot, ordinal, 0), axis=1) % 2).astype(i32)
    n_blk_arr = n_blk.astype(i32)[None]
    x_sorted = _dispatch((pad_start + counts).astype(i32), (padded - counts).astype(i32),
                         n_blk_arr, dest, h_ffn)
    y_sorted = _experts(blk_expert.astype(i32), blk_src.astype(i32), n_blk_arr, blk_first, blk_next,
                        blk_slot, x_sorted, w_exp_gate[l], w_exp_up[l], w_exp_down[l])
    out_ctx, out_lat = _combine(dest, y_sorted, wt_tok, h_ffn, x1, mods, g_ffn_post[l][None, :],
                                w_sh_gate[l].astype(BF16), w_sh_up[l].astype(BF16),
                                w_sh_down[l].astype(BF16))
    return (out_ctx.reshape(N_CTX_SEQ, CTX_LEN, D_MODEL), out_lat.reshape(N_LAT_SEQ, LAT_LEN, D_MODEL),
            new_state_gla, new_state_ssd)
```

```python
import functools

import numpy as np
import jax
import jax.numpy as jnp
from jax import lax
from jax.experimental import pallas as pl
from jax.experimental.pallas import tpu as pltpu

F32 = jnp.float32
BF16 = jnp.bfloat16

D_MODEL = 2048
N_CTX_SEQ = 16
CTX_LEN = 256
N_LAT_SEQ = 2
LAT_LEN = 4096
GRID_W = 64
EPS = 1e-6
N_CTX_TOK = N_CTX_SEQ * CTX_LEN
M_TOK = N_CTX_TOK + N_LAT_SEQ * LAT_LEN
SEQ_LENS = (CTX_LEN,) * N_CTX_SEQ + (LAT_LEN,) * N_LAT_SEQ
ROWS_PER_MOD = 4096
N_MOD = M_TOK // ROWS_PER_MOD

GLA_HEADS = 4
GLA_DK = 128
GLA_DV = 256
GLA_KEY_WIDTH = GLA_HEADS * GLA_DK
GLA_WIDTH = GLA_HEADS * GLA_DV
GLA_GATE_RANK = 16
GLA_GATE_TAU = 16.0
GLA_CHUNK = 64
GLA_ROWS = 256

SSD_HEADS = 16
SSD_HEAD_DIM = 64
SSD_GROUPS = 2
SSD_HPG = SSD_HEADS // SSD_GROUPS
SSD_STATE = 128
SSD_WIDTH = SSD_HEADS * SSD_HEAD_DIM
SSD_GROUP_WIDTH = SSD_WIDTH // SSD_GROUPS
SSD_CHUNK = 128
SSD_CONV_CH = SSD_WIDTH + 2 * SSD_GROUPS * SSD_STATE
N_DIR = 2

N_EXPERTS = 256
TOP_K = 8
N_EXPERT_GROUPS = 8
GROUP_SIZE = N_EXPERTS // N_EXPERT_GROUPS
TOPK_GROUPS = 4
EXPERT_FF = 512
ROUTED_SCALE = 2.5

C_GOUT = 0
C_Z = C_GOUT + GLA_WIDTH
C_V = C_Z + SSD_WIDTH
C_XBC = C_V + GLA_WIDTH
C_Q = C_XBC + SSD_CONV_CH
C_K = C_Q + GLA_KEY_WIDTH
MAIN_WIDTH = C_K + GLA_KEY_WIDTH
LANES = 128
SUBLANES = 8
SM_LR = 0
SM_DT = N_DIR * GLA_GATE_RANK

TM_PROJ = 1024
TN_PROJ = 512
TM_OUT = 256
TM_ROUTE = 256
MOE_BLK = 128
N_PAIRS = M_TOK * TOP_K
N_MOE_BLOCKS = N_PAIRS // MOE_BLK + N_EXPERTS
N_SLOTS = N_MOE_BLOCKS * MOE_BLK
T_DISPATCH = 512
T_COMBINE = 128
ADA_TN = 1024
VMEM_LIMIT = 52 * 1024 * 1024


def _mm(a, b):
    return jnp.dot(a, b, preferred_element_type=F32)


def _mm_nt(a, b):
    return lax.dot_general(a, b, (((1,), (1,)), ((), ())), preferred_element_type=F32)


def _mm_tn(a, b):
    return lax.dot_general(a, b, (((0,), (0,)), ((), ())), preferred_element_type=F32)


def _split2(x):
    hi = x.astype(BF16)
    lo = (x - hi.astype(F32)).astype(BF16)
    return hi, lo


def _split3(x):
    hi = x.astype(BF16)
    r = x - hi.astype(F32)
    mid = r.astype(BF16)
    lo = (r - mid.astype(F32)).astype(BF16)
    return hi, mid, lo


def _mm_x3(a, b):
    a_hi, a_lo = _split2(a)
    b_hi, b_lo = _split2(b)
    return _mm(a_hi, b_hi) + _mm(a_lo, b_hi) + _mm(a_hi, b_lo)


def _mm_sel_left(sel_bf, x):
    hi, mid, lo = _split3(x)
    return _mm(sel_bf, hi) + _mm(sel_bf, mid) + _mm(sel_bf, lo)


def _mm_sel_right(x, sel_bf):
    hi, mid, lo = _split3(x)
    return _mm(hi, sel_bf) + _mm(mid, sel_bf) + _mm(lo, sel_bf)


def _sigmoid(x):
    return 1.0 / (1.0 + jnp.exp(-x))


def _silu(x):
    return x * _sigmoid(x)


def _softplus(x):
    return jnp.maximum(x, 0.0) + jnp.log1p(jnp.exp(-jnp.abs(x)))


def _rms(x, g):
    return x * lax.rsqrt(jnp.mean(x * x, axis=-1, keepdims=True) + EPS) * g


def _params(sem, vmem=VMEM_LIMIT):
    return pltpu.CompilerParams(dimension_semantics=sem, vmem_limit_bytes=vmem)


def _ada_kernel(c_ref, w_ref, b_ref, o_ref):
    o_ref[...] = _mm_x3(_silu(c_ref[...]), w_ref[...]) + b_ref[...]


def _ada_mod(cvecs, w_ada, b_ada):
    n_out = w_ada.shape[1]
    return pl.pallas_call(
        _ada_kernel,
        out_shape=jax.ShapeDtypeStruct((SUBLANES, n_out), F32),
        grid=(n_out // ADA_TN,),
        in_specs=[pl.BlockSpec((SUBLANES, D_MODEL), lambda j: (0, 0)),
                  pl.BlockSpec((D_MODEL, ADA_TN), lambda j: (0, j)),
                  pl.BlockSpec((1, ADA_TN), lambda j: (0, j))],
        out_specs=pl.BlockSpec((SUBLANES, ADA_TN), lambda j: (0, j)),
        compiler_params=_params(("arbitrary",)),
        name="ada_mod",
    )(cvecs, w_ada, b_ada)


def _token_specs(tm, buffers=2):
    n_ctx = N_CTX_TOK // tm
    n_pos = LAT_LEN // tm
    mode = dict(pipeline_mode=pl.Buffered(buffers)) if buffers != 2 else {}
    ctx = pl.BlockSpec((tm, D_MODEL), lambda i, *_: (jnp.minimum(i, n_ctx - 1), 0), **mode)
    lat = pl.BlockSpec((tm, D_MODEL), lambda i, *_: (jnp.maximum(i - n_ctx, 0), 0), **mode)
    pos = pl.BlockSpec((tm, D_MODEL), lambda i, *_: (jnp.maximum(i - n_ctx, 0) % n_pos, 0), **mode)
    return n_ctx, [ctx, lat, pos]


def _inproj_kernel(xc_ref, xl_ref, pos_ref, mod_ref, g_ref, w_ref, ws_ref, o_ref, os_ref, h_scr, *, n_ctx):
    def prologue(x):
        h = _rms(x, g_ref[...]) * (1.0 + mod_ref[0, 1:2, :]) + mod_ref[0, 0:1, :]
        h_hi, h_lo = _split2(h)
        h_scr[...] = h_hi
        ws_hi, ws_lo = _split2(ws_ref[...])
        os_ref[...] = _mm(h_hi, ws_hi) + _mm(h_lo, ws_hi) + _mm(h_hi, ws_lo)

    first = pl.program_id(1) == 0
    is_ctx = pl.program_id(0) < n_ctx

    @pl.when(first & is_ctx)
    def _():
        prologue(xc_ref[...])

    @pl.when(first & jnp.logical_not(is_ctx))
    def _():
        prologue(xl_ref[...] + pos_ref[...])

    o_ref[...] = _mm(h_scr[...], w_ref[...])


def _in_proj(x_ctx, x_lat, pos, mods, g_pre, w_main, w_small):
    tm, tn = TM_PROJ, TN_PROJ
    n_ctx, tok_specs = _token_specs(tm, buffers=1)
    return pl.pallas_call(
        functools.partial(_inproj_kernel, n_ctx=n_ctx),
        out_shape=(jax.ShapeDtypeStruct((M_TOK, MAIN_WIDTH), F32),
                   jax.ShapeDtypeStruct((M_TOK, LANES), F32)),
        grid=(M_TOK // tm, MAIN_WIDTH // tn),
        in_specs=tok_specs + [
                  pl.BlockSpec((1, SUBLANES, D_MODEL), lambda i, j: (i * tm // ROWS_PER_MOD, 0, 0)),
                  pl.BlockSpec((1, D_MODEL), lambda i, j: (0, 0)),
                  pl.BlockSpec((D_MODEL, tn), lambda i, j: (0, j)),
                  pl.BlockSpec((D_MODEL, LANES), lambda i, j: (0, 0))],
        out_specs=(pl.BlockSpec((tm, tn), lambda i, j: (i, j)),
                   pl.BlockSpec((tm, LANES), lambda i, j: (i, 0))),
        scratch_shapes=[pltpu.VMEM((tm, D_MODEL), BF16)],
        compiler_params=_params(("arbitrary", "arbitrary")),
        name="in_proj",
    )(x_ctx, x_lat, pos, mods, g_pre, w_main, w_small)


def _scan_schedule(rows_per_step, reverse):
    blk, flag, s0i, soi, emit, has_prev, has_next = [], [], [], [], [], [], []
    start = 0
    for s, length in enumerate(SEQ_LENS):
        nb = length // rows_per_step
        is_ctx = s < N_CTX_SEQ
        order = range(nb - 1, -1, -1) if reverse else range(nb)
        for n, b in enumerate(order):
            blk.append(start + b)
            flag.append((1 if is_ctx else 2) if n == 0 else 0)
            s0i.append(0 if is_ctx else s - N_CTX_SEQ)
            soi.append(s if is_ctx else N_CTX_SEQ - 1)
            emit.append(1 if (is_ctx and n == nb - 1) else 0)
            has_prev.append(1 if b > 0 else 0)
            has_next.append(1 if b < nb - 1 else 0)
        start += nb
    return tuple(jnp.asarray(np.array(a, np.int32)) for a in (blk, flag, s0i, soi, emit, has_prev, has_next))


def _gla_kernel(blk_ref, flag_ref, s0i_ref, soi_ref, emit_ref,
                q_ref, k_ref, v_ref, sm_ref, wgk_ref, bgk_ref, s0_ref,
                o_ref, so_ref, st_scr, *, reverse):
    i = pl.program_id(0)
    flag = flag_ref[i]

    @pl.when(flag == 1)
    def _():
        st_scr[...] = jnp.zeros_like(st_scr)

    @pl.when(flag == 2)
    def _():
        for h in range(GLA_HEADS):
            st_scr[h] = s0_ref[0, h].T

    c = GLA_CHUNK
    r_id = lax.broadcasted_iota(jnp.int32, (c, c), 0)
    c_id = lax.broadcasted_iota(jnp.int32, (c, c), 1)
    tri = (c_id >= r_id) if reverse else (c_id <= r_id)
    tri_bf = jnp.where(tri, 1.0, 0.0).astype(BF16)

    gk = _mm_x3(sm_ref[...], wgk_ref[...]) + bgk_ref[...]
    log_a = (jnp.minimum(gk, 0.0) - jnp.log1p(jnp.exp(-jnp.abs(gk)))) * (1.0 / GLA_GATE_TAU)

    n_chunks = GLA_ROWS // c
    for ci in (range(n_chunks - 1, -1, -1) if reverse else range(n_chunks)):
        lo = ci * c
        b_all = _mm_sel_left(tri_bf, log_a[lo:lo + c])
        for h in range(GLA_HEADS):
            kc = slice(h * GLA_DK, (h + 1) * GLA_DK)
            vc = slice(h * GLA_DV, (h + 1) * GLA_DV)
            b = b_all[:, kc]
            b_end = b[0:1] if reverse else b[c - 1:c]
            q = q_ref[lo:lo + c, kc] * (GLA_DK ** -0.5)
            k = k_ref[lo:lo + c, kc]
            v = v_ref[lo:lo + c, vc].astype(BF16)
            q_e = (q * jnp.exp(b)).astype(BF16)
            k_e = (k * jnp.exp(-b)).astype(BF16)
            att = jnp.where(tri, _mm_nt(q_e, k_e), 0.0).astype(BF16)
            st = st_scr[h]
            o_ref[lo:lo + c, vc] = _mm(att, v) + _mm_nt(q_e, st.astype(BF16))
            k_end = (k * jnp.exp(b_end - b)).astype(BF16)
            st_scr[h] = st * jnp.exp(b_end) + _mm_tn(v, k_end)

    @pl.when(emit_ref[i] == 1)
    def _():
        for h in range(GLA_HEADS):
            so_ref[0, h] = st_scr[h].T


def _gla_scan(proj_main, proj_small, wgk_pad, bgk, s0, reverse):
    sched = _scan_schedule(GLA_ROWS, reverse)[:5]
    n_steps = M_TOK // GLA_ROWS
    t = GLA_ROWS
    q_blk, k_blk, v_blk = C_Q // GLA_KEY_WIDTH, C_K // GLA_KEY_WIDTH, C_V // GLA_WIDTH
    state_blk = (1, GLA_HEADS, GLA_DK, GLA_DV)
    return pl.pallas_call(
        functools.partial(_gla_kernel, reverse=reverse),
        out_shape=(jax.ShapeDtypeStruct((M_TOK, GLA_WIDTH), F32),
                   jax.ShapeDtypeStruct((N_CTX_SEQ,) + state_blk[1:], F32)),
        grid_spec=pltpu.PrefetchScalarGridSpec(
            num_scalar_prefetch=5,
            grid=(n_steps,),
            in_specs=[
                pl.BlockSpec((t, GLA_KEY_WIDTH), lambda i, blk, *_: (blk[i], q_blk)),
                pl.BlockSpec((t, GLA_KEY_WIDTH), lambda i, blk, *_: (blk[i], k_blk)),
                pl.BlockSpec((t, GLA_WIDTH), lambda i, blk, *_: (blk[i], v_blk)),
                pl.BlockSpec((t, LANES), lambda i, blk, *_: (blk[i], 0)),
                pl.BlockSpec((LANES, GLA_KEY_WIDTH), lambda i, *_: (0, 0)),
                pl.BlockSpec((1, GLA_KEY_WIDTH), lambda i, *_: (0, 0)),
                pl.BlockSpec(state_blk, lambda i, blk, flag, s0i, *_: (s0i[i], 0, 0, 0)),
            ],
            out_specs=(
                pl.BlockSpec((t, GLA_WIDTH), lambda i, blk, *_: (blk[i], 0)),
                pl.BlockSpec(state_blk, lambda i, blk, flag, s0i, soi, *_: (soi[i], 0, 0, 0)),
            ),
            scratch_shapes=[pltpu.VMEM((GLA_HEADS, GLA_DV, GLA_DK), F32)]),
        compiler_params=_params(("arbitrary",)),
        name="gla_bwd" if reverse else "gla_fwd",
    )(*sched, proj_main, proj_main, proj_main, proj_small, wgk_pad, bgk, s0)


def _ssd_kernel(blk_ref, flag_ref, s0i_ref, soi_ref, emit_ref, hp_ref, hn_ref,
                xbc_ref, xprev_ref, xnext_ref, sm_ref, cw_ref, cb_ref,
                dtb_r_ref, nea_r_ref, dtb_c_ref, nea_c_ref, dsk_ref, s0_ref,
                y_ref, so_ref, st_scr, *, reverse, lane0, add_skip):
    i = pl.program_id(0)
    flag = flag_ref[i]
    t = SSD_CHUNK

    @pl.when(flag == 1)
    def _():
        st_scr[...] = jnp.zeros_like(st_scr)

    @pl.when(flag == 2)
    def _():
        for g in range(SSD_GROUPS):
            st_scr[g] = s0_ref[0, g].T

    xbc = xbc_ref[...]
    prev = jnp.where(hp_ref[i] == 1, xprev_ref[SUBLANES - 1:SUBLANES, :], 0.0)
    nxt = jnp.where(hn_ref[i] == 1, xnext_ref[0:1, :], 0.0)
    row = lax.broadcasted_iota(jnp.int32, xbc.shape, 0)
    x_m1 = jnp.where(row == 0, prev, pltpu.roll(xbc, 1, 0))
    x_p1 = jnp.where(row == t - 1, nxt, pltpu.roll(xbc, t - 1, 0))
    act = _silu(x_m1 * cw_ref[0:1, :] + xbc * cw_ref[1:2, :] + x_p1 * cw_ref[2:3, :] + cb_ref[...])
    xs = act[:, :SSD_WIDTH]
    bm = act[:, SSD_WIDTH:SSD_WIDTH + SSD_GROUPS * SSD_STATE]
    cm = act[:, SSD_WIDTH + SSD_GROUPS * SSD_STATE:]

    sm = sm_ref[...]
    dt = _softplus(sm + dtb_r_ref[...])
    a = dt * nea_r_ref[...]
    a_t = _softplus(sm.T + dtb_c_ref[...]) * nea_c_ref[...]

    r_id = lax.broadcasted_iota(jnp.int32, (t, t), 0)
    c_id = lax.broadcasted_iota(jnp.int32, (t, t), 1)
    tri = (c_id >= r_id) if reverse else (c_id <= r_id)
    tri_bf = jnp.where(tri, 1.0, 0.0).astype(BF16)
    tri_t_bf = jnp.where((r_id >= c_id) if reverse else (r_id <= c_id), 1.0, 0.0).astype(BF16)
    cs = _mm_sel_left(tri_bf, a)
    cs_t = _mm_sel_right(a_t, tri_t_bf)

    e_r = lax.broadcasted_iota(jnp.int32, (LANES, SSD_WIDTH), 0)
    e_c = lax.broadcasted_iota(jnp.int32, (LANES, SSD_WIDTH), 1)
    expand = jnp.where(e_r - lane0 == e_c // SSD_HEAD_DIM, 1.0, 0.0).astype(BF16)
    dt_x = _mm_sel_right(dt, expand)
    cs_x = _mm_sel_right(cs, expand)
    cs_end_x = cs_x[0:1] if reverse else cs_x[t - 1:t]
    x_in = xs * dt_x
    x_bf = x_in.astype(BF16)
    x_w = (x_in * jnp.exp(cs_end_x - cs_x)).astype(BF16)
    decay_in = jnp.exp(cs_x)
    decay_end = jnp.exp(cs_end_x)

    for g in range(SSD_GROUPS):
        gc = slice(g * SSD_STATE, (g + 1) * SSD_STATE)
        gw = slice(g * SSD_GROUP_WIDTH, (g + 1) * SSD_GROUP_WIDTH)
        c_g = cm[:, gc].astype(BF16)
        b_g = bm[:, gc].astype(BF16)
        cb = _mm_nt(c_g, b_g)
        st = st_scr[g]
        y_state = _mm(c_g, st.astype(BF16)) * decay_in[:, gw]
        for r in range(SSD_HPG):
            hh = g * SSD_HPG + r
            hc = slice(hh * SSD_HEAD_DIM, (hh + 1) * SSD_HEAD_DIM)
            lane = lane0 + hh
            seg = cs[:, lane:lane + 1] - cs_t[lane:lane + 1, :]
            lm = jnp.exp(jnp.where(tri, seg, -jnp.inf))
            y_h = _mm((cb * lm).astype(BF16), x_bf[:, hc]) + y_state[:, r * SSD_HEAD_DIM:(r + 1) * SSD_HEAD_DIM]
            if add_skip:
                y_h = y_h + xs[:, hc] * dsk_ref[:, hc]
            y_ref[:, hc] = y_h
        st_scr[g] = st * decay_end[:, gw] + _mm_tn(b_g, x_w[:, gw])

    @pl.when(emit_ref[i] == 1)
    def _():
        for g in range(SSD_GROUPS):
            so_ref[0, g] = st_scr[g].T


def _ssd_scan(proj_main, proj_small, conv_w, conv_b, dtb_r, nea_r, dtb_c, nea_c, d_skip_x, s0,
              reverse, direction):
    sched = _scan_schedule(SSD_CHUNK, reverse)
    t = SSD_CHUNK
    n_steps = M_TOK // t
    xbc_blk = C_XBC // SSD_CONV_CH
    rb = t // SUBLANES
    n_rb = M_TOK // SUBLANES
    state_blk = (1, SSD_GROUPS, SSD_GROUP_WIDTH, SSD_STATE)
    return pl.pallas_call(
        functools.partial(_ssd_kernel, reverse=reverse, lane0=SM_DT + direction * SSD_HEADS,
                          add_skip=not reverse),
        out_shape=(jax.ShapeDtypeStruct((M_TOK, SSD_WIDTH), F32),
                   jax.ShapeDtypeStruct((N_CTX_SEQ,) + state_blk[1:], F32)),
        grid_spec=pltpu.PrefetchScalarGridSpec(
            num_scalar_prefetch=7,
            grid=(n_steps,),
            in_specs=[
                pl.BlockSpec((t, SSD_CONV_CH), lambda i, blk, *_: (blk[i], xbc_blk)),
                pl.BlockSpec((SUBLANES, SSD_CONV_CH),
                             lambda i, blk, *_: (jnp.maximum(blk[i] * rb - 1, 0), xbc_blk)),
                pl.BlockSpec((SUBLANES, SSD_CONV_CH),
                             lambda i, blk, *_: (jnp.minimum((blk[i] + 1) * rb, n_rb - 1), xbc_blk)),
                pl.BlockSpec((t, LANES), lambda i, blk, *_: (blk[i], 0)),
                pl.BlockSpec((SUBLANES, SSD_CONV_CH), lambda i, *_: (0, 0)),
                pl.BlockSpec((1, SSD_CONV_CH), lambda i, *_: (0, 0)),
                pl.BlockSpec((1, LANES), lambda i, *_: (0, 0)),
                pl.BlockSpec((1, LANES), lambda i, *_: (0, 0)),
                pl.BlockSpec((LANES, 1), lambda i, *_: (0, 0)),
                pl.BlockSpec((LANES, 1), lambda i, *_: (0, 0)),
                pl.BlockSpec((1, SSD_WIDTH), lambda i, *_: (0, 0)),
                pl.BlockSpec(state_blk, lambda i, blk, flag, s0i, *_: (s0i[i], 0, 0, 0)),
            ],
            out_specs=(
                pl.BlockSpec((t, SSD_WIDTH), lambda i, blk, *_: (blk[i], 0)),
                pl.BlockSpec(state_blk, lambda i, blk, flag, s0i, soi, *_: (soi[i], 0, 0, 0)),
            ),
            scratch_shapes=[pltpu.VMEM((SSD_GROUPS, SSD_STATE, SSD_GROUP_WIDTH), F32)]),
        compiler_params=_params(("arbitrary",)),
        name="ssd_bwd" if reverse else "ssd_fwd",
    )(*sched, proj_main, proj_main, proj_main, proj_small, conv_w, conv_b,
      dtb_r, nea_r, dtb_c, nea_c, d_skip_x, s0)


def _outproj_kernel(of_ref, ob_ref, gout_ref, yf_ref, yb_ref, z_ref, xc_ref, xl_ref, pos_ref, mod_ref,
                    ggla_ref, gssd_ref, gpost_ref, w_ref, o_ref, *, n_ctx):
    o = of_ref[...] + ob_ref[...]
    gate = _silu(gout_ref[...])
    parts = []
    for h in range(GLA_HEADS):
        hc = slice(h * GLA_DV, (h + 1) * GLA_DV)
        parts.append((_rms(o[:, hc], ggla_ref[...]) * gate[:, hc]).astype(BF16))
    y = (yf_ref[...] + yb_ref[...]) * _silu(z_ref[...])
    for g in range(SSD_GROUPS):
        gw = slice(g * SSD_GROUP_WIDTH, (g + 1) * SSD_GROUP_WIDTH)
        parts.append(_rms(y[:, gw], gssd_ref[:, gw]).astype(BF16))
    acc = None
    col = 0
    for p in parts:
        term = _mm(p, w_ref[col:col + p.shape[1], :])
        acc = term if acc is None else acc + term
        col += p.shape[1]
    delta = mod_ref[0, 2:3, :] * _rms(acc, gpost_ref[...])
    is_ctx = pl.program_id(0) < n_ctx

    @pl.when(is_ctx)
    def _():
        o_ref[...] = xc_ref[...] + delta

    @pl.when(jnp.logical_not(is_ctx))
    def _():
        o_ref[...] = xl_ref[...] + pos_ref[...] + delta


def _out_proj(o_f, o_b, y_f, y_b, proj_main, x_ctx, x_lat, pos, mods, g_gla, g_ssd, g_post, w_out):
    tm = TM_OUT
    row = lambda i: (i, 0)
    const = lambda i: (0, 0)
    n_ctx, tok_specs = _token_specs(tm)
    return pl.pallas_call(
        functools.partial(_outproj_kernel, n_ctx=n_ctx),
        out_shape=jax.ShapeDtypeStruct((M_TOK, D_MODEL), F32),
        grid=(M_TOK // tm,),
        in_specs=[pl.BlockSpec((tm, GLA_WIDTH), row),
                  pl.BlockSpec((tm, GLA_WIDTH), row),
                  pl.BlockSpec((tm, GLA_WIDTH), lambda i: (i, C_GOUT // GLA_WIDTH)),
                  pl.BlockSpec((tm, SSD_WIDTH), row),
                  pl.BlockSpec((tm, SSD_WIDTH), row),
                  pl.BlockSpec((tm, SSD_WIDTH), lambda i: (i, C_Z // SSD_WIDTH))] + tok_specs + [
                  pl.BlockSpec((1, SUBLANES, D_MODEL), lambda i: (i * tm // ROWS_PER_MOD, 0, 0)),
                  pl.BlockSpec((1, GLA_DV), const),
                  pl.BlockSpec((1, SSD_WIDTH), const),
                  pl.BlockSpec((1, D_MODEL), const),
                  pl.BlockSpec((D_MODEL, D_MODEL), const)],
        out_specs=pl.BlockSpec((tm, D_MODEL), row),
        compiler_params=_params(("arbitrary",)),
        name="out_proj",
    )(o_f, o_b, proj_main, y_f, y_b, proj_main, x_ctx, x_lat, pos, mods, g_gla, g_ssd, g_post, w_out)


def _router_kernel(x_ref, mod_ref, g_ref, wr_hi_ref, wr_lo_ref, br_ref,
                   h_ref, idx_ref, wt_ref, rank_ref, cnt_ref, cnt_scr):
    i = pl.program_id(0)
    tm = TM_ROUTE

    @pl.when(i == 0)
    def _():
        cnt_scr[...] = jnp.zeros_like(cnt_scr)

    h = _rms(x_ref[...], g_ref[...]) * (1.0 + mod_ref[0, 4:5, :]) + mod_ref[0, 3:4, :]
    h_ref[...] = h
    h_hi, h_lo = _split2(h)
    wr_hi = wr_hi_ref[...]
    logits = _mm_nt(wr_hi, h_hi) + _mm_nt(wr_hi, h_lo) + _mm_nt(wr_lo_ref[...], h_hi)
    scores = _sigmoid(logits)
    sel = scores + br_ref[...]
    neg = -jnp.inf

    def first_argmax(x, ids, n):
        m = jnp.max(x, axis=0, keepdims=True)
        return m, jnp.min(jnp.where(x == m, ids, float(n)), axis=0, keepdims=True)

    ids_g = lax.broadcasted_iota(jnp.int32, (GROUP_SIZE, tm), 0).astype(F32)
    grp = []
    for g in range(N_EXPERT_GROUPS):
        xg = sel[g * GROUP_SIZE:(g + 1) * GROUP_SIZE]
        m1, a1 = first_argmax(xg, ids_g, GROUP_SIZE)
        m2 = jnp.max(jnp.where(ids_g == a1, neg, xg), axis=0, keepdims=True)
        grp.append(m1 + m2)
    gsc = jnp.concatenate(grp, axis=0)
    ids_8 = lax.broadcasted_iota(jnp.int32, (N_EXPERT_GROUPS, tm), 0).astype(F32)
    keep = jnp.zeros((N_EXPERT_GROUPS, tm), F32)
    for _ in range(TOPK_GROUPS):
        _, a = first_argmax(gsc, ids_8, N_EXPERT_GROUPS)
        pick = ids_8 == a
        keep = jnp.where(pick, 1.0, keep)
        gsc = jnp.where(pick, neg, gsc)
    selm = jnp.concatenate(
        [jnp.where(keep[g:g + 1] > 0.5, sel[g * GROUP_SIZE:(g + 1) * GROUP_SIZE], neg)
         for g in range(N_EXPERT_GROUPS)], axis=0)

    ids_e = lax.broadcasted_iota(jnp.int32, (N_EXPERTS, tm), 0).astype(F32)
    picks, wts = [], []
    chosen = jnp.zeros((N_EXPERTS, tm), F32)
    for _ in range(TOP_K):
        _, a = first_argmax(selm, ids_e, N_EXPERTS)
        hit = ids_e == a
        picks.append(a)
        wts.append(jnp.sum(jnp.where(hit, scores, 0.0), axis=0, keepdims=True))
        chosen = jnp.where(hit, 1.0, chosen)
        selm = jnp.where(hit, neg, selm)
    w = jnp.concatenate(wts, axis=0)
    w = w / jnp.sum(w, axis=0, keepdims=True) * ROUTED_SCALE
    idx_ref[...] = jnp.concatenate(picks, axis=0).astype(jnp.int32)
    wt_ref[...] = jnp.concatenate([w, jnp.zeros((LANES - TOP_K, tm), F32)], axis=0).T

    t_r = lax.broadcasted_iota(jnp.int32, (tm, tm), 0)
    t_c = lax.broadcasted_iota(jnp.int32, (tm, tm), 1)
    before = jnp.where(t_r < t_c, 1.0, 0.0).astype(BF16)
    base = _mm(chosen.astype(BF16), before) + cnt_scr[...]
    rank_ref[...] = jnp.concatenate(
        [jnp.sum(jnp.where(ids_e == a, base, 0.0), axis=0, keepdims=True) for a in picks],
        axis=0).astype(jnp.int32)
    cnt_scr[...] = cnt_scr[...] + jnp.sum(chosen, axis=1, keepdims=True)
    cnt_ref[...] = jnp.broadcast_to(cnt_scr[...], cnt_ref.shape)


def _router(x1, mods, g_pre, wr_hi, wr_lo, b_router):
    tm = TM_ROUTE
    const = lambda i: (0, 0)
    return pl.pallas_call(
        _router_kernel,
        out_shape=(jax.ShapeDtypeStruct((M_TOK, D_MODEL), F32),
                   jax.ShapeDtypeStruct((TOP_K, M_TOK), jnp.int32),
                   jax.ShapeDtypeStruct((M_TOK, LANES), F32),
                   jax.ShapeDtypeStruct((TOP_K, M_TOK), jnp.int32),
                   jax.ShapeDtypeStruct((N_EXPERTS, LANES), F32)),
        grid=(M_TOK // tm,),
        in_specs=[pl.BlockSpec((tm, D_MODEL), lambda i: (i, 0)),
                  pl.BlockSpec((1, SUBLANES, D_MODEL), lambda i: (i * tm // ROWS_PER_MOD, 0, 0)),
                  pl.BlockSpec((1, D_MODEL), const),
                  pl.BlockSpec((N_EXPERTS, D_MODEL), const),
                  pl.BlockSpec((N_EXPERTS, D_MODEL), const),
                  pl.BlockSpec((N_EXPERTS, 1), const)],
        out_specs=(pl.BlockSpec((tm, D_MODEL), lambda i: (i, 0)),
                   pl.BlockSpec((TOP_K, tm), lambda i: (0, i)),
                   pl.BlockSpec((tm, LANES), lambda i: (i, 0)),
                   pl.BlockSpec((TOP_K, tm), lambda i: (0, i)),
                   pl.BlockSpec((N_EXPERTS, LANES), const)),
        scratch_shapes=[pltpu.VMEM((N_EXPERTS, 1), F32)],
        compiler_params=_params(("arbitrary",)),
        name="router",
    )(x1, mods, g_pre, wr_hi, wr_lo, b_router)


def _slot_table_kernel(fill_start_ref, fill_len_ref, nb_ref, dest_ref, tab_hbm, tab, sem):
    i = pl.program_id(0)

    @pl.when(i == 0)
    def _():
        def pad_body(e, carry):
            start = fill_start_ref[e]

            def one(j, c):
                tab[start + j] = -1
                return c

            lax.fori_loop(0, fill_len_ref[e], one, 0)
            return carry

        def tail_body(s, carry):
            tab[s] = -1
            return carry

        lax.fori_loop(0, N_EXPERTS, pad_body, 0)
        lax.fori_loop(nb_ref[0] * MOE_BLK, N_SLOTS, tail_body, 0)

    base = i * T_DISPATCH

    def body(t, carry):
        for k in range(TOP_K):
            tab[dest_ref[k, t]] = (base + t) * TOP_K + k
        return carry

    lax.fori_loop(0, T_DISPATCH, body, 0)

    @pl.when(i == pl.num_programs(0) - 1)
    def _():
        cp = pltpu.make_async_copy(tab, tab_hbm, sem)
        cp.start()
        cp.wait()


def _slot_table(fill_start, fill_len, n_blk, dest):
    return pl.pallas_call(
        _slot_table_kernel,
        out_shape=jax.ShapeDtypeStruct((N_SLOTS,), jnp.int32),
        grid_spec=pltpu.PrefetchScalarGridSpec(
            num_scalar_prefetch=3,
            grid=(M_TOK // T_DISPATCH,),
            in_specs=[pl.BlockSpec((TOP_K, T_DISPATCH), lambda i, *_: (0, i), memory_space=pltpu.SMEM)],
            out_specs=pl.BlockSpec(memory_space=pl.ANY),
            scratch_shapes=[pltpu.SMEM((N_SLOTS,), jnp.int32), pltpu.SemaphoreType.DMA]),
        compiler_params=_params(("arbitrary",)),
        name="moe_slot_table",
    )(fill_start, fill_len, n_blk, dest)


def _expert_weight_copies(e, s, w_hbm, w_f32, sem):
    return [pltpu.make_async_copy(w_hbm[j].at[e], w_f32[j].at[s], sem.at[s, j]) for j in range(3)]


_LOG_TOP_K = TOP_K.bit_length() - 1
_LOG_T_COMBINE = T_COMBINE.bit_length() - 1
assert TOP_K == 1 << _LOG_TOP_K and T_COMBINE == 1 << _LOG_T_COMBINE


def _pair_token(p):
    return lax.shift_right_logical(p, _LOG_TOP_K)


def _pair_row(p):
    t = _pair_token(p)
    k = p & (TOP_K - 1)
    tile = lax.shift_right_logical(t, _LOG_T_COMBINE)
    return tile * (TOP_K * T_COMBINE) + k * T_COMBINE + (t & (T_COMBINE - 1))


def _expert_kernel(be_ref, nb_ref, first_ref, next_ref, slot_ref, nv_ref,
                   src_ref, dst_ref, h_hbm, wg_hbm, wu_hbm, wd_hbm, y_hbm,
                   xbuf, ybuf, row_units, wg_f32, wu_f32, wd_f32, wg_scr, wu_scr, wd_scr,
                   gsem, ssem, wsem):
    g = pl.program_id(0)
    b = g - 1
    nb = nb_ref[0]
    s = lax.rem(g + 1, 2)
    w_hbm = (wg_hbm, wu_hbm, wd_hbm)
    w_f32 = (wg_f32, wu_f32, wd_f32)

    def for_rows(n_rows, row_fn):
        n_grp = n_rows // SUBLANES

        def grp(q, carry):
            for j in range(SUBLANES):
                row_fn(q * SUBLANES + j)
            return carry

        def one(r, carry):
            row_fn(r)
            return carry

        lax.fori_loop(0, n_grp, grp, 0)
        lax.fori_loop(n_grp * SUBLANES, n_rows, one, 0)

    def gather(slot):
        def row_fn(r):
            pltpu.make_async_copy(h_hbm.at[pl.ds(src_ref[0, 0, r], 1)],
                                  xbuf.at[slot, pl.ds(r, 1)], gsem.at[slot]).start()
        for_rows(nv_ref[g], row_fn)

    def scatter(slot, n_rows):
        def row_fn(r):
            pltpu.make_async_copy(ybuf.at[slot, pl.ds(r, 1)],
                                  y_hbm.at[pl.ds(dst_ref[0, 0, r], 1)], ssem.at[slot]).start()
        for_rows(n_rows, row_fn)

    def wait_rows(sem, n_rows):
        @pl.when(n_rows > 0)
        def _():
            pltpu.make_async_copy(row_units.at[pl.ds(0, n_rows)], row_units.at[pl.ds(0, n_rows)],
                                  sem).wait()

    @pl.when(g == 0)
    def _():
        xbuf[...] = jnp.zeros_like(xbuf)
        for cp in _expert_weight_copies(be_ref[0], 0, w_hbm, w_f32, wsem):
            cp.start()

    for slot in range(2):
        @pl.when((g < nb) & (g % 2 == slot))
        def _():
            gather(slot)

    @pl.when((g >= 1) & (b < nb))
    def _():
        n_rows = nv_ref[b]
        wait_rows(gsem.at[s], n_rows)

        @pl.when(first_ref[b] == 1)
        def _():
            ws = slot_ref[b]
            for cp in _expert_weight_copies(be_ref[b], ws, w_hbm, w_f32, wsem):
                cp.wait()

            @pl.when(next_ref[b] >= 0)
            def _():
                for cp in _expert_weight_copies(next_ref[b], 1 - ws, w_hbm, w_f32, wsem):
                    cp.start()

            wg_scr[...] = wg_f32[ws].astype(BF16)
            wu_scr[...] = wu_f32[ws].astype(BF16)
            wd_scr[...] = wd_f32[ws].astype(BF16)

        @pl.when(b >= 2)
        def _():
            wait_rows(ssem.at[s], nv_ref[jnp.maximum(b - 2, 0)])

        x = xbuf[s].astype(BF16)
        act = _silu(_mm(x, wg_scr[...])) * _mm(x, wu_scr[...])
        ybuf[s] = _mm(act.astype(BF16), wd_scr[...])

        for slot in range(2):
            @pl.when(s == slot)
            def _():
                scatter(slot, n_rows)

        @pl.when(b == nb - 1)
        def _():
            @pl.when(b >= 1)
            def _():
                wait_rows(ssem.at[1 - s], nv_ref[jnp.maximum(b - 1, 0)])
            wait_rows(ssem.at[s], n_rows)


def _experts(blk_expert, n_blk, blk_first, blk_next, blk_slot, blk_rows, src_rows, dst_rows,
             h, w_g, w_u, w_d):
    hbm = pl.BlockSpec(memory_space=pl.ANY)
    tab_blk = (1, 1, MOE_BLK)
    last = N_MOE_BLOCKS - 1
    return pl.pallas_call(
        _expert_kernel,
        out_shape=jax.ShapeDtypeStruct((N_PAIRS, D_MODEL), F32),
        grid_spec=pltpu.PrefetchScalarGridSpec(
            num_scalar_prefetch=6,
            grid=(N_MOE_BLOCKS + 1,),
            in_specs=[pl.BlockSpec(tab_blk, lambda g, *_: (jnp.minimum(g, last), 0, 0),
                                   memory_space=pltpu.SMEM),
                      pl.BlockSpec(tab_blk, lambda g, *_: (jnp.clip(g - 1, 0, last), 0, 0),
                                   memory_space=pltpu.SMEM),
                      hbm, hbm, hbm, hbm],
            out_specs=hbm,
            scratch_shapes=[pltpu.VMEM((2, MOE_BLK, D_MODEL), F32),
                            pltpu.VMEM((2, MOE_BLK, D_MODEL), F32),
                            pltpu.VMEM((MOE_BLK, D_MODEL // LANES, LANES), F32),
                            pltpu.VMEM((2, D_MODEL, EXPERT_FF), F32),
                            pltpu.VMEM((2, D_MODEL, EXPERT_FF), F32),
                            pltpu.VMEM((2, EXPERT_FF, D_MODEL), F32),
                            pltpu.VMEM((D_MODEL, EXPERT_FF), BF16),
                            pltpu.VMEM((D_MODEL, EXPERT_FF), BF16),
                            pltpu.VMEM((EXPERT_FF, D_MODEL), BF16),
                            pltpu.SemaphoreType.DMA((2,)),
                            pltpu.SemaphoreType.DMA((2,)),
                            pltpu.SemaphoreType.DMA((2, 3))]),
        compiler_params=_params(("arbitrary",)),
        name="moe_experts",
    )(blk_expert, n_blk, blk_first, blk_next, blk_slot, blk_rows,
      src_rows, dst_rows, h, w_g, w_u, w_d)


def _combine_kernel(y_ref, wt_ref, h_ref, x_ref, mod_ref, g_ref,
                    wsg_ref, wsu_ref, wsd_ref, oc_ref, ol_ref, *, n_ctx):
    i = pl.program_id(0)
    tc = T_COMBINE
    wt = wt_ref[...]
    acc = y_ref[0:tc, :] * wt[:, 0:1]
    for k in range(1, TOP_K):
        acc = acc + y_ref[k * tc:(k + 1) * tc, :] * wt[:, k:k + 1]
    h = h_ref[...].astype(BF16)
    act = _silu(_mm(h, wsg_ref[...])) * _mm(h, wsu_ref[...])
    y = acc + _mm(act.astype(BF16), wsd_ref[...])
    out = x_ref[...] + mod_ref[0, 5:6, :] * _rms(y, g_ref[...])

    @pl.when(i < n_ctx)
    def _():
        oc_ref[...] = out

    @pl.when(i >= n_ctx)
    def _():
        ol_ref[...] = out


def _combine(y_pairs, wt_tok, h, x1, mods, g_post, ws_g, ws_u, ws_d):
    tc = T_COMBINE
    n = M_TOK // tc
    n_ctx = N_CTX_TOK // tc
    row = lambda i: (i, 0)
    const = lambda i: (0, 0)
    return pl.pallas_call(
        functools.partial(_combine_kernel, n_ctx=n_ctx),
        out_shape=(jax.ShapeDtypeStruct((N_CTX_TOK, D_MODEL), F32),
                   jax.ShapeDtypeStruct((M_TOK - N_CTX_TOK, D_MODEL), F32)),
        grid=(n,),
        in_specs=[pl.BlockSpec((TOP_K * tc, D_MODEL), row),
                  pl.BlockSpec((tc, LANES), row),
                  pl.BlockSpec((tc, D_MODEL), row),
                  pl.BlockSpec((tc, D_MODEL), row),
                  pl.BlockSpec((1, SUBLANES, D_MODEL), lambda i: (i * tc // ROWS_PER_MOD, 0, 0)),
                  pl.BlockSpec((1, D_MODEL), const),
                  pl.BlockSpec((D_MODEL, EXPERT_FF), const),
                  pl.BlockSpec((D_MODEL, EXPERT_FF), const),
                  pl.BlockSpec((EXPERT_FF, D_MODEL), const)],
        out_specs=(pl.BlockSpec((tc, D_MODEL), lambda i: (jnp.minimum(i, n_ctx - 1), 0)),
                   pl.BlockSpec((tc, D_MODEL), lambda i: (jnp.maximum(i - n_ctx, 0), 0))),
        compiler_params=_params(("arbitrary",)),
        name="moe_combine",
    )(y_pairs, wt_tok, h, x1, mods, g_post, ws_g, ws_u, ws_d)


def _grid_pos_embed(n_tokens):
    rows = n_tokens // GRID_W
    half = D_MODEL // 2
    quarter = half // 2
    omega = 1.0 / (10000.0 ** (jnp.arange(quarter, dtype=F32) / quarter))

    def axis_embed(pos):
        ang = pos.astype(F32)[:, None] * omega
        return jnp.concatenate([jnp.sin(ang), jnp.cos(ang)], axis=-1)

    e_row = axis_embed(jnp.arange(rows))
    e_col = axis_embed(jnp.arange(GRID_W))
    emb = jnp.concatenate([jnp.broadcast_to(e_row[:, None], (rows, GRID_W, half)),
                           jnp.broadcast_to(e_col[None], (rows, GRID_W, half))], axis=-1)
    return emb.reshape(rows * GRID_W, D_MODEL)


def _lane_row(v, lane0):
    return jnp.zeros((1, LANES), F32).at[0, lane0:lane0 + v.shape[0]].set(v)


def kernel(x_prompt, x_sample, state_gla, state_ssd, c, c_ctx, w_ada, b_ada, g_mix_pre, g_mix_post, w_in, w_gk_up, b_gk, g_gla_norm, conv_w, conv_b, dt_bias, a_log, d_skip, g_ssd_norm, w_out, g_ffn_pre, g_ffn_post, w_router, b_router, w_exp_gate, w_exp_up, w_exp_down, w_sh_gate, w_sh_up, w_sh_down):
    assert x_prompt.shape == (N_CTX_SEQ, CTX_LEN, D_MODEL) and x_sample.shape == (N_LAT_SEQ, LAT_LEN, D_MODEL)
    assert w_ada.shape[0] == 1, "single layer"
    l = 0

    w_in_l = w_in[l]
    o_q = 0
    o_k = o_q + GLA_KEY_WIDTH
    o_v = o_k + GLA_KEY_WIDTH
    o_gout = o_v + GLA_WIDTH
    o_lr = o_gout + GLA_WIDTH
    o_z = o_lr + N_DIR * GLA_GATE_RANK
    o_xbc = o_z + SSD_WIDTH
    o_dt = o_xbc + SSD_CONV_CH
    o_end = o_dt + N_DIR * SSD_HEADS
    assert o_end == w_in_l.shape[1]
    w_main = jnp.concatenate([w_in_l[:, o_gout:o_lr], w_in_l[:, o_z:o_xbc], w_in_l[:, o_v:o_gout],
                              w_in_l[:, o_xbc:o_dt], w_in_l[:, o_q:o_k], w_in_l[:, o_k:o_v]],
                             axis=1).astype(BF16)
    w_small = jnp.concatenate([w_in_l[:, o_lr:o_z], w_in_l[:, o_dt:o_end],
                               jnp.zeros((D_MODEL, LANES - SM_DT - N_DIR * SSD_HEADS), F32)], axis=1)
    w_out_bf = w_out[l].astype(BF16)
    wr_t = w_router[l].T
    wr_hi = wr_t.astype(BF16)
    wr_lo = (wr_t - wr_hi.astype(F32)).astype(BF16)
    conv_w8 = jnp.zeros((SUBLANES, SSD_CONV_CH), F32).at[:conv_w.shape[1]].set(conv_w[l])
    d_skip_x = jnp.repeat(d_skip[l], SSD_HEAD_DIM)[None, :]
    g_gla = g_gla_norm[l][None, :]
    g_ssd = g_ssd_norm[l][None, :]

    cvecs = jnp.zeros((SUBLANES, D_MODEL), F32).at[0].set(c_ctx).at[1:1 + N_LAT_SEQ].set(c)
    mod_flat = _ada_mod(cvecs, w_ada[l], b_ada[l][None, :])
    mods = jnp.zeros((N_MOD, SUBLANES, D_MODEL), F32).at[:, :6].set(
        mod_flat[:N_MOD].reshape(N_MOD, 6, D_MODEL))

    x_ctx = x_prompt.reshape(N_CTX_TOK, D_MODEL)
    x_lat = x_sample.reshape(N_LAT_SEQ * LAT_LEN, D_MODEL)
    pos = _grid_pos_embed(LAT_LEN)

    proj_main, proj_small = _in_proj(x_ctx, x_lat, pos, mods, g_mix_pre[l][None, :], w_main, w_small)

    o_dir, s_gla_dir, y_dir, s_ssd_dir = [], [], [], []
    for d in range(N_DIR):
        rev = d == 1
        wgk_pad = jnp.zeros((LANES, GLA_KEY_WIDTH), F32).at[
            SM_LR + d * GLA_GATE_RANK:SM_LR + (d + 1) * GLA_GATE_RANK].set(w_gk_up[l, d])
        o_d, s_d = _gla_scan(proj_main, proj_small, wgk_pad, b_gk[l, d][None, :],
                             state_gla[:, l, d], rev)
        o_dir.append(o_d)
        s_gla_dir.append(s_d)
        lane0 = SM_DT + d * SSD_HEADS
        dtb_r = _lane_row(dt_bias[l, d], lane0)
        nea_r = _lane_row(-jnp.exp(a_log[l, d]), lane0)
        s0_ssd = state_ssd[:, l, d].reshape(N_LAT_SEQ, SSD_GROUPS, SSD_GROUP_WIDTH, SSD_STATE)
        y_d, t_d = _ssd_scan(proj_main, proj_small, conv_w8, conv_b[l][None, :],
                             dtb_r, nea_r, dtb_r.T, nea_r.T, d_skip_x, s0_ssd, rev, d)
        y_dir.append(y_d)
        s_ssd_dir.append(t_d.reshape(N_CTX_SEQ, SSD_HEADS, SSD_HEAD_DIM, SSD_STATE))
    new_state_gla = jnp.stack(s_gla_dir, axis=1)[:, None]
    new_state_ssd = jnp.stack(s_ssd_dir, axis=1)[:, None]

    x1 = _out_proj(o_dir[0], o_dir[1], y_dir[0], y_dir[1], proj_main, x_ctx, x_lat, pos, mods,
                   g_gla, g_ssd, g_mix_post[l][None, :], w_out_bf)

    h_ffn, idx, wt_tok, rank, cnt = _router(x1, mods, g_ffn_pre[l][None, :], wr_hi, wr_lo,
                                               b_router[l][:, None])
    i32 = jnp.int32
    e_ids = jnp.arange(N_EXPERTS, dtype=i32)
    counts = cnt[:, 0].astype(i32)
    padded = (counts + MOE_BLK - 1) // MOE_BLK * MOE_BLK
    pad_end = jnp.sum(jnp.where(e_ids[None, :] <= e_ids[:, None], padded[None, :], 0), axis=1)
    pad_start = pad_end - padded
    slot0 = jnp.sum(jnp.where(idx[:, :, None] == e_ids, pad_start, 0), axis=-1)
    dest = slot0 + rank
    n_blk = pad_end[-1] // MOE_BLK
    blk_ids = jnp.arange(N_MOE_BLOCKS, dtype=i32)
    blk_src = jnp.minimum(blk_ids, n_blk - 1)
    blk_expert = jnp.minimum(jnp.sum((pad_end[None, :] <= (blk_src * MOE_BLK)[:, None]).astype(i32), axis=1),
                             N_EXPERTS - 1)
    blk_hot = blk_expert[:, None] == e_ids
    nonempty = counts > 0
    ordinal = jnp.sum(jnp.where((e_ids[None, :] < e_ids[:, None]) & nonempty[None, :], 1, 0), axis=1)
    next_e = jnp.min(jnp.where((e_ids[None, :] > e_ids[:, None]) & nonempty[None, :], e_ids[None, :],
                               N_EXPERTS), axis=1)
    next_e = jnp.where(next_e == N_EXPERTS, -1, next_e)
    blk_first = jnp.concatenate([jnp.ones((1,), i32), (blk_expert[1:] != blk_expert[:-1]).astype(i32)])
    blk_next = jnp.sum(jnp.where(blk_hot, next_e, 0), axis=1).astype(i32)
    blk_slot = (jnp.sum(jnp.where(blk_hot, ordinal, 0), axis=1) % 2).astype(i32)
    blk_rows = jnp.clip(jnp.sum(jnp.where(blk_hot, pad_start + counts, 0), axis=1) - blk_ids * MOE_BLK,
                        0, MOE_BLK)
    blk_rows = jnp.where(blk_ids < n_blk, blk_rows, 0).astype(i32)
    n_blk_arr = n_blk.astype(i32)[None]
    slot_pair = jnp.maximum(_slot_table((pad_start + counts).astype(i32), (padded - counts).astype(i32),
                                        n_blk_arr, dest), 0).reshape(N_MOE_BLOCKS, 1, MOE_BLK)
    y_pairs = _experts(blk_expert.astype(i32), n_blk_arr, blk_first, blk_next, blk_slot, blk_rows,
                       _pair_token(slot_pair), _pair_row(slot_pair),
                       h_ffn, w_exp_gate[l], w_exp_up[l], w_exp_down[l])
    out_ctx, out_lat = _combine(y_pairs, wt_tok, h_ffn, x1, mods, g_ffn_post[l][None, :],
                                w_sh_gate[l].astype(BF16), w_sh_up[l].astype(BF16),
                                w_sh_down[l].astype(BF16))
    return (out_ctx.reshape(N_CTX_SEQ, CTX_LEN, D_MODEL), out_lat.reshape(N_LAT_SEQ, LAT_LEN, D_MODEL),
            new_state_gla, new_state_ssd)
```

```python
import functools

import numpy as np
import jax
import jax.numpy as jnp
from jax import lax
from jax.experimental import pallas as pl
from jax.experimental.pallas import tpu as pltpu

F32 = jnp.float32
BF16 = jnp.bfloat16

D_MODEL = 2048
N_CTX_SEQ = 16
CTX_LEN = 256
N_LAT_SEQ = 2
LAT_LEN = 4096
GRID_W = 64
EPS = 1e-6
N_CTX_TOK = N_CTX_SEQ * CTX_LEN
M_TOK = N_CTX_TOK + N_LAT_SEQ * LAT_LEN
SEQ_LENS = (CTX_LEN,) * N_CTX_SEQ + (LAT_LEN,) * N_LAT_SEQ
ROWS_PER_MOD = 4096
N_MOD = M_TOK // ROWS_PER_MOD

GLA_HEADS = 4
GLA_DK = 128
GLA_DV = 256
GLA_KEY_WIDTH = GLA_HEADS * GLA_DK
GLA_WIDTH = GLA_HEADS * GLA_DV
GLA_GATE_RANK = 16
GLA_GATE_TAU = 16.0
GLA_CHUNK = 64
GLA_ROWS = 256

SSD_HEADS = 16
SSD_HEAD_DIM = 64
SSD_GROUPS = 2
SSD_HPG = SSD_HEADS // SSD_GROUPS
SSD_STATE = 128
SSD_WIDTH = SSD_HEADS * SSD_HEAD_DIM
SSD_GROUP_WIDTH = SSD_WIDTH // SSD_GROUPS
SSD_CHUNK = 128
SSD_CONV_CH = SSD_WIDTH + 2 * SSD_GROUPS * SSD_STATE
N_DIR = 2

N_EXPERTS = 256
TOP_K = 8
N_EXPERT_GROUPS = 8
GROUP_SIZE = N_EXPERTS // N_EXPERT_GROUPS
TOPK_GROUPS = 4
EXPERT_FF = 512
ROUTED_SCALE = 2.5

C_GOUT = 0
C_Z = C_GOUT + GLA_WIDTH
C_V = C_Z + SSD_WIDTH
C_XBC = C_V + GLA_WIDTH
C_Q = C_XBC + SSD_CONV_CH
C_K = C_Q + GLA_KEY_WIDTH
MAIN_WIDTH = C_K + GLA_KEY_WIDTH
LANES = 128
SUBLANES = 8
SM_LR = 0
SM_DT = N_DIR * GLA_GATE_RANK

TM_PROJ = 1024
TN_PROJ = 512
TM_OUT = 256
TM_ROUTE = 256
MOE_BLK = 128
N_PAIRS = M_TOK * TOP_K
N_MOE_BLOCKS = N_PAIRS // MOE_BLK + N_EXPERTS
N_SLOTS = N_MOE_BLOCKS * MOE_BLK
T_DISPATCH = 512
T_COMBINE = 128
ADA_TN = 1024
VMEM_LIMIT = 52 * 1024 * 1024


def _mm(a, b):
    return jnp.dot(a, b, preferred_element_type=F32)


def _mm_nt(a, b):
    return lax.dot_general(a, b, (((1,), (1,)), ((), ())), preferred_element_type=F32)


def _mm_tn(a, b):
    return lax.dot_general(a, b, (((0,), (0,)), ((), ())), preferred_element_type=F32)


def _split2(x):
    hi = x.astype(BF16)
    lo = (x - hi.astype(F32)).astype(BF16)
    return hi, lo


def _split3(x):
    hi = x.astype(BF16)
    r = x - hi.astype(F32)
    mid = r.astype(BF16)
    lo = (r - mid.astype(F32)).astype(BF16)
    return hi, mid, lo


def _mm_x3(a, b):
    a_hi, a_lo = _split2(a)
    b_hi, b_lo = _split2(b)
    return _mm(a_hi, b_hi) + _mm(a_lo, b_hi) + _mm(a_hi, b_lo)


def _mm_sel_left(sel_bf, x):
    hi, mid, lo = _split3(x)
    return _mm(sel_bf, hi) + _mm(sel_bf, mid) + _mm(sel_bf, lo)


def _mm_sel_right(x, sel_bf):
    hi, mid, lo = _split3(x)
    return _mm(hi, sel_bf) + _mm(mid, sel_bf) + _mm(lo, sel_bf)


def _sigmoid(x):
    return 1.0 / (1.0 + jnp.exp(-x))


def _silu(x):
    return x * _sigmoid(x)


def _softplus(x):
    return jnp.maximum(x, 0.0) + jnp.log1p(jnp.exp(-jnp.abs(x)))


def _rms(x, g):
    return x * lax.rsqrt(jnp.mean(x * x, axis=-1, keepdims=True) + EPS) * g


def _params(sem, vmem=VMEM_LIMIT):
    return pltpu.CompilerParams(dimension_semantics=sem, vmem_limit_bytes=vmem)


def _ada_kernel(c_ref, w_ref, b_ref, o_ref):
    o_ref[...] = _mm_x3(_silu(c_ref[...]), w_ref[...]) + b_ref[...]


def _ada_mod(cvecs, w_ada, b_ada):
    n_out = w_ada.shape[1]
    return pl.pallas_call(
        _ada_kernel,
        out_shape=jax.ShapeDtypeStruct((SUBLANES, n_out), F32),
        grid=(n_out // ADA_TN,),
        in_specs=[pl.BlockSpec((SUBLANES, D_MODEL), lambda j: (0, 0)),
                  pl.BlockSpec((D_MODEL, ADA_TN), lambda j: (0, j)),
                  pl.BlockSpec((1, ADA_TN), lambda j: (0, j))],
        out_specs=pl.BlockSpec((SUBLANES, ADA_TN), lambda j: (0, j)),
        compiler_params=_params(("arbitrary",)),
        name="ada_mod",
    )(cvecs, w_ada, b_ada)


def _token_specs(tm, buffers=2):
    n_ctx = N_CTX_TOK // tm
    n_pos = LAT_LEN // tm
    mode = dict(pipeline_mode=pl.Buffered(buffers)) if buffers != 2 else {}
    ctx = pl.BlockSpec((tm, D_MODEL), lambda i, *_: (jnp.minimum(i, n_ctx - 1), 0), **mode)
    lat = pl.BlockSpec((tm, D_MODEL), lambda i, *_: (jnp.maximum(i - n_ctx, 0), 0), **mode)
    pos = pl.BlockSpec((tm, D_MODEL), lambda i, *_: (jnp.maximum(i - n_ctx, 0) % n_pos, 0), **mode)
    return n_ctx, [ctx, lat, pos]


def _inproj_kernel(xc_ref, xl_ref, pos_ref, mod_ref, g_ref, w_ref, ws_ref, o_ref, os_ref, h_scr, *, n_ctx):
    def prologue(x):
        h = _rms(x, g_ref[...]) * (1.0 + mod_ref[0, 1:2, :]) + mod_ref[0, 0:1, :]
        h_hi, h_lo = _split2(h)
        h_scr[...] = h_hi
        ws_hi, ws_lo = _split2(ws_ref[...])
        os_ref[...] = _mm(h_hi, ws_hi) + _mm(h_lo, ws_hi) + _mm(h_hi, ws_lo)

    first = pl.program_id(1) == 0
    is_ctx = pl.program_id(0) < n_ctx

    @pl.when(first & is_ctx)
    def _():
        prologue(xc_ref[...])

    @pl.when(first & jnp.logical_not(is_ctx))
    def _():
        prologue(xl_ref[...] + pos_ref[...])

    o_ref[...] = _mm(h_scr[...], w_ref[...])


def _in_proj(x_ctx, x_lat, pos, mods, g_pre, w_main, w_small):
    tm, tn = TM_PROJ, TN_PROJ
    n_ctx, tok_specs = _token_specs(tm, buffers=1)
    return pl.pallas_call(
        functools.partial(_inproj_kernel, n_ctx=n_ctx),
        out_shape=(jax.ShapeDtypeStruct((M_TOK, MAIN_WIDTH), F32),
                   jax.ShapeDtypeStruct((M_TOK, LANES), F32)),
        grid=(M_TOK // tm, MAIN_WIDTH // tn),
        in_specs=tok_specs + [
                  pl.BlockSpec((1, SUBLANES, D_MODEL), lambda i, j: (i * tm // ROWS_PER_MOD, 0, 0)),
                  pl.BlockSpec((1, D_MODEL), lambda i, j: (0, 0)),
                  pl.BlockSpec((D_MODEL, tn), lambda i, j: (0, j)),
                  pl.BlockSpec((D_MODEL, LANES), lambda i, j: (0, 0))],
        out_specs=(pl.BlockSpec((tm, tn), lambda i, j: (i, j)),
                   pl.BlockSpec((tm, LANES), lambda i, j: (i, 0))),
        scratch_shapes=[pltpu.VMEM((tm, D_MODEL), BF16)],
        compiler_params=_params(("arbitrary", "arbitrary")),
        name="in_proj",
    )(x_ctx, x_lat, pos, mods, g_pre, w_main, w_small)


def _scan_schedule(rows_per_step, reverse):
    blk, flag, s0i, soi, emit, has_prev, has_next = [], [], [], [], [], [], []
    start = 0
    for s, length in enumerate(SEQ_LENS):
        nb = length // rows_per_step
        is_ctx = s < N_CTX_SEQ
        order = range(nb - 1, -1, -1) if reverse else range(nb)
        for n, b in enumerate(order):
            blk.append(start + b)
            flag.append((1 if is_ctx else 2) if n == 0 else 0)
            s0i.append(0 if is_ctx else s - N_CTX_SEQ)
            soi.append(s if is_ctx else N_CTX_SEQ - 1)
            emit.append(1 if (is_ctx and n == nb - 1) else 0)
            has_prev.append(1 if b > 0 else 0)
            has_next.append(1 if b < nb - 1 else 0)
        start += nb
    return tuple(jnp.asarray(np.array(a, np.int32)) for a in (blk, flag, s0i, soi, emit, has_prev, has_next))


def _gla_kernel(blk_ref, flag_ref, s0i_ref, soi_ref, emit_ref,
                q_ref, k_ref, v_ref, sm_ref, wgk_ref, bgk_ref, s0_ref,
                o_ref, so_ref, st_scr, *, reverse):
    i = pl.program_id(0)
    flag = flag_ref[i]

    @pl.when(flag == 1)
    def _():
        st_scr[...] = jnp.zeros_like(st_scr)

    @pl.when(flag == 2)
    def _():
        for h in range(GLA_HEADS):
            st_scr[h] = s0_ref[0, h].T

    c = GLA_CHUNK
    r_id = lax.broadcasted_iota(jnp.int32, (c, c), 0)
    c_id = lax.broadcasted_iota(jnp.int32, (c, c), 1)
    tri = (c_id >= r_id) if reverse else (c_id <= r_id)
    tri_bf = jnp.where(tri, 1.0, 0.0).astype(BF16)

    gk = _mm_x3(sm_ref[...], wgk_ref[...]) + bgk_ref[...]
    log_a = (jnp.minimum(gk, 0.0) - jnp.log1p(jnp.exp(-jnp.abs(gk)))) * (1.0 / GLA_GATE_TAU)

    n_chunks = GLA_ROWS // c
    for ci in (range(n_chunks - 1, -1, -1) if reverse else range(n_chunks)):
        lo = ci * c
        b_all = _mm_sel_left(tri_bf, log_a[lo:lo + c])
        for h in range(GLA_HEADS):
            kc = slice(h * GLA_DK, (h + 1) * GLA_DK)
            vc = slice(h * GLA_DV, (h + 1) * GLA_DV)
            b = b_all[:, kc]
            b_end = b[0:1] if reverse else b[c - 1:c]
            q = q_ref[lo:lo + c, kc] * (GLA_DK ** -0.5)
            k = k_ref[lo:lo + c, kc]
            v = v_ref[lo:lo + c, vc].astype(BF16)
            q_e = (q * jnp.exp(b)).astype(BF16)
            k_e = (k * jnp.exp(-b)).astype(BF16)
            att = jnp.where(tri, _mm_nt(q_e, k_e), 0.0).astype(BF16)
            st = st_scr[h]
            o_ref[lo:lo + c, vc] = _mm(att, v) + _mm_nt(q_e, st.astype(BF16))
            k_end = (k * jnp.exp(b_end - b)).astype(BF16)
            st_scr[h] = st * jnp.exp(b_end) + _mm_tn(v, k_end)

    @pl.when(emit_ref[i] == 1)
    def _():
        for h in range(GLA_HEADS):
            so_ref[0, h] = st_scr[h].T


def _gla_scan(proj_main, proj_small, wgk_pad, bgk, s0, reverse):
    sched = _scan_schedule(GLA_ROWS, reverse)[:5]
    n_steps = M_TOK // GLA_ROWS
    t = GLA_ROWS
    q_blk, k_blk, v_blk = C_Q // GLA_KEY_WIDTH, C_K // GLA_KEY_WIDTH, C_V // GLA_WIDTH
    state_blk = (1, GLA_HEADS, GLA_DK, GLA_DV)
    return pl.pallas_call(
        functools.partial(_gla_kernel, reverse=reverse),
        out_shape=(jax.ShapeDtypeStruct((M_TOK, GLA_WIDTH), F32),
                   jax.ShapeDtypeStruct((N_CTX_SEQ,) + state_blk[1:], F32)),
        grid_spec=pltpu.PrefetchScalarGridSpec(
            num_scalar_prefetch=5,
            grid=(n_steps,),
            in_specs=[
                pl.BlockSpec((t, GLA_KEY_WIDTH), lambda i, blk, *_: (blk[i], q_blk)),
                pl.BlockSpec((t, GLA_KEY_WIDTH), lambda i, blk, *_: (blk[i], k_blk)),
                pl.BlockSpec((t, GLA_WIDTH), lambda i, blk, *_: (blk[i], v_blk)),
                pl.BlockSpec((t, LANES), lambda i, blk, *_: (blk[i], 0)),
                pl.BlockSpec((LANES, GLA_KEY_WIDTH), lambda i, *_: (0, 0)),
                pl.BlockSpec((1, GLA_KEY_WIDTH), lambda i, *_: (0, 0)),
                pl.BlockSpec(state_blk, lambda i, blk, flag, s0i, *_: (s0i[i], 0, 0, 0)),
            ],
            out_specs=(
                pl.BlockSpec((t, GLA_WIDTH), lambda i, blk, *_: (blk[i], 0)),
                pl.BlockSpec(state_blk, lambda i, blk, flag, s0i, soi, *_: (soi[i], 0, 0, 0)),
            ),
            scratch_shapes=[pltpu.VMEM((GLA_HEADS, GLA_DV, GLA_DK), F32)]),
        compiler_params=_params(("arbitrary",)),
        name="gla_bwd" if reverse else "gla_fwd",
    )(*sched, proj_main, proj_main, proj_main, proj_small, wgk_pad, bgk, s0)


def _ssd_kernel(blk_ref, flag_ref, s0i_ref, soi_ref, emit_ref, hp_ref, hn_ref,
                xbc_ref, xprev_ref, xnext_ref, sm_ref, cw_ref, cb_ref,
                dtb_r_ref, nea_r_ref, dtb_c_ref, nea_c_ref, dsk_ref, s0_ref,
                y_ref, so_ref, st_scr, *, reverse, lane0, add_skip):
    i = pl.program_id(0)
    flag = flag_ref[i]
    t = SSD_CHUNK

    @pl.when(flag == 1)
    def _():
        st_scr[...] = jnp.zeros_like(st_scr)

    @pl.when(flag == 2)
    def _():
        for g in range(SSD_GROUPS):
            st_scr[g] = s0_ref[0, g].T

    xbc = xbc_ref[...]
    prev = jnp.where(hp_ref[i] == 1, xprev_ref[SUBLANES - 1:SUBLANES, :], 0.0)
    nxt = jnp.where(hn_ref[i] == 1, xnext_ref[0:1, :], 0.0)
    row = lax.broadcasted_iota(jnp.int32, xbc.shape, 0)
    x_m1 = jnp.where(row == 0, prev, pltpu.roll(xbc, 1, 0))
    x_p1 = jnp.where(row == t - 1, nxt, pltpu.roll(xbc, t - 1, 0))
    act = _silu(x_m1 * cw_ref[0:1, :] + xbc * cw_ref[1:2, :] + x_p1 * cw_ref[2:3, :] + cb_ref[...])
    xs = act[:, :SSD_WIDTH]
    bm = act[:, SSD_WIDTH:SSD_WIDTH + SSD_GROUPS * SSD_STATE]
    cm = act[:, SSD_WIDTH + SSD_GROUPS * SSD_STATE:]

    sm = sm_ref[...]
    dt = _softplus(sm + dtb_r_ref[...])
    a = dt * nea_r_ref[...]
    a_t = _softplus(sm.T + dtb_c_ref[...]) * nea_c_ref[...]

    r_id = lax.broadcasted_iota(jnp.int32, (t, t), 0)
    c_id = lax.broadcasted_iota(jnp.int32, (t, t), 1)
    tri = (c_id >= r_id) if reverse else (c_id <= r_id)
    tri_bf = jnp.where(tri, 1.0, 0.0).astype(BF16)
    tri_t_bf = jnp.where((r_id >= c_id) if reverse else (r_id <= c_id), 1.0, 0.0).astype(BF16)
    cs = _mm_sel_left(tri_bf, a)
    cs_t = _mm_sel_right(a_t, tri_t_bf)

    e_r = lax.broadcasted_iota(jnp.int32, (LANES, SSD_WIDTH), 0)
    e_c = lax.broadcasted_iota(jnp.int32, (LANES, SSD_WIDTH), 1)
    expand = jnp.where(e_r - lane0 == e_c // SSD_HEAD_DIM, 1.0, 0.0).astype(BF16)
    dt_x = _mm_sel_right(dt, expand)
    cs_x = _mm_sel_right(cs, expand)
    cs_end_x = cs_x[0:1] if reverse else cs_x[t - 1:t]
    x_in = xs * dt_x
    x_bf = x_in.astype(BF16)
    x_w = (x_in * jnp.exp(cs_end_x - cs_x)).astype(BF16)
    decay_in = jnp.exp(cs_x)
    decay_end = jnp.exp(cs_end_x)

    for g in range(SSD_GROUPS):
        gc = slice(g * SSD_STATE, (g + 1) * SSD_STATE)
        gw = slice(g * SSD_GROUP_WIDTH, (g + 1) * SSD_GROUP_WIDTH)
        c_g = cm[:, gc].astype(BF16)
        b_g = bm[:, gc].astype(BF16)
        cb = _mm_nt(c_g, b_g)
        st = st_scr[g]
        y_state = _mm(c_g, st.astype(BF16)) * decay_in[:, gw]
        for r in range(SSD_HPG):
            hh = g * SSD_HPG + r
            hc = slice(hh * SSD_HEAD_DIM, (hh + 1) * SSD_HEAD_DIM)
            lane = lane0 + hh
            seg = cs[:, lane:lane + 1] - cs_t[lane:lane + 1, :]
            lm = jnp.exp(jnp.where(tri, seg, -jnp.inf))
            y_h = _mm((cb * lm).astype(BF16), x_bf[:, hc]) + y_state[:, r * SSD_HEAD_DIM:(r + 1) * SSD_HEAD_DIM]
            if add_skip:
                y_h = y_h + xs[:, hc] * dsk_ref[:, hc]
            y_ref[:, hc] = y_h
        st_scr[g] = st * decay_end[:, gw] + _mm_tn(b_g, x_w[:, gw])

    @pl.when(emit_ref[i] == 1)
    def _():
        for g in range(SSD_GROUPS):
            so_ref[0, g] = st_scr[g].T


def _ssd_scan(proj_main, proj_small, conv_w, conv_b, dtb_r, nea_r, dtb_c, nea_c, d_skip_x, s0,
              reverse, direction):
    sched = _scan_schedule(SSD_CHUNK, reverse)
    t = SSD_CHUNK
    n_steps = M_TOK // t
    xbc_blk = C_XBC // SSD_CONV_CH
    rb = t // SUBLANES
    n_rb = M_TOK // SUBLANES
    state_blk = (1, SSD_GROUPS, SSD_GROUP_WIDTH, SSD_STATE)
    return pl.pallas_call(
        functools.partial(_ssd_kernel, reverse=reverse, lane0=SM_DT + direction * SSD_HEADS,
                          add_skip=not reverse),
        out_shape=(jax.ShapeDtypeStruct((M_TOK, SSD_WIDTH), F32),
                   jax.ShapeDtypeStruct((N_CTX_SEQ,) + state_blk[1:], F32)),
        grid_spec=pltpu.PrefetchScalarGridSpec(
            num_scalar_prefetch=7,
            grid=(n_steps,),
            in_specs=[
                pl.BlockSpec((t, SSD_CONV_CH), lambda i, blk, *_: (blk[i], xbc_blk)),
                pl.BlockSpec((SUBLANES, SSD_CONV_CH),
                             lambda i, blk, *_: (jnp.maximum(blk[i] * rb - 1, 0), xbc_blk)),
                pl.BlockSpec((SUBLANES, SSD_CONV_CH),
                             lambda i, blk, *_: (jnp.minimum((blk[i] + 1) * rb, n_rb - 1), xbc_blk)),
                pl.BlockSpec((t, LANES), lambda i, blk, *_: (blk[i], 0)),
                pl.BlockSpec((SUBLANES, SSD_CONV_CH), lambda i, *_: (0, 0)),
                pl.BlockSpec((1, SSD_CONV_CH), lambda i, *_: (0, 0)),
                pl.BlockSpec((1, LANES), lambda i, *_: (0, 0)),
                pl.BlockSpec((1, LANES), lambda i, *_: (0, 0)),
                pl.BlockSpec((LANES, 1), lambda i, *_: (0, 0)),
                pl.BlockSpec((LANES, 1), lambda i, *_: (0, 0)),
                pl.BlockSpec((1, SSD_WIDTH), lambda i, *_: (0, 0)),
                pl.BlockSpec(state_blk, lambda i, blk, flag, s0i, *_: (s0i[i], 0, 0, 0)),
            ],
            out_specs=(
                pl.BlockSpec((t, SSD_WIDTH), lambda i, blk, *_: (blk[i], 0)),
                pl.BlockSpec(state_blk, lambda i, blk, flag, s0i, soi, *_: (soi[i], 0, 0, 0)),
            ),
            scratch_shapes=[pltpu.VMEM((SSD_GROUPS, SSD_STATE, SSD_GROUP_WIDTH), F32)]),
        compiler_params=_params(("arbitrary",)),
        name="ssd_bwd" if reverse else "ssd_fwd",
    )(*sched, proj_main, proj_main, proj_main, proj_small, conv_w, conv_b,
      dtb_r, nea_r, dtb_c, nea_c, d_skip_x, s0)


def _outproj_kernel(of_ref, ob_ref, gout_ref, yf_ref, yb_ref, z_ref, xc_ref, xl_ref, pos_ref, mod_ref,
                    ggla_ref, gssd_ref, gpost_ref, w_ref, o_ref, *, n_ctx):
    o = of_ref[...] + ob_ref[...]
    gate = _silu(gout_ref[...])
    parts = []
    for h in range(GLA_HEADS):
        hc = slice(h * GLA_DV, (h + 1) * GLA_DV)
        parts.append((_rms(o[:, hc], ggla_ref[...]) * gate[:, hc]).astype(BF16))
    y = (yf_ref[...] + yb_ref[...]) * _silu(z_ref[...])
    for g in range(SSD_GROUPS):
        gw = slice(g * SSD_GROUP_WIDTH, (g + 1) * SSD_GROUP_WIDTH)
        parts.append(_rms(y[:, gw], gssd_ref[:, gw]).astype(BF16))
    acc = None
    col = 0
    for p in parts:
        term = _mm(p, w_ref[col:col + p.shape[1], :])
        acc = term if acc is None else acc + term
        col += p.shape[1]
    delta = mod_ref[0, 2:3, :] * _rms(acc, gpost_ref[...])
    is_ctx = pl.program_id(0) < n_ctx

    @pl.when(is_ctx)
    def _():
        o_ref[...] = xc_ref[...] + delta

    @pl.when(jnp.logical_not(is_ctx))
    def _():
        o_ref[...] = xl_ref[...] + pos_ref[...] + delta


def _out_proj(o_f, o_b, y_f, y_b, proj_main, x_ctx, x_lat, pos, mods, g_gla, g_ssd, g_post, w_out):
    tm = TM_OUT
    row = lambda i: (i, 0)
    const = lambda i: (0, 0)
    n_ctx, tok_specs = _token_specs(tm)
    return pl.pallas_call(
        functools.partial(_outproj_kernel, n_ctx=n_ctx),
        out_shape=jax.ShapeDtypeStruct((M_TOK, D_MODEL), F32),
        grid=(M_TOK // tm,),
        in_specs=[pl.BlockSpec((tm, GLA_WIDTH), row),
                  pl.BlockSpec((tm, GLA_WIDTH), row),
                  pl.BlockSpec((tm, GLA_WIDTH), lambda i: (i, C_GOUT // GLA_WIDTH)),
                  pl.BlockSpec((tm, SSD_WIDTH), row),
                  pl.BlockSpec((tm, SSD_WIDTH), row),
                  pl.BlockSpec((tm, SSD_WIDTH), lambda i: (i, C_Z // SSD_WIDTH))] + tok_specs + [
                  pl.BlockSpec((1, SUBLANES, D_MODEL), lambda i: (i * tm // ROWS_PER_MOD, 0, 0)),
                  pl.BlockSpec((1, GLA_DV), const),
                  pl.BlockSpec((1, SSD_WIDTH), const),
                  pl.BlockSpec((1, D_MODEL), const),
                  pl.BlockSpec((D_MODEL, D_MODEL), const)],
        out_specs=pl.BlockSpec((tm, D_MODEL), row),
        compiler_params=_params(("arbitrary",)),
        name="out_proj",
    )(o_f, o_b, proj_main, y_f, y_b, proj_main, x_ctx, x_lat, pos, mods, g_gla, g_ssd, g_post, w_out)


def _router_kernel(x_ref, mod_ref, g_ref, wr_hi_ref, wr_lo_ref, br_ref,
                   h_ref, idx_ref, wt_ref, rank_ref, cnt_ref, cnt_scr):
    i = pl.program_id(0)
    tm = TM_ROUTE

    @pl.when(i == 0)
    def _():
        cnt_scr[...] = jnp.zeros_like(cnt_scr)

    h = _rms(x_ref[...], g_ref[...]) * (1.0 + mod_ref[0, 4:5, :]) + mod_ref[0, 3:4, :]
    h_ref[...] = h
    h_hi, h_lo = _split2(h)
    wr_hi = wr_hi_ref[...]
    logits = _mm_nt(wr_hi, h_hi) + _mm_nt(wr_hi, h_lo) + _mm_nt(wr_lo_ref[...], h_hi)
    scores = _sigmoid(logits)
    sel = scores + br_ref[...]
    neg = -jnp.inf

    def first_argmax(x, ids, n):
        m = jnp.max(x, axis=0, keepdims=True)
        return m, jnp.min(jnp.where(x == m, ids, float(n)), axis=0, keepdims=True)

    ids_g = lax.broadcasted_iota(jnp.int32, (GROUP_SIZE, tm), 0).astype(F32)
    grp = []
    for g in range(N_EXPERT_GROUPS):
        xg = sel[g * GROUP_SIZE:(g + 1) * GROUP_SIZE]
        m1, a1 = first_argmax(xg, ids_g, GROUP_SIZE)
        m2 = jnp.max(jnp.where(ids_g == a1, neg, xg), axis=0, keepdims=True)
        grp.append(m1 + m2)
    gsc = jnp.concatenate(grp, axis=0)
    ids_8 = lax.broadcasted_iota(jnp.int32, (N_EXPERT_GROUPS, tm), 0).astype(F32)
    keep = jnp.zeros((N_EXPERT_GROUPS, tm), F32)
    for _ in range(TOPK_GROUPS):
        _, a = first_argmax(gsc, ids_8, N_EXPERT_GROUPS)
        pick = ids_8 == a
        keep = jnp.where(pick, 1.0, keep)
        gsc = jnp.where(pick, neg, gsc)
    selm = jnp.concatenate(
        [jnp.where(keep[g:g + 1] > 0.5, sel[g * GROUP_SIZE:(g + 1) * GROUP_SIZE], neg)
         for g in range(N_EXPERT_GROUPS)], axis=0)

    ids_e = lax.broadcasted_iota(jnp.int32, (N_EXPERTS, tm), 0).astype(F32)
    picks, wts = [], []
    chosen = jnp.zeros((N_EXPERTS, tm), F32)
    for _ in range(TOP_K):
        _, a = first_argmax(selm, ids_e, N_EXPERTS)
        hit = ids_e == a
        picks.append(a)
        wts.append(jnp.sum(jnp.where(hit, scores, 0.0), axis=0, keepdims=True))
        chosen = jnp.where(hit, 1.0, chosen)
        selm = jnp.where(hit, neg, selm)
    w = jnp.concatenate(wts, axis=0)
    w = w / jnp.sum(w, axis=0, keepdims=True) * ROUTED_SCALE
    idx_ref[...] = jnp.concatenate(picks, axis=0).astype(jnp.int32)
    wt_ref[...] = jnp.concatenate([w, jnp.zeros((LANES - TOP_K, tm), F32)], axis=0).T

    t_r = lax.broadcasted_iota(jnp.int32, (tm, tm), 0)
    t_c = lax.broadcasted_iota(jnp.int32, (tm, tm), 1)
    before = jnp.where(t_r < t_c, 1.0, 0.0).astype(BF16)
    base = _mm(chosen.astype(BF16), before) + cnt_scr[...]
    rank_ref[...] = jnp.concatenate(
        [jnp.sum(jnp.where(ids_e == a, base, 0.0), axis=0, keepdims=True) for a in picks],
        axis=0).astype(jnp.int32)
    cnt_scr[...] = cnt_scr[...] + jnp.sum(chosen, axis=1, keepdims=True)
    cnt_ref[...] = jnp.broadcast_to(cnt_scr[...], cnt_ref.shape)


def _router(x1, mods, g_pre, wr_hi, wr_lo, b_router):
    tm = TM_ROUTE
    const = lambda i: (0, 0)
    return pl.pallas_call(
        _router_kernel,
        out_shape=(jax.ShapeDtypeStruct((M_TOK, D_MODEL), F32),
                   jax.ShapeDtypeStruct((TOP_K, M_TOK), jnp.int32),
                   jax.ShapeDtypeStruct((M_TOK, LANES), F32),
                   jax.ShapeDtypeStruct((TOP_K, M_TOK), jnp.int32),
                   jax.ShapeDtypeStruct((N_EXPERTS, LANES), F32)),
        grid=(M_TOK // tm,),
        in_specs=[pl.BlockSpec((tm, D_MODEL), lambda i: (i, 0)),
                  pl.BlockSpec((1, SUBLANES, D_MODEL), lambda i: (i * tm // ROWS_PER_MOD, 0, 0)),
                  pl.BlockSpec((1, D_MODEL), const),
                  pl.BlockSpec((N_EXPERTS, D_MODEL), const),
                  pl.BlockSpec((N_EXPERTS, D_MODEL), const),
                  pl.BlockSpec((N_EXPERTS, 1), const)],
        out_specs=(pl.BlockSpec((tm, D_MODEL), lambda i: (i, 0)),
                   pl.BlockSpec((TOP_K, tm), lambda i: (0, i)),
                   pl.BlockSpec((tm, LANES), lambda i: (i, 0)),
                   pl.BlockSpec((TOP_K, tm), lambda i: (0, i)),
                   pl.BlockSpec((N_EXPERTS, LANES), const)),
        scratch_shapes=[pltpu.VMEM((N_EXPERTS, 1), F32)],
        compiler_params=_params(("arbitrary",)),
        name="router",
    )(x1, mods, g_pre, wr_hi, wr_lo, b_router)


def _slot_table_kernel(fill_start_ref, fill_len_ref, nb_ref, dest_ref, tab_hbm, tab, sem):
    i = pl.program_id(0)

    @pl.when(i == 0)
    def _():
        def pad_body(e, carry):
            start = fill_start_ref[e]

            def one(j, c):
                tab[start + j] = -1
                return c

            lax.fori_loop(0, fill_len_ref[e], one, 0)
            return carry

        def tail_body(s, carry):
            tab[s] = -1
            return carry

        lax.fori_loop(0, N_EXPERTS, pad_body, 0)
        lax.fori_loop(nb_ref[0] * MOE_BLK, N_SLOTS, tail_body, 0)

    base = i * T_DISPATCH

    def body(t, carry):
        for k in range(TOP_K):
            tab[dest_ref[k, t]] = (base + t) * TOP_K + k
        return carry

    lax.fori_loop(0, T_DISPATCH, body, 0)

    @pl.when(i == pl.num_programs(0) - 1)
    def _():
        cp = pltpu.make_async_copy(tab, tab_hbm, sem)
        cp.start()
        cp.wait()


def _slot_table(fill_start, fill_len, n_blk, dest):
    return pl.pallas_call(
        _slot_table_kernel,
        out_shape=jax.ShapeDtypeStruct((N_SLOTS,), jnp.int32),
        grid_spec=pltpu.PrefetchScalarGridSpec(
            num_scalar_prefetch=3,
            grid=(M_TOK // T_DISPATCH,),
            in_specs=[pl.BlockSpec((TOP_K, T_DISPATCH), lambda i, *_: (0, i), memory_space=pltpu.SMEM)],
            out_specs=pl.BlockSpec(memory_space=pl.ANY),
            scratch_shapes=[pltpu.SMEM((N_SLOTS,), jnp.int32), pltpu.SemaphoreType.DMA]),
        compiler_params=_params(("arbitrary",)),
        name="moe_slot_table",
    )(fill_start, fill_len, n_blk, dest)


def _expert_weight_copies(e, s, w_hbm, w_f32, sem):
    return [pltpu.make_async_copy(w_hbm[j].at[e], w_f32[j].at[s], sem.at[s, j]) for j in range(3)]


_LOG_TOP_K = TOP_K.bit_length() - 1
_LOG_T_COMBINE = T_COMBINE.bit_length() - 1
assert TOP_K == 1 << _LOG_TOP_K and T_COMBINE == 1 << _LOG_T_COMBINE


def _pair_token(p):
    return lax.shift_right_logical(p, _LOG_TOP_K)


def _pair_row(p):
    t = _pair_token(p)
    k = p & (TOP_K - 1)
    tile = lax.shift_right_logical(t, _LOG_T_COMBINE)
    return tile * (TOP_K * T_COMBINE) + k * T_COMBINE + (t & (T_COMBINE - 1))


def _expert_kernel(be_ref, nb_ref, first_ref, next_ref, slot_ref,
                   src_ref, dst_ref, h_hbm, wg_hbm, wu_hbm, wd_hbm, y_hbm,
                   xbuf, ybuf, wg_f32, wu_f32, wd_f32, wg_scr, wu_scr, wd_scr,
                   gsem, ssem, wsem):
    g = pl.program_id(0)
    nb = nb_ref[0]
    w_hbm = (wg_hbm, wu_hbm, wd_hbm)
    w_f32 = (wg_f32, wu_f32, wd_f32)

    def gather(slot):
        for r in range(MOE_BLK):
            pltpu.make_async_copy(h_hbm.at[pl.ds(src_ref[0, 0, r], 1)],
                                  xbuf.at[slot, pl.ds(r, 1)], gsem.at[slot]).start()

    def scatter(slot):
        for r in range(MOE_BLK):
            pltpu.make_async_copy(ybuf.at[slot, pl.ds(r, 1)],
                                  y_hbm.at[pl.ds(dst_ref[0, 0, r], 1)], ssem.at[slot]).start()

    def wait_block(buf, sem, slot):
        pltpu.make_async_copy(buf.at[slot], buf.at[slot], sem.at[slot]).wait()

    @pl.when(g == 0)
    def _():
        for cp in _expert_weight_copies(be_ref[0], 0, w_hbm, w_f32, wsem):
            cp.start()
        ybuf[0] = jnp.zeros((MOE_BLK, D_MODEL), F32)
        spare = pltpu.make_async_copy(ybuf.at[0], y_hbm.at[pl.ds(N_PAIRS, MOE_BLK)], ssem.at[0])
        spare.start()
        spare.wait()

    def stage(par, do_gather, do_compute, do_scatter):
        b = g - 1
        if do_scatter:
            @pl.when(g >= 3)
            def _():
                wait_block(ybuf, ssem, 1 - par)
        if do_compute:
            wait_block(xbuf, gsem, 1 - par)
            weights(b)
            x = xbuf[1 - par].astype(BF16)
        if do_gather:
            gather(par)
        if do_scatter:
            scatter(par)
        if do_compute:
            act = _silu(_mm(x, wg_scr[...])) * _mm(x, wu_scr[...])
            ybuf[1 - par] = _mm(act.astype(BF16), wd_scr[...])

    def weights(b):
        @pl.when(first_ref[b] == 1)
        def _():
            ws = slot_ref[b]
            for cp in _expert_weight_copies(be_ref[b], ws, w_hbm, w_f32, wsem):
                cp.wait()

            @pl.when(next_ref[b] >= 0)
            def _():
                for cp in _expert_weight_copies(next_ref[b], 1 - ws, w_hbm, w_f32, wsem):
                    cp.start()

            wg_scr[...] = wg_f32[ws].astype(BF16)
            wu_scr[...] = wu_f32[ws].astype(BF16)
            wd_scr[...] = wd_f32[ws].astype(BF16)

    for par in range(2):
        is_par = lax.rem(g, 2) == par

        @pl.when(is_par & (g == 0))
        def _():
            stage(par, True, False, False)

        @pl.when(is_par & (g == 1))
        def _():
            stage(par, True, True, False)

        @pl.when(is_par & (g >= 2) & (g < nb))
        def _():
            stage(par, True, True, True)

        @pl.when(is_par & (g == nb))
        def _():
            stage(par, False, True, True)

        @pl.when(is_par & (g == nb + 1))
        def _():
            stage(par, False, False, True)
            wait_block(ybuf, ssem, par)


def _experts(blk_expert, n_blk, blk_first, blk_next, blk_slot, src_rows, dst_rows, h, w_g, w_u, w_d):
    hbm = pl.BlockSpec(memory_space=pl.ANY)
    tab_blk = (1, 1, MOE_BLK)
    last = N_MOE_BLOCKS - 1
    return pl.pallas_call(
        _expert_kernel,
        out_shape=jax.ShapeDtypeStruct((N_PAIRS + MOE_BLK, D_MODEL), F32),
        grid_spec=pltpu.PrefetchScalarGridSpec(
            num_scalar_prefetch=5,
            grid=(N_MOE_BLOCKS + 2,),
            in_specs=[pl.BlockSpec(tab_blk, lambda g, *_: (jnp.minimum(g, last), 0, 0),
                                   memory_space=pltpu.SMEM),
                      pl.BlockSpec(tab_blk, lambda g, *_: (jnp.clip(g - 2, 0, last), 0, 0),
                                   memory_space=pltpu.SMEM),
                      hbm, hbm, hbm, hbm],
            out_specs=hbm,
            scratch_shapes=[pltpu.VMEM((2, MOE_BLK, D_MODEL), F32),
                            pltpu.VMEM((2, MOE_BLK, D_MODEL), F32),
                            pltpu.VMEM((2, D_MODEL, EXPERT_FF), F32),
                            pltpu.VMEM((2, D_MODEL, EXPERT_FF), F32),
                            pltpu.VMEM((2, EXPERT_FF, D_MODEL), F32),
                            pltpu.VMEM((D_MODEL, EXPERT_FF), BF16),
                            pltpu.VMEM((D_MODEL, EXPERT_FF), BF16),
                            pltpu.VMEM((EXPERT_FF, D_MODEL), BF16),
                            pltpu.SemaphoreType.DMA((2,)),
                            pltpu.SemaphoreType.DMA((2,)),
                            pltpu.SemaphoreType.DMA((2, 3))]),
        compiler_params=_params(("arbitrary",)),
        name="moe_experts",
    )(blk_expert, n_blk, blk_first, blk_next, blk_slot, src_rows, dst_rows, h, w_g, w_u, w_d)


def _combine_kernel(y_ref, wt_ref, h_ref, x_ref, mod_ref, g_ref,
                    wsg_ref, wsu_ref, wsd_ref, oc_ref, ol_ref, *, n_ctx):
    i = pl.program_id(0)
    tc = T_COMBINE
    wt = wt_ref[...]
    acc = y_ref[0:tc, :] * wt[:, 0:1]
    for k in range(1, TOP_K):
        acc = acc + y_ref[k * tc:(k + 1) * tc, :] * wt[:, k:k + 1]
    h = h_ref[...].astype(BF16)
    act = _silu(_mm(h, wsg_ref[...])) * _mm(h, wsu_ref[...])
    y = acc + _mm(act.astype(BF16), wsd_ref[...])
    out = x_ref[...] + mod_ref[0, 5:6, :] * _rms(y, g_ref[...])

    @pl.when(i < n_ctx)
    def _():
        oc_ref[...] = out

    @pl.when(i >= n_ctx)
    def _():
        ol_ref[...] = out


def _combine(y_pairs, wt_tok, h, x1, mods, g_post, ws_g, ws_u, ws_d):
    tc = T_COMBINE
    n = M_TOK // tc
    n_ctx = N_CTX_TOK // tc
    row = lambda i: (i, 0)
    const = lambda i: (0, 0)
    return pl.pallas_call(
        functools.partial(_combine_kernel, n_ctx=n_ctx),
        out_shape=(jax.ShapeDtypeStruct((N_CTX_TOK, D_MODEL), F32),
                   jax.ShapeDtypeStruct((M_TOK - N_CTX_TOK, D_MODEL), F32)),
        grid=(n,),
        in_specs=[pl.BlockSpec((TOP_K * tc, D_MODEL), row),
                  pl.BlockSpec((tc, LANES), row),
                  pl.BlockSpec((tc, D_MODEL), row),
                  pl.BlockSpec((tc, D_MODEL), row),
                  pl.BlockSpec((1, SUBLANES, D_MODEL), lambda i: (i * tc // ROWS_PER_MOD, 0, 0)),
                  pl.BlockSpec((1, D_MODEL), const),
                  pl.BlockSpec((D_MODEL, EXPERT_FF), const),
                  pl.BlockSpec((D_MODEL, EXPERT_FF), const),
                  pl.BlockSpec((EXPERT_FF, D_MODEL), const)],
        out_specs=(pl.BlockSpec((tc, D_MODEL), lambda i: (jnp.minimum(i, n_ctx - 1), 0)),
                   pl.BlockSpec((tc, D_MODEL), lambda i: (jnp.maximum(i - n_ctx, 0), 0))),
        compiler_params=_params(("arbitrary",)),
        name="moe_combine",
    )(y_pairs, wt_tok, h, x1, mods, g_post, ws_g, ws_u, ws_d)


def _grid_pos_embed(n_tokens):
    rows = n_tokens // GRID_W
    half = D_MODEL // 2
    quarter = half // 2
    omega = 1.0 / (10000.0 ** (jnp.arange(quarter, dtype=F32) / quarter))

    def axis_embed(pos):
        ang = pos.astype(F32)[:, None] * omega
        return jnp.concatenate([jnp.sin(ang), jnp.cos(ang)], axis=-1)

    e_row = axis_embed(jnp.arange(rows))
    e_col = axis_embed(jnp.arange(GRID_W))
    emb = jnp.concatenate([jnp.broadcast_to(e_row[:, None], (rows, GRID_W, half)),
                           jnp.broadcast_to(e_col[None], (rows, GRID_W, half))], axis=-1)
    return emb.reshape(rows * GRID_W, D_MODEL)


def _lane_row(v, lane0):
    return jnp.zeros((1, LANES), F32).at[0, lane0:lane0 + v.shape[0]].set(v)


def kernel(x_prompt, x_sample, state_gla, state_ssd, c, c_ctx, w_ada, b_ada, g_mix_pre, g_mix_post, w_in, w_gk_up, b_gk, g_gla_norm, conv_w, conv_b, dt_bias, a_log, d_skip, g_ssd_norm, w_out, g_ffn_pre, g_ffn_post, w_router, b_router, w_exp_gate, w_exp_up, w_exp_down, w_sh_gate, w_sh_up, w_sh_down):
    assert x_prompt.shape == (N_CTX_SEQ, CTX_LEN, D_MODEL) and x_sample.shape == (N_LAT_SEQ, LAT_LEN, D_MODEL)
    assert w_ada.shape[0] == 1, "single layer"
    l = 0

    w_in_l = w_in[l]
    o_q = 0
    o_k = o_q + GLA_KEY_WIDTH
    o_v = o_k + GLA_KEY_WIDTH
    o_gout = o_v + GLA_WIDTH
    o_lr = o_gout + GLA_WIDTH
    o_z = o_lr + N_DIR * GLA_GATE_RANK
    o_xbc = o_z + SSD_WIDTH
    o_dt = o_xbc + SSD_CONV_CH
    o_end = o_dt + N_DIR * SSD_HEADS
    assert o_end == w_in_l.shape[1]
    w_main = jnp.concatenate([w_in_l[:, o_gout:o_lr], w_in_l[:, o_z:o_xbc], w_in_l[:, o_v:o_gout],
                              w_in_l[:, o_xbc:o_dt], w_in_l[:, o_q:o_k], w_in_l[:, o_k:o_v]],
                             axis=1).astype(BF16)
    w_small = jnp.concatenate([w_in_l[:, o_lr:o_z], w_in_l[:, o_dt:o_end],
                               jnp.zeros((D_MODEL, LANES - SM_DT - N_DIR * SSD_HEADS), F32)], axis=1)
    w_out_bf = w_out[l].astype(BF16)
    wr_t = w_router[l].T
    wr_hi = wr_t.astype(BF16)
    wr_lo = (wr_t - wr_hi.astype(F32)).astype(BF16)
    conv_w8 = jnp.zeros((SUBLANES, SSD_CONV_CH), F32).at[:conv_w.shape[1]].set(conv_w[l])
    d_skip_x = jnp.repeat(d_skip[l], SSD_HEAD_DIM)[None, :]
    g_gla = g_gla_norm[l][None, :]
    g_ssd = g_ssd_norm[l][None, :]

    cvecs = jnp.zeros((SUBLANES, D_MODEL), F32).at[0].set(c_ctx).at[1:1 + N_LAT_SEQ].set(c)
    mod_flat = _ada_mod(cvecs, w_ada[l], b_ada[l][None, :])
    mods = jnp.zeros((N_MOD, SUBLANES, D_MODEL), F32).at[:, :6].set(
        mod_flat[:N_MOD].reshape(N_MOD, 6, D_MODEL))

    x_ctx = x_prompt.reshape(N_CTX_TOK, D_MODEL)
    x_lat = x_sample.reshape(N_LAT_SEQ * LAT_LEN, D_MODEL)
    pos = _grid_pos_embed(LAT_LEN)

    proj_main, proj_small = _in_proj(x_ctx, x_lat, pos, mods, g_mix_pre[l][None, :], w_main, w_small)

    o_dir, s_gla_dir, y_dir, s_ssd_dir = [], [], [], []
    for d in range(N_DIR):
        rev = d == 1
        wgk_pad = jnp.zeros((LANES, GLA_KEY_WIDTH), F32).at[
            SM_LR + d * GLA_GATE_RANK:SM_LR + (d + 1) * GLA_GATE_RANK].set(w_gk_up[l, d])
        o_d, s_d = _gla_scan(proj_main, proj_small, wgk_pad, b_gk[l, d][None, :],
                             state_gla[:, l, d], rev)
        o_dir.append(o_d)
        s_gla_dir.append(s_d)
        lane0 = SM_DT + d * SSD_HEADS
        dtb_r = _lane_row(dt_bias[l, d], lane0)
        nea_r = _lane_row(-jnp.exp(a_log[l, d]), lane0)
        s0_ssd = state_ssd[:, l, d].reshape(N_LAT_SEQ, SSD_GROUPS, SSD_GROUP_WIDTH, SSD_STATE)
        y_d, t_d = _ssd_scan(proj_main, proj_small, conv_w8, conv_b[l][None, :],
                             dtb_r, nea_r, dtb_r.T, nea_r.T, d_skip_x, s0_ssd, rev, d)
        y_dir.append(y_d)
        s_ssd_dir.append(t_d.reshape(N_CTX_SEQ, SSD_HEADS, SSD_HEAD_DIM, SSD_STATE))
    new_state_gla = jnp.stack(s_gla_dir, axis=1)[:, None]
    new_state_ssd = jnp.stack(s_ssd_dir, axis=1)[:, None]

    x1 = _out_proj(o_dir[0], o_dir[1], y_dir[0], y_dir[1], proj_main, x_ctx, x_lat, pos, mods,
                   g_gla, g_ssd, g_mix_post[l][None, :], w_out_bf)

    h_ffn, idx, wt_tok, rank, cnt = _router(x1, mods, g_ffn_pre[l][None, :], wr_hi, wr_lo,
                                               b_router[l][:, None])
    i32 = jnp.int32
    e_ids = jnp.arange(N_EXPERTS, dtype=i32)
    counts = cnt[:, 0].astype(i32)
    padded = (counts + MOE_BLK - 1) // MOE_BLK * MOE_BLK
    pad_end = jnp.sum(jnp.where(e_ids[None, :] <= e_ids[:, None], padded[None, :], 0), axis=1)
    pad_start = pad_end - padded
    slot0 = jnp.sum(jnp.where(idx[:, :, None] == e_ids, pad_start, 0), axis=-1)
    dest = slot0 + rank
    n_blk = pad_end[-1] // MOE_BLK
    blk_ids = jnp.arange(N_MOE_BLOCKS, dtype=i32)
    blk_src = jnp.minimum(blk_ids, n_blk - 1)
    blk_expert = jnp.minimum(jnp.sum((pad_end[None, :] <= (blk_src * MOE_BLK)[:, None]).astype(i32), axis=1),
                             N_EXPERTS - 1)
    blk_hot = blk_expert[:, None] == e_ids
    nonempty = counts > 0
    ordinal = jnp.sum(jnp.where((e_ids[None, :] < e_ids[:, None]) & nonempty[None, :], 1, 0), axis=1)
    next_e = jnp.min(jnp.where((e_ids[None, :] > e_ids[:, None]) & nonempty[None, :], e_ids[None, :],
                               N_EXPERTS), axis=1)
    next_e = jnp.where(next_e == N_EXPERTS, -1, next_e)
    blk_first = jnp.concatenate([jnp.ones((1,), i32), (blk_expert[1:] != blk_expert[:-1]).astype(i32)])
    blk_next = jnp.sum(jnp.where(blk_hot, next_e, 0), axis=1).astype(i32)
    blk_slot = (jnp.sum(jnp.where(blk_hot, ordinal, 0), axis=1) % 2).astype(i32)
    n_blk_arr = n_blk.astype(i32)[None]
    slot_pair = _slot_table((pad_start + counts).astype(i32), (padded - counts).astype(i32),
                            n_blk_arr, dest).reshape(N_MOE_BLOCKS, 1, MOE_BLK)
    is_pad = slot_pair < 0
    real_pair = jnp.maximum(slot_pair, 0)
    src_rows = jnp.where(is_pad, 0, _pair_token(real_pair))
    spare_rows = N_PAIRS + jnp.arange(MOE_BLK, dtype=i32)[None, None, :]
    dst_rows = jnp.where(is_pad, spare_rows, _pair_row(real_pair))
    y_pairs = _experts(blk_expert.astype(i32), n_blk_arr, blk_first, blk_next, blk_slot,
                       src_rows, dst_rows, h_ffn, w_exp_gate[l], w_exp_up[l], w_exp_down[l])
    out_ctx, out_lat = _combine(y_pairs, wt_tok, h_ffn, x1, mods, g_ffn_post[l][None, :],
                                w_sh_gate[l].astype(BF16), w_sh_up[l].astype(BF16),
                                w_sh_down[l].astype(BF16))
    return (out_ctx.reshape(N_CTX_SEQ, CTX_LEN, D_MODEL), out_lat.reshape(N_LAT_SEQ, LAT_LEN, D_MODEL),
            new_state_gla, new_state_ssd)
```

```python
import functools

import numpy as np
import jax
import jax.numpy as jnp
from jax import lax
from jax.experimental import pallas as pl
from jax.experimental.pallas import tpu as pltpu

F32 = jnp.float32
BF16 = jnp.bfloat16

D_MODEL = 2048
N_CTX_SEQ = 16
CTX_LEN = 256
N_LAT_SEQ = 2
LAT_LEN = 4096
GRID_W = 64
EPS = 1e-6
N_CTX_TOK = N_CTX_SEQ * CTX_LEN
M_TOK = N_CTX_TOK + N_LAT_SEQ * LAT_LEN
SEQ_LENS = (CTX_LEN,) * N_CTX_SEQ + (LAT_LEN,) * N_LAT_SEQ
ROWS_PER_MOD = 4096
N_MOD = M_TOK // ROWS_PER_MOD

GLA_HEADS = 4
GLA_DK = 128
GLA_DV = 256
GLA_KEY_WIDTH = GLA_HEADS * GLA_DK
GLA_WIDTH = GLA_HEADS * GLA_DV
GLA_GATE_RANK = 16
GLA_GATE_TAU = 16.0
GLA_CHUNK = 64
GLA_ROWS = 256

SSD_HEADS = 16
SSD_HEAD_DIM = 64
SSD_GROUPS = 2
SSD_HPG = SSD_HEADS // SSD_GROUPS
SSD_STATE = 128
SSD_WIDTH = SSD_HEADS * SSD_HEAD_DIM
SSD_GROUP_WIDTH = SSD_WIDTH // SSD_GROUPS
SSD_CHUNK = 128
SSD_CONV_CH = SSD_WIDTH + 2 * SSD_GROUPS * SSD_STATE
N_DIR = 2

N_EXPERTS = 256
TOP_K = 8
N_EXPERT_GROUPS = 8
GROUP_SIZE = N_EXPERTS // N_EXPERT_GROUPS
TOPK_GROUPS = 4
EXPERT_FF = 512
ROUTED_SCALE = 2.5

C_GOUT = 0
C_Z = C_GOUT + GLA_WIDTH
C_V = C_Z + SSD_WIDTH
C_XBC = C_V + GLA_WIDTH
C_Q = C_XBC + SSD_CONV_CH
C_K = C_Q + GLA_KEY_WIDTH
MAIN_WIDTH = C_K + GLA_KEY_WIDTH
LANES = 128
SUBLANES = 8
SUBLANES_BF16 = 2 * SUBLANES
SM_LR = 0
SM_DT = N_DIR * GLA_GATE_RANK

TM_PROJ = 1024
TN_PROJ = 512
TM_OUT = 256
TM_ROUTE = 256
MOE_BLK = 128
N_PAIRS = M_TOK * TOP_K
N_MOE_BLOCKS = N_PAIRS // MOE_BLK + N_EXPERTS
N_SLOTS = N_MOE_BLOCKS * MOE_BLK
T_DISPATCH = 512
T_COMBINE = 128
ADA_TN = 1024
VMEM_LIMIT = 52 * 1024 * 1024


def _mm(a, b):
    return jnp.dot(a, b, preferred_element_type=F32)


def _mm_nt(a, b):
    return lax.dot_general(a, b, (((1,), (1,)), ((), ())), preferred_element_type=F32)


def _mm_tn(a, b):
    return lax.dot_general(a, b, (((0,), (0,)), ((), ())), preferred_element_type=F32)


def _split2(x):
    hi = x.astype(BF16)
    lo = (x - hi.astype(F32)).astype(BF16)
    return hi, lo


def _split3(x):
    hi = x.astype(BF16)
    r = x - hi.astype(F32)
    mid = r.astype(BF16)
    lo = (r - mid.astype(F32)).astype(BF16)
    return hi, mid, lo


def _mm_x3(a, b):
    a_hi, a_lo = _split2(a)
    b_hi, b_lo = _split2(b)
    return _mm(a_hi, b_hi) + _mm(a_lo, b_hi) + _mm(a_hi, b_lo)


def _mm_sel_left(sel_bf, x):
    hi, mid, lo = _split3(x)
    return _mm(sel_bf, hi) + _mm(sel_bf, mid) + _mm(sel_bf, lo)


def _mm_sel_right(x, sel_bf):
    hi, mid, lo = _split3(x)
    return _mm(hi, sel_bf) + _mm(mid, sel_bf) + _mm(lo, sel_bf)


def _sigmoid(x):
    return 1.0 / (1.0 + jnp.exp(-x))


def _silu(x):
    return x * _sigmoid(x)


def _softplus(x):
    return jnp.maximum(x, 0.0) + jnp.log1p(jnp.exp(-jnp.abs(x)))


def _rms(x, g):
    return x * lax.rsqrt(jnp.mean(x * x, axis=-1, keepdims=True) + EPS) * g


def _params(sem, vmem=VMEM_LIMIT):
    return pltpu.CompilerParams(dimension_semantics=sem, vmem_limit_bytes=vmem)


def _ada_kernel(c_ref, w_ref, b_ref, o_ref):
    o_ref[...] = _mm_x3(_silu(c_ref[...]), w_ref[...]) + b_ref[...]


def _ada_mod(cvecs, w_ada, b_ada):
    n_out = w_ada.shape[1]
    return pl.pallas_call(
        _ada_kernel,
        out_shape=jax.ShapeDtypeStruct((SUBLANES, n_out), F32),
        grid=(n_out // ADA_TN,),
        in_specs=[pl.BlockSpec((SUBLANES, D_MODEL), lambda j: (0, 0)),
                  pl.BlockSpec((D_MODEL, ADA_TN), lambda j: (0, j)),
                  pl.BlockSpec((1, ADA_TN), lambda j: (0, j))],
        out_specs=pl.BlockSpec((SUBLANES, ADA_TN), lambda j: (0, j)),
        compiler_params=_params(("arbitrary",)),
        name="ada_mod",
    )(cvecs, w_ada, b_ada)


def _token_specs(tm, buffers=2):
    n_ctx = N_CTX_TOK // tm
    n_pos = LAT_LEN // tm
    mode = dict(pipeline_mode=pl.Buffered(buffers)) if buffers != 2 else {}
    ctx = pl.BlockSpec((tm, D_MODEL), lambda i, *_: (jnp.minimum(i, n_ctx - 1), 0), **mode)
    lat = pl.BlockSpec((tm, D_MODEL), lambda i, *_: (jnp.maximum(i - n_ctx, 0), 0), **mode)
    pos = pl.BlockSpec((tm, D_MODEL), lambda i, *_: (jnp.maximum(i - n_ctx, 0) % n_pos, 0), **mode)
    return n_ctx, [ctx, lat, pos]


def _inproj_kernel(xc_ref, xl_ref, pos_ref, mod_ref, g_ref, w_ref, ws_ref, o_ref, os_ref, h_scr, *, n_ctx):
    def prologue(x):
        h = _rms(x, g_ref[...]) * (1.0 + mod_ref[0, 1:2, :]) + mod_ref[0, 0:1, :]
        h_hi, h_lo = _split2(h)
        h_scr[...] = h_hi
        ws_hi, ws_lo = _split2(ws_ref[...])
        os_ref[...] = _mm(h_hi, ws_hi) + _mm(h_lo, ws_hi) + _mm(h_hi, ws_lo)

    first = pl.program_id(1) == 0
    is_ctx = pl.program_id(0) < n_ctx

    @pl.when(first & is_ctx)
    def _():
        prologue(xc_ref[...])

    @pl.when(first & jnp.logical_not(is_ctx))
    def _():
        prologue(xl_ref[...] + pos_ref[...])

    o_ref[...] = _mm(h_scr[...], w_ref[...]).astype(BF16)


def _in_proj(x_ctx, x_lat, pos, mods, g_pre, w_main, w_small):
    tm, tn = TM_PROJ, TN_PROJ
    n_ctx, tok_specs = _token_specs(tm, buffers=1)
    return pl.pallas_call(
        functools.partial(_inproj_kernel, n_ctx=n_ctx),
        out_shape=(jax.ShapeDtypeStruct((M_TOK, MAIN_WIDTH), BF16),
                   jax.ShapeDtypeStruct((M_TOK, LANES), F32)),
        grid=(M_TOK // tm, MAIN_WIDTH // tn),
        in_specs=tok_specs + [
                  pl.BlockSpec((1, SUBLANES, D_MODEL), lambda i, j: (i * tm // ROWS_PER_MOD, 0, 0)),
                  pl.BlockSpec((1, D_MODEL), lambda i, j: (0, 0)),
                  pl.BlockSpec((D_MODEL, tn), lambda i, j: (0, j)),
                  pl.BlockSpec((D_MODEL, LANES), lambda i, j: (0, 0))],
        out_specs=(pl.BlockSpec((tm, tn), lambda i, j: (i, j)),
                   pl.BlockSpec((tm, LANES), lambda i, j: (i, 0))),
        scratch_shapes=[pltpu.VMEM((tm, D_MODEL), BF16)],
        compiler_params=_params(("arbitrary", "arbitrary")),
        name="in_proj",
    )(x_ctx, x_lat, pos, mods, g_pre, w_main, w_small)


def _scan_schedule(rows_per_step, reverse):
    blk, flag, s0i, soi, emit, has_prev, has_next = [], [], [], [], [], [], []
    start = 0
    for s, length in enumerate(SEQ_LENS):
        nb = length // rows_per_step
        is_ctx = s < N_CTX_SEQ
        order = range(nb - 1, -1, -1) if reverse else range(nb)
        for n, b in enumerate(order):
            blk.append(start + b)
            flag.append((1 if is_ctx else 2) if n == 0 else 0)
            s0i.append(0 if is_ctx else s - N_CTX_SEQ)
            soi.append(s if is_ctx else N_CTX_SEQ - 1)
            emit.append(1 if (is_ctx and n == nb - 1) else 0)
            has_prev.append(1 if b > 0 else 0)
            has_next.append(1 if b < nb - 1 else 0)
        start += nb
    return tuple(jnp.asarray(np.array(a, np.int32)) for a in (blk, flag, s0i, soi, emit, has_prev, has_next))


def _gla_kernel(blk_ref, flag_ref, s0i_ref, soi_ref, emit_ref,
                q_ref, k_ref, v_ref, sm_ref, wgk_ref, bgk_ref, s0_ref,
                o_ref, so_ref, st_scr, *, reverse):
    i = pl.program_id(0)
    flag = flag_ref[i]

    @pl.when(flag == 1)
    def _():
        st_scr[...] = jnp.zeros_like(st_scr)

    @pl.when(flag == 2)
    def _():
        for h in range(GLA_HEADS):
            st_scr[h] = s0_ref[0, h].T

    c = GLA_CHUNK
    r_id = lax.broadcasted_iota(jnp.int32, (c, c), 0)
    c_id = lax.broadcasted_iota(jnp.int32, (c, c), 1)
    tri = (c_id >= r_id) if reverse else (c_id <= r_id)
    tri_bf = jnp.where(tri, 1.0, 0.0).astype(BF16)

    gk = _mm_x3(sm_ref[...], wgk_ref[...]) + bgk_ref[...]
    log_a = (jnp.minimum(gk, 0.0) - jnp.log1p(jnp.exp(-jnp.abs(gk)))) * (1.0 / GLA_GATE_TAU)

    n_chunks = GLA_ROWS // c
    for ci in (range(n_chunks - 1, -1, -1) if reverse else range(n_chunks)):
        lo = ci * c
        b_all = _mm_sel_left(tri_bf, log_a[lo:lo + c])
        for h in range(GLA_HEADS):
            kc = slice(h * GLA_DK, (h + 1) * GLA_DK)
            vc = slice(h * GLA_DV, (h + 1) * GLA_DV)
            b = b_all[:, kc]
            b_end = b[0:1] if reverse else b[c - 1:c]
            q = q_ref[lo:lo + c, kc].astype(F32) * (GLA_DK ** -0.5)
            k = k_ref[lo:lo + c, kc].astype(F32)
            v = v_ref[lo:lo + c, vc]
            q_e = (q * jnp.exp(b)).astype(BF16)
            k_e = (k * jnp.exp(-b)).astype(BF16)
            att = jnp.where(tri, _mm_nt(q_e, k_e), 0.0).astype(BF16)
            st = st_scr[h]
            o_ref[lo:lo + c, vc] = (_mm(att, v) + _mm_nt(q_e, st.astype(BF16))).astype(BF16)
            k_end = (k * jnp.exp(b_end - b)).astype(BF16)
            st_scr[h] = st * jnp.exp(b_end) + _mm_tn(v, k_end)

    @pl.when(emit_ref[i] == 1)
    def _():
        for h in range(GLA_HEADS):
            so_ref[0, h] = st_scr[h].T


def _gla_scan(proj_main, proj_small, wgk_pad, bgk, s0, reverse):
    sched = _scan_schedule(GLA_ROWS, reverse)[:5]
    n_steps = M_TOK // GLA_ROWS
    t = GLA_ROWS
    q_blk, k_blk, v_blk = C_Q // GLA_KEY_WIDTH, C_K // GLA_KEY_WIDTH, C_V // GLA_WIDTH
    state_blk = (1, GLA_HEADS, GLA_DK, GLA_DV)
    return pl.pallas_call(
        functools.partial(_gla_kernel, reverse=reverse),
        out_shape=(jax.ShapeDtypeStruct((M_TOK, GLA_WIDTH), BF16),
                   jax.ShapeDtypeStruct((N_CTX_SEQ,) + state_blk[1:], F32)),
        grid_spec=pltpu.PrefetchScalarGridSpec(
            num_scalar_prefetch=5,
            grid=(n_steps,),
            in_specs=[
                pl.BlockSpec((t, GLA_KEY_WIDTH), lambda i, blk, *_: (blk[i], q_blk)),
                pl.BlockSpec((t, GLA_KEY_WIDTH), lambda i, blk, *_: (blk[i], k_blk)),
                pl.BlockSpec((t, GLA_WIDTH), lambda i, blk, *_: (blk[i], v_blk)),
                pl.BlockSpec((t, LANES), lambda i, blk, *_: (blk[i], 0)),
                pl.BlockSpec((LANES, GLA_KEY_WIDTH), lambda i, *_: (0, 0)),
                pl.BlockSpec((1, GLA_KEY_WIDTH), lambda i, *_: (0, 0)),
                pl.BlockSpec(state_blk, lambda i, blk, flag, s0i, *_: (s0i[i], 0, 0, 0)),
            ],
            out_specs=(
                pl.BlockSpec((t, GLA_WIDTH), lambda i, blk, *_: (blk[i], 0)),
                pl.BlockSpec(state_blk, lambda i, blk, flag, s0i, soi, *_: (soi[i], 0, 0, 0)),
            ),
            scratch_shapes=[pltpu.VMEM((GLA_HEADS, GLA_DV, GLA_DK), F32)]),
        compiler_params=_params(("arbitrary",)),
        name="gla_bwd" if reverse else "gla_fwd",
    )(*sched, proj_main, proj_main, proj_main, proj_small, wgk_pad, bgk, s0)


def _ssd_kernel(blk_ref, flag_ref, s0i_ref, soi_ref, emit_ref, hp_ref, hn_ref,
                xbc_ref, xprev_ref, xnext_ref, sm_ref, cw_ref, cb_ref,
                dtb_r_ref, nea_r_ref, dtb_c_ref, nea_c_ref, dsk_ref, s0_ref,
                y_ref, so_ref, st_scr, *, reverse, lane0, add_skip):
    i = pl.program_id(0)
    flag = flag_ref[i]
    t = SSD_CHUNK

    @pl.when(flag == 1)
    def _():
        st_scr[...] = jnp.zeros_like(st_scr)

    @pl.when(flag == 2)
    def _():
        for g in range(SSD_GROUPS):
            st_scr[g] = s0_ref[0, g].T

    xbc = xbc_ref[...].astype(F32)
    last = SUBLANES_BF16 - 1
    prev = jnp.where(hp_ref[i] == 1, xprev_ref[last:last + 1, :].astype(F32), 0.0)
    nxt = jnp.where(hn_ref[i] == 1, xnext_ref[0:1, :].astype(F32), 0.0)
    row = lax.broadcasted_iota(jnp.int32, xbc.shape, 0)
    x_m1 = jnp.where(row == 0, prev, pltpu.roll(xbc, 1, 0))
    x_p1 = jnp.where(row == t - 1, nxt, pltpu.roll(xbc, t - 1, 0))
    act = _silu(x_m1 * cw_ref[0:1, :] + xbc * cw_ref[1:2, :] + x_p1 * cw_ref[2:3, :] + cb_ref[...])
    xs = act[:, :SSD_WIDTH]
    bm = act[:, SSD_WIDTH:SSD_WIDTH + SSD_GROUPS * SSD_STATE]
    cm = act[:, SSD_WIDTH + SSD_GROUPS * SSD_STATE:]

    sm = sm_ref[...]
    dt = _softplus(sm + dtb_r_ref[...])
    a = dt * nea_r_ref[...]
    a_t = _softplus(sm.T + dtb_c_ref[...]) * nea_c_ref[...]

    r_id = lax.broadcasted_iota(jnp.int32, (t, t), 0)
    c_id = lax.broadcasted_iota(jnp.int32, (t, t), 1)
    tri = (c_id >= r_id) if reverse else (c_id <= r_id)
    tri_bf = jnp.where(tri, 1.0, 0.0).astype(BF16)
    tri_t_bf = jnp.where((r_id >= c_id) if reverse else (r_id <= c_id), 1.0, 0.0).astype(BF16)
    cs = _mm_sel_left(tri_bf, a)
    cs_t = _mm_sel_right(a_t, tri_t_bf)

    e_r = lax.broadcasted_iota(jnp.int32, (LANES, SSD_WIDTH), 0)
    e_c = lax.broadcasted_iota(jnp.int32, (LANES, SSD_WIDTH), 1)
    expand = jnp.where(e_r - lane0 == e_c // SSD_HEAD_DIM, 1.0, 0.0).astype(BF16)
    dt_x = _mm_sel_right(dt, expand)
    cs_x = _mm_sel_right(cs, expand)
    cs_end_x = cs_x[0:1] if reverse else cs_x[t - 1:t]
    x_in = xs * dt_x
    x_bf = x_in.astype(BF16)
    x_w = (x_in * jnp.exp(cs_end_x - cs_x)).astype(BF16)
    decay_in = jnp.exp(cs_x)
    decay_end = jnp.exp(cs_end_x)

    for g in range(SSD_GROUPS):
        gc = slice(g * SSD_STATE, (g + 1) * SSD_STATE)
        gw = slice(g * SSD_GROUP_WIDTH, (g + 1) * SSD_GROUP_WIDTH)
        c_g = cm[:, gc].astype(BF16)
        b_g = bm[:, gc].astype(BF16)
        cb = _mm_nt(c_g, b_g)
        st = st_scr[g]
        y_state = _mm(c_g, st.astype(BF16)) * decay_in[:, gw]
        for r in range(SSD_HPG):
            hh = g * SSD_HPG + r
            hc = slice(hh * SSD_HEAD_DIM, (hh + 1) * SSD_HEAD_DIM)
            lane = lane0 + hh
            seg = cs[:, lane:lane + 1] - cs_t[lane:lane + 1, :]
            lm = jnp.exp(jnp.where(tri, seg, -jnp.inf))
            y_h = _mm((cb * lm).astype(BF16), x_bf[:, hc]) + y_state[:, r * SSD_HEAD_DIM:(r + 1) * SSD_HEAD_DIM]
            if add_skip:
                y_h = y_h + xs[:, hc] * dsk_ref[:, hc]
            y_ref[:, hc] = y_h.astype(BF16)
        st_scr[g] = st * decay_end[:, gw] + _mm_tn(b_g, x_w[:, gw])

    @pl.when(emit_ref[i] == 1)
    def _():
        for g in range(SSD_GROUPS):
            so_ref[0, g] = st_scr[g].T


def _ssd_scan(proj_main, proj_small, conv_w, conv_b, dtb_r, nea_r, dtb_c, nea_c, d_skip_x, s0,
              reverse, direction):
    sched = _scan_schedule(SSD_CHUNK, reverse)
    t = SSD_CHUNK
    n_steps = M_TOK // t
    xbc_blk = C_XBC // SSD_CONV_CH
    halo = SUBLANES_BF16
    rb = t // halo
    n_rb = M_TOK // halo
    state_blk = (1, SSD_GROUPS, SSD_GROUP_WIDTH, SSD_STATE)
    return pl.pallas_call(
        functools.partial(_ssd_kernel, reverse=reverse, lane0=SM_DT + direction * SSD_HEADS,
                          add_skip=not reverse),
        out_shape=(jax.ShapeDtypeStruct((M_TOK, SSD_WIDTH), BF16),
                   jax.ShapeDtypeStruct((N_CTX_SEQ,) + state_blk[1:], F32)),
        grid_spec=pltpu.PrefetchScalarGridSpec(
            num_scalar_prefetch=7,
            grid=(n_steps,),
            in_specs=[
                pl.BlockSpec((t, SSD_CONV_CH), lambda i, blk, *_: (blk[i], xbc_blk)),
                pl.BlockSpec((halo, SSD_CONV_CH),
                             lambda i, blk, *_: (jnp.maximum(blk[i] * rb - 1, 0), xbc_blk)),
                pl.BlockSpec((halo, SSD_CONV_CH),
                             lambda i, blk, *_: (jnp.minimum((blk[i] + 1) * rb, n_rb - 1), xbc_blk)),
                pl.BlockSpec((t, LANES), lambda i, blk, *_: (blk[i], 0)),
                pl.BlockSpec((SUBLANES, SSD_CONV_CH), lambda i, *_: (0, 0)),
                pl.BlockSpec((1, SSD_CONV_CH), lambda i, *_: (0, 0)),
                pl.BlockSpec((1, LANES), lambda i, *_: (0, 0)),
                pl.BlockSpec((1, LANES), lambda i, *_: (0, 0)),
                pl.BlockSpec((LANES, 1), lambda i, *_: (0, 0)),
                pl.BlockSpec((LANES, 1), lambda i, *_: (0, 0)),
                pl.BlockSpec((1, SSD_WIDTH), lambda i, *_: (0, 0)),
                pl.BlockSpec(state_blk, lambda i, blk, flag, s0i, *_: (s0i[i], 0, 0, 0)),
            ],
            out_specs=(
                pl.BlockSpec((t, SSD_WIDTH), lambda i, blk, *_: (blk[i], 0)),
                pl.BlockSpec(state_blk, lambda i, blk, flag, s0i, soi, *_: (soi[i], 0, 0, 0)),
            ),
            scratch_shapes=[pltpu.VMEM((SSD_GROUPS, SSD_STATE, SSD_GROUP_WIDTH), F32)]),
        compiler_params=_params(("arbitrary",)),
        name="ssd_bwd" if reverse else "ssd_fwd",
    )(*sched, proj_main, proj_main, proj_main, proj_small, conv_w, conv_b,
      dtb_r, nea_r, dtb_c, nea_c, d_skip_x, s0)


def _outproj_kernel(of_ref, ob_ref, gout_ref, yf_ref, yb_ref, z_ref, xc_ref, xl_ref, pos_ref, mod_ref,
                    ggla_ref, gssd_ref, gpost_ref, w_ref, o_ref, *, n_ctx):
    o = of_ref[...].astype(F32) + ob_ref[...].astype(F32)
    gate = _silu(gout_ref[...].astype(F32))
    parts = []
    for h in range(GLA_HEADS):
        hc = slice(h * GLA_DV, (h + 1) * GLA_DV)
        parts.append((_rms(o[:, hc], ggla_ref[...]) * gate[:, hc]).astype(BF16))
    y = (yf_ref[...].astype(F32) + yb_ref[...].astype(F32)) * _silu(z_ref[...].astype(F32))
    for g in range(SSD_GROUPS):
        gw = slice(g * SSD_GROUP_WIDTH, (g + 1) * SSD_GROUP_WIDTH)
        parts.append(_rms(y[:, gw], gssd_ref[:, gw]).astype(BF16))
    acc = None
    col = 0
    for p in parts:
        term = _mm(p, w_ref[col:col + p.shape[1], :])
        acc = term if acc is None else acc + term
        col += p.shape[1]
    delta = mod_ref[0, 2:3, :] * _rms(acc, gpost_ref[...])
    is_ctx = pl.program_id(0) < n_ctx

    @pl.when(is_ctx)
    def _():
        o_ref[...] = xc_ref[...] + delta

    @pl.when(jnp.logical_not(is_ctx))
    def _():
        o_ref[...] = xl_ref[...] + pos_ref[...] + delta


def _out_proj(o_f, o_b, y_f, y_b, proj_main, x_ctx, x_lat, pos, mods, g_gla, g_ssd, g_post, w_out):
    tm = TM_OUT
    row = lambda i: (i, 0)
    const = lambda i: (0, 0)
    n_ctx, tok_specs = _token_specs(tm)
    return pl.pallas_call(
        functools.partial(_outproj_kernel, n_ctx=n_ctx),
        out_shape=jax.ShapeDtypeStruct((M_TOK, D_MODEL), F32),
        grid=(M_TOK // tm,),
        in_specs=[pl.BlockSpec((tm, GLA_WIDTH), row),
                  pl.BlockSpec((tm, GLA_WIDTH), row),
                  pl.BlockSpec((tm, GLA_WIDTH), lambda i: (i, C_GOUT // GLA_WIDTH)),
                  pl.BlockSpec((tm, SSD_WIDTH), row),
                  pl.BlockSpec((tm, SSD_WIDTH), row),
                  pl.BlockSpec((tm, SSD_WIDTH), lambda i: (i, C_Z // SSD_WIDTH))] + tok_specs + [
                  pl.BlockSpec((1, SUBLANES, D_MODEL), lambda i: (i * tm // ROWS_PER_MOD, 0, 0)),
                  pl.BlockSpec((1, GLA_DV), const),
                  pl.BlockSpec((1, SSD_WIDTH), const),
                  pl.BlockSpec((1, D_MODEL), const),
                  pl.BlockSpec((D_MODEL, D_MODEL), const)],
        out_specs=pl.BlockSpec((tm, D_MODEL), row),
        compiler_params=_params(("arbitrary",)),
        name="out_proj",
    )(o_f, o_b, proj_main, y_f, y_b, proj_main, x_ctx, x_lat, pos, mods, g_gla, g_ssd, g_post, w_out)


def _router_kernel(x_ref, mod_ref, g_ref, wr_hi_ref, wr_lo_ref, br_ref,
                   h_ref, idx_ref, wt_ref, rank_ref, cnt_ref, cnt_scr):
    i = pl.program_id(0)
    tm = TM_ROUTE

    @pl.when(i == 0)
    def _():
        cnt_scr[...] = jnp.zeros_like(cnt_scr)

    h = _rms(x_ref[...], g_ref[...]) * (1.0 + mod_ref[0, 4:5, :]) + mod_ref[0, 3:4, :]
    h_ref[...] = h
    h_hi, h_lo = _split2(h)
    wr_hi = wr_hi_ref[...]
    logits = _mm_nt(wr_hi, h_hi) + _mm_nt(wr_hi, h_lo) + _mm_nt(wr_lo_ref[...], h_hi)
    scores = _sigmoid(logits)
    sel = scores + br_ref[...]
    neg = -jnp.inf

    def first_argmax(x, ids, n):
        m = jnp.max(x, axis=0, keepdims=True)
        return m, jnp.min(jnp.where(x == m, ids, float(n)), axis=0, keepdims=True)

    ids_g = lax.broadcasted_iota(jnp.int32, (GROUP_SIZE, tm), 0).astype(F32)
    grp = []
    for g in range(N_EXPERT_GROUPS):
        xg = sel[g * GROUP_SIZE:(g + 1) * GROUP_SIZE]
        m1, a1 = first_argmax(xg, ids_g, GROUP_SIZE)
        m2 = jnp.max(jnp.where(ids_g == a1, neg, xg), axis=0, keepdims=True)
        grp.append(m1 + m2)
    gsc = jnp.concatenate(grp, axis=0)
    ids_8 = lax.broadcasted_iota(jnp.int32, (N_EXPERT_GROUPS, tm), 0).astype(F32)
    keep = jnp.zeros((N_EXPERT_GROUPS, tm), F32)
    for _ in range(TOPK_GROUPS):
        _, a = first_argmax(gsc, ids_8, N_EXPERT_GROUPS)
        pick = ids_8 == a
        keep = jnp.where(pick, 1.0, keep)
        gsc = jnp.where(pick, neg, gsc)
    selm = jnp.concatenate(
        [jnp.where(keep[g:g + 1] > 0.5, sel[g * GROUP_SIZE:(g + 1) * GROUP_SIZE], neg)
         for g in range(N_EXPERT_GROUPS)], axis=0)

    ids_e = lax.broadcasted_iota(jnp.int32, (N_EXPERTS, tm), 0).astype(F32)
    picks, wts = [], []
    chosen = jnp.zeros((N_EXPERTS, tm), F32)
    for _ in range(TOP_K):
        _, a = first_argmax(selm, ids_e, N_EXPERTS)
        hit = ids_e == a
        picks.append(a)
        wts.append(jnp.sum(jnp.where(hit, scores, 0.0), axis=0, keepdims=True))
        chosen = jnp.where(hit, 1.0, chosen)
        selm = jnp.where(hit, neg, selm)
    w = jnp.concatenate(wts, axis=0)
    w = w / jnp.sum(w, axis=0, keepdims=True) * ROUTED_SCALE
    idx_ref[...] = jnp.concatenate(picks, axis=0).astype(jnp.int32)
    wt_ref[...] = jnp.concatenate([w, jnp.zeros((LANES - TOP_K, tm), F32)], axis=0).T

    t_r = lax.broadcasted_iota(jnp.int32, (tm, tm), 0)
    t_c = lax.broadcasted_iota(jnp.int32, (tm, tm), 1)
    before = jnp.where(t_r < t_c, 1.0, 0.0).astype(BF16)
    base = _mm(chosen.astype(BF16), before) + cnt_scr[...]
    rank_ref[...] = jnp.concatenate(
        [jnp.sum(jnp.where(ids_e == a, base, 0.0), axis=0, keepdims=True) for a in picks],
        axis=0).astype(jnp.int32)
    cnt_scr[...] = cnt_scr[...] + jnp.sum(chosen, axis=1, keepdims=True)
    cnt_ref[...] = jnp.broadcast_to(cnt_scr[...], cnt_ref.shape)


def _router(x1, mods, g_pre, wr_hi, wr_lo, b_router):
    tm = TM_ROUTE
    const = lambda i: (0, 0)
    return pl.pallas_call(
        _router_kernel,
        out_shape=(jax.ShapeDtypeStruct((M_TOK, D_MODEL), F32),
                   jax.ShapeDtypeStruct((TOP_K, M_TOK), jnp.int32),
                   jax.ShapeDtypeStruct((M_TOK, LANES), F32),
                   jax.ShapeDtypeStruct((TOP_K, M_TOK), jnp.int32),
                   jax.ShapeDtypeStruct((N_EXPERTS, LANES), F32)),
        grid=(M_TOK // tm,),
        in_specs=[pl.BlockSpec((tm, D_MODEL), lambda i: (i, 0)),
                  pl.BlockSpec((1, SUBLANES, D_MODEL), lambda i: (i * tm // ROWS_PER_MOD, 0, 0)),
                  pl.BlockSpec((1, D_MODEL), const),
                  pl.BlockSpec((N_EXPERTS, D_MODEL), const),
                  pl.BlockSpec((N_EXPERTS, D_MODEL), const),
                  pl.BlockSpec((N_EXPERTS, 1), const)],
        out_specs=(pl.BlockSpec((tm, D_MODEL), lambda i: (i, 0)),
                   pl.BlockSpec((TOP_K, tm), lambda i: (0, i)),
                   pl.BlockSpec((tm, LANES), lambda i: (i, 0)),
                   pl.BlockSpec((TOP_K, tm), lambda i: (0, i)),
                   pl.BlockSpec((N_EXPERTS, LANES), const)),
        scratch_shapes=[pltpu.VMEM((N_EXPERTS, 1), F32)],
        compiler_params=_params(("arbitrary",)),
        name="router",
    )(x1, mods, g_pre, wr_hi, wr_lo, b_router)


def _dispatch_kernel(fill_start_ref, fill_len_ref, nb_ref, dest_ref, h_ref, xs_hbm, zero_scr, sem, zsem):
    @pl.when(pl.program_id(0) == 0)
    def _():
        zero_scr[...] = jnp.zeros_like(zero_scr)

        def for_each_fill(act):
            def pad_body(e, carry):
                start = fill_start_ref[e]
                length = fill_len_ref[e]
                head = jnp.minimum((-start) & (SUBLANES - 1), length)
                for j in range(SUBLANES - 1):
                    @pl.when(j < head)
                    def _():
                        act(pltpu.make_async_copy(zero_scr.at[pl.ds(0, 1)],
                                                  xs_hbm.at[pl.ds(start + j, 1)], zsem))
                body_start = start + head
                body_len = length - head
                for bit in (64, 32, 16, 8):
                    @pl.when((body_len & bit) != 0)
                    def _():
                        off = pl.multiple_of(body_start + (body_len & jnp.int32(~(2 * bit - 1))), SUBLANES)
                        act(pltpu.make_async_copy(zero_scr.at[pl.ds(0, bit)],
                                                  xs_hbm.at[pl.ds(off, bit)], zsem))
                return carry

            def tail_body(b, carry):
                act(pltpu.make_async_copy(zero_scr, xs_hbm.at[pl.ds(b * MOE_BLK, MOE_BLK)], zsem))
                return carry

            lax.fori_loop(0, N_EXPERTS, pad_body, 0)
            lax.fori_loop(nb_ref[0], N_MOE_BLOCKS, tail_body, 0)

        for_each_fill(lambda cp: cp.start())
        for_each_fill(lambda cp: cp.wait())

    def body(t, carry):
        for k in range(TOP_K):
            pltpu.make_async_copy(h_ref.at[pl.ds(t, 1)],
                                  xs_hbm.at[pl.ds(dest_ref[k, t], 1)], sem).start()
        return carry

    lax.fori_loop(0, T_DISPATCH, body, 0)
    pltpu.make_async_copy(xs_hbm.at[pl.ds(0, T_DISPATCH * TOP_K)],
                          xs_hbm.at[pl.ds(0, T_DISPATCH * TOP_K)], sem).wait()


def _dispatch(fill_start, fill_len, n_blk, dest, h):
    return pl.pallas_call(
        _dispatch_kernel,
        out_shape=jax.ShapeDtypeStruct((N_SLOTS, D_MODEL), F32),
        grid_spec=pltpu.PrefetchScalarGridSpec(
            num_scalar_prefetch=3,
            grid=(M_TOK // T_DISPATCH,),
            in_specs=[pl.BlockSpec((TOP_K, T_DISPATCH), lambda i, *_: (0, i), memory_space=pltpu.SMEM),
                      pl.BlockSpec((T_DISPATCH, D_MODEL), lambda i, *_: (i, 0))],
            out_specs=pl.BlockSpec(memory_space=pl.ANY),
            scratch_shapes=[pltpu.VMEM((MOE_BLK, D_MODEL), F32),
                            pltpu.SemaphoreType.DMA, pltpu.SemaphoreType.DMA]),
        compiler_params=_params(("arbitrary",)),
        name="moe_dispatch",
    )(fill_start, fill_len, n_blk, dest, h)


def _expert_weight_copies(e, s, w_hbm, w_f32, sem):
    return [pltpu.make_async_copy(w_hbm[j].at[e], w_f32[j].at[s], sem.at[s, j]) for j in range(3)]


def _expert_kernel(be_ref, bs_ref, nb_ref, first_ref, next_ref, slot_ref,
                   x_ref, wg_hbm, wu_hbm, wd_hbm, y_ref,
                   wg_f32, wu_f32, wd_f32, wg_scr, wu_scr, wd_scr, sem):
    b = pl.program_id(0)
    w_hbm = (wg_hbm, wu_hbm, wd_hbm)
    w_f32 = (wg_f32, wu_f32, wd_f32)

    @pl.when(b == 0)
    def _():
        for cp in _expert_weight_copies(be_ref[0], 0, w_hbm, w_f32, sem):
            cp.start()

    @pl.when(b < nb_ref[0])
    def _():
        @pl.when(first_ref[b] == 1)
        def _():
            s = slot_ref[b]
            for cp in _expert_weight_copies(be_ref[b], s, w_hbm, w_f32, sem):
                cp.wait()

            @pl.when(next_ref[b] >= 0)
            def _():
                for cp in _expert_weight_copies(next_ref[b], 1 - s, w_hbm, w_f32, sem):
                    cp.start()

            wg_scr[...] = wg_f32[s].astype(BF16)
            wu_scr[...] = wu_f32[s].astype(BF16)
            wd_scr[...] = wd_f32[s].astype(BF16)

        x = x_ref[...].astype(BF16)
        act = _silu(_mm(x, wg_scr[...])) * _mm(x, wu_scr[...])
        y_ref[...] = _mm(act.astype(BF16), wd_scr[...])

    @pl.when(b >= nb_ref[0])
    def _():
        y_ref[...] = jnp.zeros_like(y_ref)


def _experts(blk_expert, blk_src, n_blk, blk_first, blk_next, blk_slot, x_sorted, w_g, w_u, w_d):
    hbm = pl.BlockSpec(memory_space=pl.ANY)
    return pl.pallas_call(
        _expert_kernel,
        out_shape=jax.ShapeDtypeStruct((N_SLOTS, D_MODEL), F32),
        grid_spec=pltpu.PrefetchScalarGridSpec(
            num_scalar_prefetch=6,
            grid=(N_MOE_BLOCKS,),
            in_specs=[pl.BlockSpec((MOE_BLK, D_MODEL), lambda b, be, bs, *_: (bs[b], 0)), hbm, hbm, hbm],
            out_specs=pl.BlockSpec((MOE_BLK, D_MODEL), lambda b, *_: (b, 0)),
            scratch_shapes=[pltpu.VMEM((2, D_MODEL, EXPERT_FF), F32),
                            pltpu.VMEM((2, D_MODEL, EXPERT_FF), F32),
                            pltpu.VMEM((2, EXPERT_FF, D_MODEL), F32),
                            pltpu.VMEM((D_MODEL, EXPERT_FF), BF16),
                            pltpu.VMEM((D_MODEL, EXPERT_FF), BF16),
                            pltpu.VMEM((EXPERT_FF, D_MODEL), BF16),
                            pltpu.SemaphoreType.DMA((2, 3))]),
        compiler_params=_params(("arbitrary",)),
        name="moe_experts",
    )(blk_expert, blk_src, n_blk, blk_first, blk_next, blk_slot, x_sorted, w_g, w_u, w_d)


def _combine_copy(y_hbm, buf, sem, slot, k, t, src_row):
    return pltpu.make_async_copy(y_hbm.at[pl.ds(src_row, 1)], buf.at[slot, k, pl.ds(t, 1)], sem.at[slot])


def _combine_kernel(dest_ref, dest_next_ref, y_hbm, wt_ref, h_ref, x_ref, mod_ref, g_ref,
                    wsg_ref, wsu_ref, wsd_ref, oc_ref, ol_ref, buf, sem, *, n_ctx):
    i = pl.program_id(0)
    n = pl.num_programs(0)
    slot = i % 2

    def issue(d_ref, s):
        def body(t, carry):
            for k in range(TOP_K):
                _combine_copy(y_hbm, buf, sem, s, k, t, d_ref[k, t]).start()
            return carry
        lax.fori_loop(0, T_COMBINE, body, 0)

    @pl.when(i == 0)
    def _():
        issue(dest_ref, 0)

    @pl.when(i + 1 < n)
    def _():
        issue(dest_next_ref, 1 - slot)

    pltpu.make_async_copy(buf.at[slot], buf.at[slot], sem.at[slot]).wait()

    wt = wt_ref[...]
    acc = buf[slot, 0] * wt[:, 0:1]
    for k in range(1, TOP_K):
        acc = acc + buf[slot, k] * wt[:, k:k + 1]
    h = h_ref[...].astype(BF16)
    act = _silu(_mm(h, wsg_ref[...])) * _mm(h, wsu_ref[...])
    y = acc + _mm(act.astype(BF16), wsd_ref[...])
    out = x_ref[...] + mod_ref[0, 5:6, :] * _rms(y, g_ref[...])

    @pl.when(i < n_ctx)
    def _():
        oc_ref[...] = out

    @pl.when(i >= n_ctx)
    def _():
        ol_ref[...] = out


def _combine(dest, y_sorted, wt_tok, h, x1, mods, g_post, ws_g, ws_u, ws_d):
    tc = T_COMBINE
    n = M_TOK // tc
    n_ctx = N_CTX_TOK // tc
    row = lambda i: (i, 0)
    const = lambda i: (0, 0)
    return pl.pallas_call(
        functools.partial(_combine_kernel, n_ctx=n_ctx),
        out_shape=(jax.ShapeDtypeStruct((N_CTX_TOK, D_MODEL), F32),
                   jax.ShapeDtypeStruct((M_TOK - N_CTX_TOK, D_MODEL), F32)),
        grid=(n,),
        in_specs=[pl.BlockSpec((TOP_K, tc), lambda i: (0, i), memory_space=pltpu.SMEM),
                  pl.BlockSpec((TOP_K, tc), lambda i: (0, jnp.minimum(i + 1, n - 1)),
                               memory_space=pltpu.SMEM),
                  pl.BlockSpec(memory_space=pl.ANY),
                  pl.BlockSpec((tc, LANES), row),
                  pl.BlockSpec((tc, D_MODEL), row),
                  pl.BlockSpec((tc, D_MODEL), row),
                  pl.BlockSpec((1, SUBLANES, D_MODEL), lambda i: (i * tc // ROWS_PER_MOD, 0, 0)),
                  pl.BlockSpec((1, D_MODEL), const),
                  pl.BlockSpec((D_MODEL, EXPERT_FF), const),
                  pl.BlockSpec((D_MODEL, EXPERT_FF), const),
                  pl.BlockSpec((EXPERT_FF, D_MODEL), const)],
        out_specs=(pl.BlockSpec((tc, D_MODEL), lambda i: (jnp.minimum(i, n_ctx - 1), 0)),
                   pl.BlockSpec((tc, D_MODEL), lambda i: (jnp.maximum(i - n_ctx, 0), 0))),
        scratch_shapes=[pltpu.VMEM((2, TOP_K, tc, D_MODEL), F32),
                        pltpu.SemaphoreType.DMA((2,))],
        compiler_params=_params(("arbitrary",)),
        name="moe_combine",
    )(dest, dest, y_sorted, wt_tok, h, x1, mods, g_post, ws_g, ws_u, ws_d)


def _grid_pos_embed(n_tokens):
    rows = n_tokens // GRID_W
    half = D_MODEL // 2
    quarter = half // 2
    omega = 1.0 / (10000.0 ** (jnp.arange(quarter, dtype=F32) / quarter))

    def axis_embed(pos):
        ang = pos.astype(F32)[:, None] * omega
        return jnp.concatenate([jnp.sin(ang), jnp.cos(ang)], axis=-1)

    e_row = axis_embed(jnp.arange(rows))
    e_col = axis_embed(jnp.arange(GRID_W))
    emb = jnp.concatenate([jnp.broadcast_to(e_row[:, None], (rows, GRID_W, half)),
                           jnp.broadcast_to(e_col[None], (rows, GRID_W, half))], axis=-1)
    return emb.reshape(rows * GRID_W, D_MODEL)


def _lane_row(v, lane0):
    return jnp.zeros((1, LANES), F32).at[0, lane0:lane0 + v.shape[0]].set(v)


def kernel(x_prompt, x_sample, state_gla, state_ssd, c, c_ctx, w_ada, b_ada, g_mix_pre, g_mix_post, w_in, w_gk_up, b_gk, g_gla_norm, conv_w, conv_b, dt_bias, a_log, d_skip, g_ssd_norm, w_out, g_ffn_pre, g_ffn_post, w_router, b_router, w_exp_gate, w_exp_up, w_exp_down, w_sh_gate, w_sh_up, w_sh_down):
    assert x_prompt.shape == (N_CTX_SEQ, CTX_LEN, D_MODEL) and x_sample.shape == (N_LAT_SEQ, LAT_LEN, D_MODEL)
    assert w_ada.shape[0] == 1, "single layer"
    l = 0

    w_in_l = w_in[l]
    o_q = 0
    o_k = o_q + GLA_KEY_WIDTH
    o_v = o_k + GLA_KEY_WIDTH
    o_gout = o_v + GLA_WIDTH
    o_lr = o_gout + GLA_WIDTH
    o_z = o_lr + N_DIR * GLA_GATE_RANK
    o_xbc = o_z + SSD_WIDTH
    o_dt = o_xbc + SSD_CONV_CH
    o_end = o_dt + N_DIR * SSD_HEADS
    assert o_end == w_in_l.shape[1]
    w_main = jnp.concatenate([w_in_l[:, o_gout:o_lr], w_in_l[:, o_z:o_xbc], w_in_l[:, o_v:o_gout],
                              w_in_l[:, o_xbc:o_dt], w_in_l[:, o_q:o_k], w_in_l[:, o_k:o_v]],
                             axis=1).astype(BF16)
    w_small = jnp.concatenate([w_in_l[:, o_lr:o_z], w_in_l[:, o_dt:o_end],
                               jnp.zeros((D_MODEL, LANES - SM_DT - N_DIR * SSD_HEADS), F32)], axis=1)
    w_out_bf = w_out[l].astype(BF16)
    wr_t = w_router[l].T
    wr_hi = wr_t.astype(BF16)
    wr_lo = (wr_t - wr_hi.astype(F32)).astype(BF16)
    conv_w8 = jnp.zeros((SUBLANES, SSD_CONV_CH), F32).at[:conv_w.shape[1]].set(conv_w[l])
    d_skip_x = jnp.repeat(d_skip[l], SSD_HEAD_DIM)[None, :]
    g_gla = g_gla_norm[l][None, :]
    g_ssd = g_ssd_norm[l][None, :]

    cvecs = jnp.zeros((SUBLANES, D_MODEL), F32).at[0].set(c_ctx).at[1:1 + N_LAT_SEQ].set(c)
    mod_flat = _ada_mod(cvecs, w_ada[l], b_ada[l][None, :])
    mods = jnp.zeros((N_MOD, SUBLANES, D_MODEL), F32).at[:, :6].set(
        mod_flat[:N_MOD].reshape(N_MOD, 6, D_MODEL))

    x_ctx = x_prompt.reshape(N_CTX_TOK, D_MODEL)
    x_lat = x_sample.reshape(N_LAT_SEQ * LAT_LEN, D_MODEL)
    pos = _grid_pos_embed(LAT_LEN)

    proj_main, proj_small = _in_proj(x_ctx, x_lat, pos, mods, g_mix_pre[l][None, :], w_main, w_small)

    o_dir, s_gla_dir, y_dir, s_ssd_dir = [], [], [], []
    for d in range(N_DIR):
        rev = d == 1
        wgk_pad = jnp.zeros((LANES, GLA_KEY_WIDTH), F32).at[
            SM_LR + d * GLA_GATE_RANK:SM_LR + (d + 1) * GLA_GATE_RANK].set(w_gk_up[l, d])
        o_d, s_d = _gla_scan(proj_main, proj_small, wgk_pad, b_gk[l, d][None, :],
                             state_gla[:, l, d], rev)
        o_dir.append(o_d)
        s_gla_dir.append(s_d)
        lane0 = SM_DT + d * SSD_HEADS
        dtb_r = _lane_row(dt_bias[l, d], lane0)
        nea_r = _lane_row(-jnp.exp(a_log[l, d]), lane0)
        s0_ssd = state_ssd[:, l, d].reshape(N_LAT_SEQ, SSD_GROUPS, SSD_GROUP_WIDTH, SSD_STATE)
        y_d, t_d = _ssd_scan(proj_main, proj_small, conv_w8, conv_b[l][None, :],
                             dtb_r, nea_r, dtb_r.T, nea_r.T, d_skip_x, s0_ssd, rev, d)
        y_dir.append(y_d)
        s_ssd_dir.append(t_d.reshape(N_CTX_SEQ, SSD_HEADS, SSD_HEAD_DIM, SSD_STATE))
    new_state_gla = jnp.stack(s_gla_dir, axis=1)[:, None]
    new_state_ssd = jnp.stack(s_ssd_dir, axis=1)[:, None]

    x1 = _out_proj(o_dir[0], o_dir[1], y_dir[0], y_dir[1], proj_main, x_ctx, x_lat, pos, mods,
                   g_gla, g_ssd, g_mix_post[l][None, :], w_out_bf)

    h_ffn, idx, wt_tok, rank, cnt = _router(x1, mods, g_ffn_pre[l][None, :], wr_hi, wr_lo,
                                            b_router[l][:, None])
    i32 = jnp.int32
    e_ids = jnp.arange(N_EXPERTS, dtype=i32)
    counts = cnt[:, 0].astype(i32)
    padded = (counts + MOE_BLK - 1) // MOE_BLK * MOE_BLK
    pad_end = jnp.sum(jnp.where(e_ids[None, :] <= e_ids[:, None], padded[None, :], 0), axis=1)
    pad_start = pad_end - padded
    slot0 = jnp.sum(jnp.where(idx[:, :, None] == e_ids, pad_start, 0), axis=-1)
    dest = slot0 + rank
    n_blk = pad_end[-1] // MOE_BLK
    blk_ids = jnp.arange(N_MOE_BLOCKS, dtype=i32)
    blk_src = jnp.minimum(blk_ids, n_blk - 1)
    blk_expert = jnp.minimum(jnp.sum((pad_end[None, :] <= (blk_src * MOE_BLK)[:, None]).astype(i32), axis=1),
                             N_EXPERTS - 1)
    blk_hot = blk_expert[:, None] == e_ids
    nonempty = counts > 0
    ordinal = jnp.sum(jnp.where((e_ids[None, :] < e_ids[:, None]) & nonempty[None, :], 1, 0), axis=1)
    next_e = jnp.min(jnp.where((e_ids[None, :] > e_ids[:, None]) & nonempty[None, :], e_ids[None, :],
                               N_EXPERTS), axis=1)
    next_e = jnp.where(next_e == N_EXPERTS, -1, next_e)
    blk_first = jnp.concatenate([jnp.ones((1,), i32), (blk_expert[1:] != blk_expert[:-1]).astype(i32)])
    blk_next = jnp.sum(jnp.where(blk_hot, next_e, 0), axis=1).astype(i32)
    blk_slot = (jnp.sum(jnp.where(blk_hot, ordinal, 0), axis=1) % 2).astype(i32)
    n_blk_arr = n_blk.astype(i32)[None]
    x_sorted = _dispatch((pad_start + counts).astype(i32), (padded - counts).astype(i32),
                         n_blk_arr, dest, h_ffn)
    y_sorted = _experts(blk_expert.astype(i32), blk_src.astype(i32), n_blk_arr, blk_first, blk_next,
                        blk_slot, x_sorted, w_exp_gate[l], w_exp_up[l], w_exp_down[l])
    out_ctx, out_lat = _combine(dest, y_sorted, wt_tok, h_ffn, x1, mods, g_ffn_post[l][None, :],
                                w_sh_gate[l].astype(BF16), w_sh_up[l].astype(BF16),
                                w_sh_down[l].astype(BF16))
    return (out_ctx.reshape(N_CTX_SEQ, CTX_LEN, D_MODEL), out_lat.reshape(N_LAT_SEQ, LAT_LEN, D_MODEL),
            new_state_gla, new_state_ssd)
```

```python
import functools

import numpy as np
import jax
import jax.numpy as jnp
from jax import lax
from jax.experimental import pallas as pl
from jax.experimental.pallas import tpu as pltpu

F32 = jnp.float32
BF16 = jnp.bfloat16

D_MODEL = 2048
N_CTX_SEQ = 16
CTX_LEN = 256
N_LAT_SEQ = 2
LAT_LEN = 4096
GRID_W = 64
EPS = 1e-6
N_CTX_TOK = N_CTX_SEQ * CTX_LEN
M_TOK = N_CTX_TOK + N_LAT_SEQ * LAT_LEN
SEQ_LENS = (CTX_LEN,) * N_CTX_SEQ + (LAT_LEN,) * N_LAT_SEQ
ROWS_PER_MOD = 4096
N_MOD = M_TOK // ROWS_PER_MOD

GLA_HEADS = 4
GLA_DK = 128
GLA_DV = 256
GLA_KEY_WIDTH = GLA_HEADS * GLA_DK
GLA_WIDTH = GLA_HEADS * GLA_DV
GLA_GATE_RANK = 16
GLA_GATE_TAU = 16.0
GLA_CHUNK = 64
GLA_SCAN_CHUNK = 2 * GLA_CHUNK
GLA_ROWS = 256

SSD_HEADS = 16
SSD_HEAD_DIM = 64
SSD_GROUPS = 2
SSD_HPG = SSD_HEADS // SSD_GROUPS
SSD_STATE = 128
SSD_WIDTH = SSD_HEADS * SSD_HEAD_DIM
SSD_GROUP_WIDTH = SSD_WIDTH // SSD_GROUPS
SSD_CHUNK = 128
SSD_CONV_CH = SSD_WIDTH + 2 * SSD_GROUPS * SSD_STATE
N_DIR = 2

N_EXPERTS = 256
TOP_K = 8
N_EXPERT_GROUPS = 8
GROUP_SIZE = N_EXPERTS // N_EXPERT_GROUPS
TOPK_GROUPS = 4
EXPERT_FF = 512
ROUTED_SCALE = 2.5

C_GOUT = 0
C_Z = C_GOUT + GLA_WIDTH
C_V = C_Z + SSD_WIDTH
C_XBC = C_V + GLA_WIDTH
C_Q = C_XBC + SSD_CONV_CH
C_K = C_Q + GLA_KEY_WIDTH
MAIN_WIDTH = C_K + GLA_KEY_WIDTH
LANES = 128
SUBLANES = 8
SUBLANES_BF16 = 2 * SUBLANES
SM_LR = 0
SM_DT = N_DIR * GLA_GATE_RANK

TM_PROJ = 256
TN_PROJ = 512
TM_OUT = 256
TM_ROUTE = 256
MOE_BLK = 128
N_PAIRS = M_TOK * TOP_K
N_MOE_BLOCKS = N_PAIRS // MOE_BLK + N_EXPERTS
N_SLOTS = N_MOE_BLOCKS * MOE_BLK
T_DISPATCH = 512
T_COMBINE = 128
ADA_TN = 1024
VMEM_LIMIT = 52 * 1024 * 1024


def _mm(a, b):
    return jnp.dot(a, b, preferred_element_type=F32)


def _mm_nt(a, b):
    return lax.dot_general(a, b, (((1,), (1,)), ((), ())), preferred_element_type=F32)


def _mm_tn(a, b):
    return lax.dot_general(a, b, (((0,), (0,)), ((), ())), preferred_element_type=F32)


def _split2(x):
    hi = x.astype(BF16)
    lo = (x - hi.astype(F32)).astype(BF16)
    return hi, lo


def _split3(x):
    hi = x.astype(BF16)
    r = x - hi.astype(F32)
    mid = r.astype(BF16)
    lo = (r - mid.astype(F32)).astype(BF16)
    return hi, mid, lo


def _mm_x3(a, b):
    a_hi, a_lo = _split2(a)
    b_hi, b_lo = _split2(b)
    return _mm(a_hi, b_hi) + _mm(a_lo, b_hi) + _mm(a_hi, b_lo)


def _mm_sel_left(sel_bf, x):
    hi, mid, lo = _split3(x)
    return _mm(sel_bf, hi) + _mm(sel_bf, mid) + _mm(sel_bf, lo)


def _mm_sel_right(x, sel_bf):
    hi, mid, lo = _split3(x)
    return _mm(hi, sel_bf) + _mm(mid, sel_bf) + _mm(lo, sel_bf)


def _sigmoid(x):
    return 1.0 / (1.0 + jnp.exp(-x))


def _silu(x):
    return x * _sigmoid(x)


def _softplus(x):
    return jnp.maximum(x, 0.0) + jnp.log1p(jnp.exp(-jnp.abs(x)))


def _rms(x, g):
    return x * lax.rsqrt(jnp.mean(x * x, axis=-1, keepdims=True) + EPS) * g


def _params(sem, vmem=VMEM_LIMIT):
    return pltpu.CompilerParams(dimension_semantics=sem, vmem_limit_bytes=vmem)


def _ada_kernel(c_ref, w_ref, b_ref, o_ref):
    o_ref[...] = _mm_x3(_silu(c_ref[...]), w_ref[...]) + b_ref[...]


def _ada_mod(cvecs, w_ada, b_ada):
    n_out = w_ada.shape[1]
    return pl.pallas_call(
        _ada_kernel,
        out_shape=jax.ShapeDtypeStruct((SUBLANES, n_out), F32),
        grid=(n_out // ADA_TN,),
        in_specs=[pl.BlockSpec((SUBLANES, D_MODEL), lambda j: (0, 0)),
                  pl.BlockSpec((D_MODEL, ADA_TN), lambda j: (0, j)),
                  pl.BlockSpec((1, ADA_TN), lambda j: (0, j))],
        out_specs=pl.BlockSpec((SUBLANES, ADA_TN), lambda j: (0, j)),
        compiler_params=_params(("arbitrary",)),
        name="ada_mod",
    )(cvecs, w_ada, b_ada)


def _token_specs(tm, buffers=2):
    n_ctx = N_CTX_TOK // tm
    n_pos = LAT_LEN // tm
    mode = dict(pipeline_mode=pl.Buffered(buffers)) if buffers != 2 else {}
    ctx = pl.BlockSpec((tm, D_MODEL), lambda i, *_: (jnp.minimum(i, n_ctx - 1), 0), **mode)
    lat = pl.BlockSpec((tm, D_MODEL), lambda i, *_: (jnp.maximum(i - n_ctx, 0), 0), **mode)
    pos = pl.BlockSpec((tm, D_MODEL), lambda i, *_: (jnp.maximum(i - n_ctx, 0) % n_pos, 0), **mode)
    return n_ctx, [ctx, lat, pos]


def _inproj_kernel(xc_ref, xl_ref, pos_ref, mod_ref, g_ref, w_ref, ws_ref, o_ref, os_ref, h_scr, *, n_ctx):
    def prologue(x):
        h = _rms(x, g_ref[...]) * (1.0 + mod_ref[0, 1:2, :]) + mod_ref[0, 0:1, :]
        h_hi, h_lo = _split2(h)
        h_scr[...] = h_hi
        ws_hi, ws_lo = _split2(ws_ref[...])
        os_ref[...] = _mm(h_hi, ws_hi) + _mm(h_lo, ws_hi) + _mm(h_hi, ws_lo)

    is_ctx = pl.program_id(0) < n_ctx

    @pl.when(is_ctx)
    def _():
        prologue(xc_ref[...])

    @pl.when(jnp.logical_not(is_ctx))
    def _():
        prologue(xl_ref[...] + pos_ref[...])

    for j in range(MAIN_WIDTH // TN_PROJ):
        cols = slice(j * TN_PROJ, (j + 1) * TN_PROJ)
        o_ref[:, cols] = _mm(h_scr[...], w_ref[:, cols]).astype(BF16)


def _in_proj(x_ctx, x_lat, pos, mods, g_pre, w_main, w_small):
    tm = TM_PROJ
    n_ctx, tok_specs = _token_specs(tm)
    const = lambda i: (0, 0)
    once = dict(pipeline_mode=pl.Buffered(1))
    return pl.pallas_call(
        functools.partial(_inproj_kernel, n_ctx=n_ctx),
        out_shape=(jax.ShapeDtypeStruct((M_TOK, MAIN_WIDTH), BF16),
                   jax.ShapeDtypeStruct((M_TOK, LANES), F32)),
        grid=(M_TOK // tm,),
        in_specs=tok_specs + [
                  pl.BlockSpec((1, SUBLANES, D_MODEL), lambda i: (i * tm // ROWS_PER_MOD, 0, 0)),
                  pl.BlockSpec((1, D_MODEL), const),
                  pl.BlockSpec((D_MODEL, MAIN_WIDTH), const, **once),
                  pl.BlockSpec((D_MODEL, LANES), const, **once)],
        out_specs=(pl.BlockSpec((tm, MAIN_WIDTH), lambda i: (i, 0)),
                   pl.BlockSpec((tm, LANES), lambda i: (i, 0))),
        scratch_shapes=[pltpu.VMEM((tm, D_MODEL), BF16)],
        compiler_params=_params(("arbitrary",)),
        name="in_proj",
    )(x_ctx, x_lat, pos, mods, g_pre, w_main, w_small)


def _scan_schedule(rows_per_step, reverse):
    blk, flag, s0i, soi, emit, has_prev, has_next = [], [], [], [], [], [], []
    start = 0
    for s, length in enumerate(SEQ_LENS):
        nb = length // rows_per_step
        is_ctx = s < N_CTX_SEQ
        order = range(nb - 1, -1, -1) if reverse else range(nb)
        for n, b in enumerate(order):
            blk.append(start + b)
            flag.append((1 if is_ctx else 2) if n == 0 else 0)
            s0i.append(0 if is_ctx else s - N_CTX_SEQ)
            soi.append(s if is_ctx else N_CTX_SEQ - 1)
            emit.append(1 if (is_ctx and n == nb - 1) else 0)
            has_prev.append(1 if b > 0 else 0)
            has_next.append(1 if b < nb - 1 else 0)
        start += nb
    return tuple(np.array(a, np.int32) for a in (blk, flag, s0i, soi, emit, has_prev, has_next))


def _gla_kernel(blkf_ref, blkb_ref, flag_ref, s0i_ref, soi_ref, emit_ref, *refs):
    n_in = 7
    ins = (refs[:n_in], refs[n_in:2 * n_in])
    outs = (refs[2 * n_in:2 * n_in + 2], refs[2 * n_in + 2:2 * n_in + 4])
    st_scr = refs[2 * n_in + 4]
    i = pl.program_id(0)
    flag = flag_ref[i]

    @pl.when(flag == 1)
    def _():
        st_scr[...] = jnp.zeros_like(st_scr)

    @pl.when(flag == 2)
    def _():
        for d in range(N_DIR):
            for h in range(GLA_HEADS):
                st_scr[d, h] = ins[d][6][0, h].T

    c = GLA_SCAN_CHUNK
    r_id = lax.broadcasted_iota(jnp.int32, (c, c), 0)
    c_id = lax.broadcasted_iota(jnp.int32, (c, c), 1)
    tri = (c_id <= r_id, c_id >= r_id)
    tri_bf = tuple(jnp.where(m, 1.0, 0.0).astype(BF16) for m in tri)

    log_a = []
    for d in range(N_DIR):
        sm_ref, wgk_ref, bgk_ref = ins[d][3:6]
        gk = _mm_x3(sm_ref[...], wgk_ref[...]) + bgk_ref[...]
        log_a.append((jnp.minimum(gk, 0.0) - jnp.log1p(jnp.exp(-jnp.abs(gk)))) * (1.0 / GLA_GATE_TAU))

    n_chunks = GLA_ROWS // c
    for step in range(n_chunks):
        for d in range(N_DIR):
            reverse = d == 1
            q_ref, k_ref, v_ref = ins[d][:3]
            o_ref = outs[d][0]
            lo = (n_chunks - 1 - step if reverse else step) * c
            b_all = _mm_sel_left(tri_bf[d], log_a[d][lo:lo + c])
            for h in range(GLA_HEADS):
                kc = slice(h * GLA_DK, (h + 1) * GLA_DK)
                vc = slice(h * GLA_DV, (h + 1) * GLA_DV)
                b = b_all[:, kc]
                b_end = b[0:1] if reverse else b[c - 1:c]
                mid = c // 2 if reverse else c // 2 - 1
                b_mid = b[mid:mid + 1]
                q = q_ref[lo:lo + c, kc].astype(F32) * (GLA_DK ** -0.5)
                k = k_ref[lo:lo + c, kc].astype(F32)
                v = v_ref[lo:lo + c, vc]
                q_m = (q * jnp.exp(b - b_mid)).astype(BF16)
                k_m = (k * jnp.exp(b_mid - b)).astype(BF16)
                q_e = (q * jnp.exp(b)).astype(BF16)
                att = jnp.where(tri[d], _mm_nt(q_m, k_m), 0.0).astype(BF16)
                st = st_scr[d, h]
                o_ref[lo:lo + c, vc] = (_mm(att, v) + _mm_nt(q_e, st.astype(BF16))).astype(BF16)
                k_end = (k * jnp.exp(b_end - b)).astype(BF16)
                st_scr[d, h] = st * jnp.exp(b_end) + _mm_tn(v, k_end)

    @pl.when(emit_ref[i] == 1)
    def _():
        for d in range(N_DIR):
            for h in range(GLA_HEADS):
                outs[d][1][0, h] = st_scr[d, h].T


def _gla_scan(proj_main, proj_small, wgk_pads, bgks, s0s):
    sched_f = _scan_schedule(GLA_ROWS, False)
    sched_b = _scan_schedule(GLA_ROWS, True)
    for a, b in zip(sched_f[1:5], sched_b[1:5]):
        assert np.array_equal(a, b)
    n_steps = M_TOK // GLA_ROWS
    t = GLA_ROWS
    q_blk, k_blk, v_blk = C_Q // GLA_KEY_WIDTH, C_K // GLA_KEY_WIDTH, C_V // GLA_WIDTH
    state_blk = (1, GLA_HEADS, GLA_DK, GLA_DV)

    def dir_in_specs(d):
        blk_of = lambda refs: refs[d]
        return [
            pl.BlockSpec((t, GLA_KEY_WIDTH), lambda i, *r: (blk_of(r)[i], q_blk)),
            pl.BlockSpec((t, GLA_KEY_WIDTH), lambda i, *r: (blk_of(r)[i], k_blk)),
            pl.BlockSpec((t, GLA_WIDTH), lambda i, *r: (blk_of(r)[i], v_blk)),
            pl.BlockSpec((t, LANES), lambda i, *r: (blk_of(r)[i], 0)),
            pl.BlockSpec((LANES, GLA_KEY_WIDTH), lambda i, *r: (0, 0)),
            pl.BlockSpec((1, GLA_KEY_WIDTH), lambda i, *r: (0, 0)),
            pl.BlockSpec(state_blk, lambda i, *r: (r[3][i], 0, 0, 0)),
        ]

    def dir_out_specs(d):
        blk_of = lambda refs: refs[d]
        return [pl.BlockSpec((t, GLA_WIDTH), lambda i, *r: (blk_of(r)[i], 0)),
                pl.BlockSpec(state_blk, lambda i, *r: (r[4][i], 0, 0, 0))]

    dir_out_shape = [jax.ShapeDtypeStruct((M_TOK, GLA_WIDTH), BF16),
                     jax.ShapeDtypeStruct((N_CTX_SEQ,) + state_blk[1:], F32)]
    operands = []
    for d in range(N_DIR):
        operands += [proj_main, proj_main, proj_main, proj_small, wgk_pads[d], bgks[d], s0s[d]]
    o_f, s_f, o_b, s_b = pl.pallas_call(
        _gla_kernel,
        out_shape=tuple(dir_out_shape * N_DIR),
        grid_spec=pltpu.PrefetchScalarGridSpec(
            num_scalar_prefetch=6,
            grid=(n_steps,),
            in_specs=dir_in_specs(0) + dir_in_specs(1),
            out_specs=tuple(dir_out_specs(0) + dir_out_specs(1)),
            scratch_shapes=[pltpu.VMEM((N_DIR, GLA_HEADS, GLA_DV, GLA_DK), F32)]),
        compiler_params=_params(("arbitrary",)),
        name="gla_scan",
    )(*(jnp.asarray(a) for a in (sched_f[0], sched_b[0]) + sched_f[1:5]), *operands)
    return (o_f, s_f), (o_b, s_b)


def _ssd_kernel(blk_ref, flag_ref, s0i_ref, soi_ref, emit_ref, hp_ref, hn_ref,
                xbc_ref, xprev_ref, xnext_ref, sm_ref, cw_ref, cb_ref,
                dtb_r_ref, nea_r_ref, dtb_c_ref, nea_c_ref, dsk_ref, s0_ref,
                y_ref, so_ref, st_scr, *, reverse, lane0, add_skip):
    i = pl.program_id(0)
    flag = flag_ref[i]
    t = SSD_CHUNK

    @pl.when(flag == 1)
    def _():
        st_scr[...] = jnp.zeros_like(st_scr)

    @pl.when(flag == 2)
    def _():
        for g in range(SSD_GROUPS):
            st_scr[g] = s0_ref[0, g].T

    xbc = xbc_ref[...].astype(F32)
    last = SUBLANES_BF16 - 1
    prev = jnp.where(hp_ref[i] == 1, xprev_ref[last:last + 1, :].astype(F32), 0.0)
    nxt = jnp.where(hn_ref[i] == 1, xnext_ref[0:1, :].astype(F32), 0.0)
    row = lax.broadcasted_iota(jnp.int32, xbc.shape, 0)
    x_m1 = jnp.where(row == 0, prev, pltpu.roll(xbc, 1, 0))
    x_p1 = jnp.where(row == t - 1, nxt, pltpu.roll(xbc, t - 1, 0))
    act = _silu(x_m1 * cw_ref[0:1, :] + xbc * cw_ref[1:2, :] + x_p1 * cw_ref[2:3, :] + cb_ref[...])
    xs = act[:, :SSD_WIDTH]
    bm = act[:, SSD_WIDTH:SSD_WIDTH + SSD_GROUPS * SSD_STATE]
    cm = act[:, SSD_WIDTH + SSD_GROUPS * SSD_STATE:]

    sm = sm_ref[...]
    dt = _softplus(sm + dtb_r_ref[...])
    a = dt * nea_r_ref[...]
    a_t = _softplus(sm.T + dtb_c_ref[...]) * nea_c_ref[...]

    r_id = lax.broadcasted_iota(jnp.int32, (t, t), 0)
    c_id = lax.broadcasted_iota(jnp.int32, (t, t), 1)
    tri = (c_id >= r_id) if reverse else (c_id <= r_id)
    tri_bf = jnp.where(tri, 1.0, 0.0).astype(BF16)
    tri_t_bf = jnp.where((r_id >= c_id) if reverse else (r_id <= c_id), 1.0, 0.0).astype(BF16)
    cs = _mm_sel_left(tri_bf, a)
    cs_t = _mm_sel_right(a_t, tri_t_bf)

    e_r = lax.broadcasted_iota(jnp.int32, (LANES, SSD_WIDTH), 0)
    e_c = lax.broadcasted_iota(jnp.int32, (LANES, SSD_WIDTH), 1)
    expand = jnp.where(e_r - lane0 == e_c // SSD_HEAD_DIM, 1.0, 0.0).astype(BF16)
    dt_x = _mm_sel_right(dt, expand)
    cs_x = _mm_sel_right(cs, expand)
    cs_end_x = cs_x[0:1] if reverse else cs_x[t - 1:t]
    x_in = xs * dt_x
    x_bf = x_in.astype(BF16)
    x_w = (x_in * jnp.exp(cs_end_x - cs_x)).astype(BF16)
    decay_in = jnp.exp(cs_x)
    decay_end = jnp.exp(cs_end_x)

    for g in range(SSD_GROUPS):
        gc = slice(g * SSD_STATE, (g + 1) * SSD_STATE)
        gw = slice(g * SSD_GROUP_WIDTH, (g + 1) * SSD_GROUP_WIDTH)
        c_g = cm[:, gc].astype(BF16)
        b_g = bm[:, gc].astype(BF16)
        cb = _mm_nt(c_g, b_g)
        st = st_scr[g]
        y_state = _mm(c_g, st.astype(BF16)) * decay_in[:, gw]
        for r in range(SSD_HPG):
            hh = g * SSD_HPG + r
            hc = slice(hh * SSD_HEAD_DIM, (hh + 1) * SSD_HEAD_DIM)
            lane = lane0 + hh
            seg = cs[:, lane:lane + 1] - cs_t[lane:lane + 1, :]
            lm = jnp.exp(jnp.where(tri, seg, -jnp.inf))
            y_h = _mm((cb * lm).astype(BF16), x_bf[:, hc]) + y_state[:, r * SSD_HEAD_DIM:(r + 1) * SSD_HEAD_DIM]
            if add_skip:
                y_h = y_h + xs[:, hc] * dsk_ref[:, hc]
            y_ref[:, hc] = y_h.astype(BF16)
        st_scr[g] = st * decay_end[:, gw] + _mm_tn(b_g, x_w[:, gw])

    @pl.when(emit_ref[i] == 1)
    def _():
        for g in range(SSD_GROUPS):
            so_ref[0, g] = st_scr[g].T


def _ssd_scan(proj_main, proj_small, conv_w, conv_b, dtb_r, nea_r, dtb_c, nea_c, d_skip_x, s0,
              reverse, direction):
    sched = tuple(jnp.asarray(a) for a in _scan_schedule(SSD_CHUNK, reverse))
    t = SSD_CHUNK
    n_steps = M_TOK // t
    xbc_blk = C_XBC // SSD_CONV_CH
    halo = SUBLANES_BF16
    rb = t // halo
    n_rb = M_TOK // halo
    state_blk = (1, SSD_GROUPS, SSD_GROUP_WIDTH, SSD_STATE)
    return pl.pallas_call(
        functools.partial(_ssd_kernel, reverse=reverse, lane0=SM_DT + direction * SSD_HEADS,
                          add_skip=not reverse),
        out_shape=(jax.ShapeDtypeStruct((M_TOK, SSD_WIDTH), BF16),
                   jax.ShapeDtypeStruct((N_CTX_SEQ,) + state_blk[1:], F32)),
        grid_spec=pltpu.PrefetchScalarGridSpec(
            num_scalar_prefetch=7,
            grid=(n_steps,),
            in_specs=[
                pl.BlockSpec((t, SSD_CONV_CH), lambda i, blk, *_: (blk[i], xbc_blk)),
                pl.BlockSpec((halo, SSD_CONV_CH),
                             lambda i, blk, *_: (jnp.maximum(blk[i] * rb - 1, 0), xbc_blk)),
                pl.BlockSpec((halo, SSD_CONV_CH),
                             lambda i, blk, *_: (jnp.minimum((blk[i] + 1) * rb, n_rb - 1), xbc_blk)),
                pl.BlockSpec((t, LANES), lambda i, blk, *_: (blk[i], 0)),
                pl.BlockSpec((SUBLANES, SSD_CONV_CH), lambda i, *_: (0, 0)),
                pl.BlockSpec((1, SSD_CONV_CH), lambda i, *_: (0, 0)),
                pl.BlockSpec((1, LANES), lambda i, *_: (0, 0)),
                pl.BlockSpec((1, LANES), lambda i, *_: (0, 0)),
                pl.BlockSpec((LANES, 1), lambda i, *_: (0, 0)),
                pl.BlockSpec((LANES, 1), lambda i, *_: (0, 0)),
                pl.BlockSpec((1, SSD_WIDTH), lambda i, *_: (0, 0)),
                pl.BlockSpec(state_blk, lambda i, blk, flag, s0i, *_: (s0i[i], 0, 0, 0)),
            ],
            out_specs=(
                pl.BlockSpec((t, SSD_WIDTH), lambda i, blk, *_: (blk[i], 0)),
                pl.BlockSpec(state_blk, lambda i, blk, flag, s0i, soi, *_: (soi[i], 0, 0, 0)),
            ),
            scratch_shapes=[pltpu.VMEM((SSD_GROUPS, SSD_STATE, SSD_GROUP_WIDTH), F32)]),
        compiler_params=_params(("arbitrary",)),
        name="ssd_bwd" if reverse else "ssd_fwd",
    )(*sched, proj_main, proj_main, proj_main, proj_small, conv_w, conv_b,
      dtb_r, nea_r, dtb_c, nea_c, d_skip_x, s0)


def _outproj_kernel(of_ref, ob_ref, gout_ref, yf_ref, yb_ref, z_ref, xc_ref, xl_ref, pos_ref, mod_ref,
                    ggla_ref, gssd_ref, gpost_ref, w_ref, o_ref, *, n_ctx):
    o = of_ref[...].astype(F32) + ob_ref[...].astype(F32)
    gate = _silu(gout_ref[...].astype(F32))
    parts = []
    for h in range(GLA_HEADS):
        hc = slice(h * GLA_DV, (h + 1) * GLA_DV)
        parts.append((_rms(o[:, hc], ggla_ref[...]) * gate[:, hc]).astype(BF16))
    y = (yf_ref[...].astype(F32) + yb_ref[...].astype(F32)) * _silu(z_ref[...].astype(F32))
    for g in range(SSD_GROUPS):
        gw = slice(g * SSD_GROUP_WIDTH, (g + 1) * SSD_GROUP_WIDTH)
        parts.append(_rms(y[:, gw], gssd_ref[:, gw]).astype(BF16))
    acc = None
    col = 0
    for p in parts:
        term = _mm(p, w_ref[col:col + p.shape[1], :])
        acc = term if acc is None else acc + term
        col += p.shape[1]
    delta = mod_ref[0, 2:3, :] * _rms(acc, gpost_ref[...])
    is_ctx = pl.program_id(0) < n_ctx

    @pl.when(is_ctx)
    def _():
        o_ref[...] = xc_ref[...] + delta

    @pl.when(jnp.logical_not(is_ctx))
    def _():
        o_ref[...] = xl_ref[...] + pos_ref[...] + delta


def _out_proj(o_f, o_b, y_f, y_b, proj_main, x_ctx, x_lat, pos, mods, g_gla, g_ssd, g_post, w_out):
    tm = TM_OUT
    row = lambda i: (i, 0)
    const = lambda i: (0, 0)
    n_ctx, tok_specs = _token_specs(tm)
    return pl.pallas_call(
        functools.partial(_outproj_kernel, n_ctx=n_ctx),
        out_shape=jax.ShapeDtypeStruct((M_TOK, D_MODEL), F32),
        grid=(M_TOK // tm,),
        in_specs=[pl.BlockSpec((tm, GLA_WIDTH), row),
                  pl.BlockSpec((tm, GLA_WIDTH), row),
                  pl.BlockSpec((tm, GLA_WIDTH), lambda i: (i, C_GOUT // GLA_WIDTH)),
                  pl.BlockSpec((tm, SSD_WIDTH), row),
                  pl.BlockSpec((tm, SSD_WIDTH), row),
                  pl.BlockSpec((tm, SSD_WIDTH), lambda i: (i, C_Z // SSD_WIDTH))] + tok_specs + [
                  pl.BlockSpec((1, SUBLANES, D_MODEL), lambda i: (i * tm // ROWS_PER_MOD, 0, 0)),
                  pl.BlockSpec((1, GLA_DV), const),
                  pl.BlockSpec((1, SSD_WIDTH), const),
                  pl.BlockSpec((1, D_MODEL), const),
                  pl.BlockSpec((D_MODEL, D_MODEL), const)],
        out_specs=pl.BlockSpec((tm, D_MODEL), row),
        compiler_params=_params(("arbitrary",)),
        name="out_proj",
    )(o_f, o_b, proj_main, y_f, y_b, proj_main, x_ctx, x_lat, pos, mods, g_gla, g_ssd, g_post, w_out)


def _router_kernel(x_ref, mod_ref, g_ref, wr_hi_ref, wr_lo_ref, br_ref,
                   h_ref, idx_ref, wt_ref, rank_ref, cnt_ref, cnt_scr):
    i = pl.program_id(0)
    tm = TM_ROUTE

    @pl.when(i == 0)
    def _():
        cnt_scr[...] = jnp.zeros_like(cnt_scr)

    h = _rms(x_ref[...], g_ref[...]) * (1.0 + mod_ref[0, 4:5, :]) + mod_ref[0, 3:4, :]
    h_ref[...] = h
    h_hi, h_lo = _split2(h)
    wr_hi = wr_hi_ref[...]
    logits = _mm_nt(wr_hi, h_hi) + _mm_nt(wr_hi, h_lo) + _mm_nt(wr_lo_ref[...], h_hi)
    scores = _sigmoid(logits)
    sel = scores + br_ref[...]
    neg = -jnp.inf

    def first_argmax(x, ids, n):
        m = jnp.max(x, axis=0, keepdims=True)
        return m, jnp.min(jnp.where(x == m, ids, float(n)), axis=0, keepdims=True)

    ids_g = lax.broadcasted_iota(jnp.int32, (GROUP_SIZE, tm), 0).astype(F32)
    grp = []
    for g in range(N_EXPERT_GROUPS):
        xg = sel[g * GROUP_SIZE:(g + 1) * GROUP_SIZE]
        m1, a1 = first_argmax(xg, ids_g, GROUP_SIZE)
        m2 = jnp.max(jnp.where(ids_g == a1, neg, xg), axis=0, keepdims=True)
        grp.append(m1 + m2)
    gsc = jnp.concatenate(grp, axis=0)
    ids_8 = lax.broadcasted_iota(jnp.int32, (N_EXPERT_GROUPS, tm), 0).astype(F32)
    keep = jnp.zeros((N_EXPERT_GROUPS, tm), F32)
    for _ in range(TOPK_GROUPS):
        _, a = first_argmax(gsc, ids_8, N_EXPERT_GROUPS)
        pick = ids_8 == a
        keep = jnp.where(pick, 1.0, keep)
        gsc = jnp.where(pick, neg, gsc)
    selm = jnp.concatenate(
        [jnp.where(keep[g:g + 1] > 0.5, sel[g * GROUP_SIZE:(g + 1) * GROUP_SIZE], neg)
         for g in range(N_EXPERT_GROUPS)], axis=0)

    ids_e = lax.broadcasted_iota(jnp.int32, (N_EXPERTS, tm), 0).astype(F32)
    picks, wts = [], []
    chosen = jnp.zeros((N_EXPERTS, tm), F32)
    for _ in range(TOP_K):
        _, a = first_argmax(selm, ids_e, N_EXPERTS)
        hit = ids_e == a
        picks.append(a)
        wts.append(jnp.sum(jnp.where(hit, scores, 0.0), axis=0, keepdims=True))
        chosen = jnp.where(hit, 1.0, chosen)
        selm = jnp.where(hit, neg, selm)
    w = jnp.concatenate(wts, axis=0)
    w = w / jnp.sum(w, axis=0, keepdims=True) * ROUTED_SCALE
    idx_ref[...] = jnp.concatenate(picks, axis=0).astype(jnp.int32)
    wt_ref[...] = jnp.concatenate([w, jnp.zeros((LANES - TOP_K, tm), F32)], axis=0).T

    t_r = lax.broadcasted_iota(jnp.int32, (tm, tm), 0)
    t_c = lax.broadcasted_iota(jnp.int32, (tm, tm), 1)
    before = jnp.where(t_r < t_c, 1.0, 0.0).astype(BF16)
    base = _mm(chosen.astype(BF16), before) + cnt_scr[...]
    rank_ref[...] = jnp.concatenate(
        [jnp.sum(jnp.where(ids_e == a, base, 0.0), axis=0, keepdims=True) for a in picks],
        axis=0).astype(jnp.int32)
    cnt_scr[...] = cnt_scr[...] + jnp.sum(chosen, axis=1, keepdims=True)
    cnt_ref[...] = jnp.broadcast_to(cnt_scr[...], cnt_ref.shape)


def _router(x1, mods, g_pre, wr_hi, wr_lo, b_router):
    tm = TM_ROUTE
    const = lambda i: (0, 0)
    return pl.pallas_call(
        _router_kernel,
        out_shape=(jax.ShapeDtypeStruct((M_TOK, D_MODEL), F32),
                   jax.ShapeDtypeStruct((TOP_K, M_TOK), jnp.int32),
                   jax.ShapeDtypeStruct((M_TOK, LANES), F32),
                   jax.ShapeDtypeStruct((TOP_K, M_TOK), jnp.int32),
                   jax.ShapeDtypeStruct((N_EXPERTS, LANES), F32)),
        grid=(M_TOK // tm,),
        in_specs=[pl.BlockSpec((tm, D_MODEL), lambda i: (i, 0)),
                  pl.BlockSpec((1, SUBLANES, D_MODEL), lambda i: (i * tm // ROWS_PER_MOD, 0, 0)),
                  pl.BlockSpec((1, D_MODEL), const),
                  pl.BlockSpec((N_EXPERTS, D_MODEL), const),
                  pl.BlockSpec((N_EXPERTS, D_MODEL), const),
                  pl.BlockSpec((N_EXPERTS, 1), const)],
        out_specs=(pl.BlockSpec((tm, D_MODEL), lambda i: (i, 0)),
                   pl.BlockSpec((TOP_K, tm), lambda i: (0, i)),
                   pl.BlockSpec((tm, LANES), lambda i: (i, 0)),
                   pl.BlockSpec((TOP_K, tm), lambda i: (0, i)),
                   pl.BlockSpec((N_EXPERTS, LANES), const)),
        scratch_shapes=[pltpu.VMEM((N_EXPERTS, 1), F32)],
        compiler_params=_params(("arbitrary",)),
        name="router",
    )(x1, mods, g_pre, wr_hi, wr_lo, b_router)


def _dispatch_kernel(fill_start_ref, fill_len_ref, nb_ref, dest_ref, h_ref, xs_hbm, zero_scr, sem, zsem):
    @pl.when(pl.program_id(0) == 0)
    def _():
        zero_scr[...] = jnp.zeros_like(zero_scr)

        def for_each_fill(act):
            def pad_body(e, carry):
                start = fill_start_ref[e]
                length = fill_len_ref[e]
                head = jnp.minimum((-start) & (SUBLANES - 1), length)
                for j in range(SUBLANES - 1):
                    @pl.when(j < head)
                    def _():
                        act(pltpu.make_async_copy(zero_scr.at[pl.ds(0, 1)],
                                                  xs_hbm.at[pl.ds(start + j, 1)], zsem))
                body_start = start + head
                body_len = length - head
                for bit in (64, 32, 16, 8):
                    @pl.when((body_len & bit) != 0)
                    def _():
                        off = pl.multiple_of(body_start + (body_len & jnp.int32(~(2 * bit - 1))), SUBLANES)
                        act(pltpu.make_async_copy(zero_scr.at[pl.ds(0, bit)],
                                                  xs_hbm.at[pl.ds(off, bit)], zsem))
                return carry

            def tail_body(b, carry):
                act(pltpu.make_async_copy(zero_scr, xs_hbm.at[pl.ds(b * MOE_BLK, MOE_BLK)], zsem))
                return carry

            lax.fori_loop(0, N_EXPERTS, pad_body, 0)
            lax.fori_loop(nb_ref[0], N_MOE_BLOCKS, tail_body, 0)

        for_each_fill(lambda cp: cp.start())
        for_each_fill(lambda cp: cp.wait())

    def body(t, carry):
        for k in range(TOP_K):
            pltpu.make_async_copy(h_ref.at[pl.ds(t, 1)],
                                  xs_hbm.at[pl.ds(dest_ref[k, t], 1)], sem).start()
        return carry

    lax.fori_loop(0, T_DISPATCH, body, 0)
    pltpu.make_async_copy(xs_hbm.at[pl.ds(0, T_DISPATCH * TOP_K)],
                          xs_hbm.at[pl.ds(0, T_DISPATCH * TOP_K)], sem).wait()


def _dispatch(fill_start, fill_len, n_blk, dest, h):
    return pl.pallas_call(
        _dispatch_kernel,
        out_shape=jax.ShapeDtypeStruct((N_SLOTS, D_MODEL), F32),
        grid_spec=pltpu.PrefetchScalarGridSpec(
            num_scalar_prefetch=3,
            grid=(M_TOK // T_DISPATCH,),
            in_specs=[pl.BlockSpec((TOP_K, T_DISPATCH), lambda i, *_: (0, i), memory_space=pltpu.SMEM),
                      pl.BlockSpec((T_DISPATCH, D_MODEL), lambda i, *_: (i, 0))],
            out_specs=pl.BlockSpec(memory_space=pl.ANY),
            scratch_shapes=[pltpu.VMEM((MOE_BLK, D_MODEL), F32),
                            pltpu.SemaphoreType.DMA, pltpu.SemaphoreType.DMA]),
        compiler_params=_params(("arbitrary",)),
        name="moe_dispatch",
    )(fill_start, fill_len, n_blk, dest, h)


def _expert_weight_copies(e, s, w_hbm, w_f32, sem):
    return [pltpu.make_async_copy(w_hbm[j].at[e], w_f32[j].at[s], sem.at[s, j]) for j in range(3)]


def _expert_kernel(be_ref, bs_ref, nb_ref, first_ref, next_ref, slot_ref,
                   x_ref, wg_hbm, wu_hbm, wd_hbm, y_ref,
                   wg_f32, wu_f32, wd_f32, wg_scr, wu_scr, wd_scr, sem):
    b = pl.program_id(0)
    w_hbm = (wg_hbm, wu_hbm, wd_hbm)
    w_f32 = (wg_f32, wu_f32, wd_f32)

    @pl.when(b == 0)
    def _():
        for cp in _expert_weight_copies(be_ref[0], 0, w_hbm, w_f32, sem):
            cp.start()

    @pl.when(b < nb_ref[0])
    def _():
        @pl.when(first_ref[b] == 1)
        def _():
            s = slot_ref[b]
            for cp in _expert_weight_copies(be_ref[b], s, w_hbm, w_f32, sem):
                cp.wait()

            @pl.when(next_ref[b] >= 0)
            def _():
                for cp in _expert_weight_copies(next_ref[b], 1 - s, w_hbm, w_f32, sem):
                    cp.start()

            wg_scr[...] = wg_f32[s].astype(BF16)
            wu_scr[...] = wu_f32[s].astype(BF16)
            wd_scr[...] = wd_f32[s].astype(BF16)

        x = x_ref[...].astype(BF16)
        act = _silu(_mm(x, wg_scr[...])) * _mm(x, wu_scr[...])
        y_ref[...] = _mm(act.astype(BF16), wd_scr[...])

    @pl.when(b >= nb_ref[0])
    def _():
        y_ref[...] = jnp.zeros_like(y_ref)


def _experts(blk_expert, blk_src, n_blk, blk_first, blk_next, blk_slot, x_sorted, w_g, w_u, w_d):
    hbm = pl.BlockSpec(memory_space=pl.ANY)
    return pl.pallas_call(
        _expert_kernel,
        out_shape=jax.ShapeDtypeStruct((N_SLOTS, D_MODEL), F32),
        grid_spec=pltpu.PrefetchScalarGridSpec(
            num_scalar_prefetch=6,
            grid=(N_MOE_BLOCKS,),
            in_specs=[pl.BlockSpec((MOE_BLK, D_MODEL), lambda b, be, bs, *_: (bs[b], 0)), hbm, hbm, hbm],
            out_specs=pl.BlockSpec((MOE_BLK, D_MODEL), lambda b, *_: (b, 0)),
            scratch_shapes=[pltpu.VMEM((2, D_MODEL, EXPERT_FF), F32),
                            pltpu.VMEM((2, D_MODEL, EXPERT_FF), F32),
                            pltpu.VMEM((2, EXPERT_FF, D_MODEL), F32),
                            pltpu.VMEM((D_MODEL, EXPERT_FF), BF16),
                            pltpu.VMEM((D_MODEL, EXPERT_FF), BF16),
                            pltpu.VMEM((EXPERT_FF, D_MODEL), BF16),
                            pltpu.SemaphoreType.DMA((2, 3))]),
        compiler_params=_params(("arbitrary",)),
        name="moe_experts",
    )(blk_expert, blk_src, n_blk, blk_first, blk_next, blk_slot, x_sorted, w_g, w_u, w_d)


def _combine_copy(y_hbm, buf, sem, slot, k, t, src_row):
    return pltpu.make_async_copy(y_hbm.at[pl.ds(src_row, 1)], buf.at[slot, k, pl.ds(t, 1)], sem.at[slot])


def _combine_kernel(dest_ref, dest_next_ref, y_hbm, wt_ref, h_ref, x_ref, mod_ref, g_ref,
                    wsg_ref, wsu_ref, wsd_ref, oc_ref, ol_ref, buf, sem, *, n_ctx):
    i = pl.program_id(0)
    n = pl.num_programs(0)
    slot = i % 2

    def issue(d_ref, s):
        def body(t, carry):
            for k in range(TOP_K):
                _combine_copy(y_hbm, buf, sem, s, k, t, d_ref[k, t]).start()
            return carry
        lax.fori_loop(0, T_COMBINE, body, 0)

    @pl.when(i == 0)
    def _():
        issue(dest_ref, 0)

    @pl.when(i + 1 < n)
    def _():
        issue(dest_next_ref, 1 - slot)

    pltpu.make_async_copy(buf.at[slot], buf.at[slot], sem.at[slot]).wait()

    wt = wt_ref[...]
    acc = buf[slot, 0] * wt[:, 0:1]
    for k in range(1, TOP_K):
        acc = acc + buf[slot, k] * wt[:, k:k + 1]
    h = h_ref[...].astype(BF16)
    act = _silu(_mm(h, wsg_ref[...])) * _mm(h, wsu_ref[...])
    y = acc + _mm(act.astype(BF16), wsd_ref[...])
    out = x_ref[...] + mod_ref[0, 5:6, :] * _rms(y, g_ref[...])

    @pl.when(i < n_ctx)
    def _():
        oc_ref[...] = out

    @pl.when(i >= n_ctx)
    def _():
        ol_ref[...] = out


def _combine(dest, y_sorted, wt_tok, h, x1, mods, g_post, ws_g, ws_u, ws_d):
    tc = T_COMBINE
    n = M_TOK // tc
    n_ctx = N_CTX_TOK // tc
    row = lambda i: (i, 0)
    const = lambda i: (0, 0)
    return pl.pallas_call(
        functools.partial(_combine_kernel, n_ctx=n_ctx),
        out_shape=(jax.ShapeDtypeStruct((N_CTX_TOK, D_MODEL), F32),
                   jax.ShapeDtypeStruct((M_TOK - N_CTX_TOK, D_MODEL), F32)),
        grid=(n,),
        in_specs=[pl.BlockSpec((TOP_K, tc), lambda i: (0, i), memory_space=pltpu.SMEM),
                  pl.BlockSpec((TOP_K, tc), lambda i: (0, jnp.minimum(i + 1, n - 1)),
                               memory_space=pltpu.SMEM),
                  pl.BlockSpec(memory_space=pl.ANY),
                  pl.BlockSpec((tc, LANES), row),
                  pl.BlockSpec((tc, D_MODEL), row),
                  pl.BlockSpec((tc, D_MODEL), row),
                  pl.BlockSpec((1, SUBLANES, D_MODEL), lambda i: (i * tc // ROWS_PER_MOD, 0, 0)),
                  pl.BlockSpec((1, D_MODEL), const),
                  pl.BlockSpec((D_MODEL, EXPERT_FF), const),
                  pl.BlockSpec((D_MODEL, EXPERT_FF), const),
                  pl.BlockSpec((EXPERT_FF, D_MODEL), const)],
        out_specs=(pl.BlockSpec((tc, D_MODEL), lambda i: (jnp.minimum(i, n_ctx - 1), 0)),
                   pl.BlockSpec((tc, D_MODEL), lambda i: (jnp.maximum(i - n_ctx, 0), 0))),
        scratch_shapes=[pltpu.VMEM((2, TOP_K, tc, D_MODEL), F32),
                        pltpu.SemaphoreType.DMA((2,))],
        compiler_params=_params(("arbitrary",)),
        name="moe_combine",
    )(dest, dest, y_sorted, wt_tok, h, x1, mods, g_post, ws_g, ws_u, ws_d)


def _grid_pos_embed(n_tokens):
    rows = n_tokens // GRID_W
    half = D_MODEL // 2
    quarter = half // 2
    omega = 1.0 / (10000.0 ** (jnp.arange(quarter, dtype=F32) / quarter))

    def axis_embed(pos):
        ang = pos.astype(F32)[:, None] * omega
        return jnp.concatenate([jnp.sin(ang), jnp.cos(ang)], axis=-1)

    e_row = axis_embed(jnp.arange(rows))
    e_col = axis_embed(jnp.arange(GRID_W))
    emb = jnp.concatenate([jnp.broadcast_to(e_row[:, None], (rows, GRID_W, half)),
                           jnp.broadcast_to(e_col[None], (rows, GRID_W, half))], axis=-1)
    return emb.reshape(rows * GRID_W, D_MODEL)


def _lane_row(v, lane0):
    return jnp.zeros((1, LANES), F32).at[0, lane0:lane0 + v.shape[0]].set(v)


def kernel(x_prompt, x_sample, state_gla, state_ssd, c, c_ctx, w_ada, b_ada, g_mix_pre, g_mix_post, w_in, w_gk_up, b_gk, g_gla_norm, conv_w, conv_b, dt_bias, a_log, d_skip, g_ssd_norm, w_out, g_ffn_pre, g_ffn_post, w_router, b_router, w_exp_gate, w_exp_up, w_exp_down, w_sh_gate, w_sh_up, w_sh_down):
    assert x_prompt.shape == (N_CTX_SEQ, CTX_LEN, D_MODEL) and x_sample.shape == (N_LAT_SEQ, LAT_LEN, D_MODEL)
    assert w_ada.shape[0] == 1, "single layer"
    l = 0

    w_in_l = w_in[l]
    o_q = 0
    o_k = o_q + GLA_KEY_WIDTH
    o_v = o_k + GLA_KEY_WIDTH
    o_gout = o_v + GLA_WIDTH
    o_lr = o_gout + GLA_WIDTH
    o_z = o_lr + N_DIR * GLA_GATE_RANK
    o_xbc = o_z + SSD_WIDTH
    o_dt = o_xbc + SSD_CONV_CH
    o_end = o_dt + N_DIR * SSD_HEADS
    assert o_end == w_in_l.shape[1]
    w_main = jnp.concatenate([w_in_l[:, o_gout:o_lr], w_in_l[:, o_z:o_xbc], w_in_l[:, o_v:o_gout],
                              w_in_l[:, o_xbc:o_dt], w_in_l[:, o_q:o_k], w_in_l[:, o_k:o_v]],
                             axis=1).astype(BF16)
    w_small = jnp.concatenate([w_in_l[:, o_lr:o_z], w_in_l[:, o_dt:o_end],
                               jnp.zeros((D_MODEL, LANES - SM_DT - N_DIR * SSD_HEADS), F32)], axis=1)
    w_out_bf = w_out[l].astype(BF16)
    wr_t = w_router[l].T
    wr_hi = wr_t.astype(BF16)
    wr_lo = (wr_t - wr_hi.astype(F32)).astype(BF16)
    conv_w8 = jnp.zeros((SUBLANES, SSD_CONV_CH), F32).at[:conv_w.shape[1]].set(conv_w[l])
    d_skip_x = jnp.repeat(d_skip[l], SSD_HEAD_DIM)[None, :]
    g_gla = g_gla_norm[l][None, :]
    g_ssd = g_ssd_norm[l][None, :]

    cvecs = jnp.zeros((SUBLANES, D_MODEL), F32).at[0].set(c_ctx).at[1:1 + N_LAT_SEQ].set(c)
    mod_flat = _ada_mod(cvecs, w_ada[l], b_ada[l][None, :])
    mods = jnp.zeros((N_MOD, SUBLANES, D_MODEL), F32).at[:, :6].set(
        mod_flat[:N_MOD].reshape(N_MOD, 6, D_MODEL))

    x_ctx = x_prompt.reshape(N_CTX_TOK, D_MODEL)
    x_lat = x_sample.reshape(N_LAT_SEQ * LAT_LEN, D_MODEL)
    pos = _grid_pos_embed(LAT_LEN)

    proj_main, proj_small = _in_proj(x_ctx, x_lat, pos, mods, g_mix_pre[l][None, :], w_main, w_small)

    wgk_pads = [jnp.zeros((LANES, GLA_KEY_WIDTH), F32).at[
        SM_LR + d * GLA_GATE_RANK:SM_LR + (d + 1) * GLA_GATE_RANK].set(w_gk_up[l, d]) for d in range(N_DIR)]
    gla_out = _gla_scan(proj_main, proj_small, wgk_pads, [b_gk[l, d][None, :] for d in range(N_DIR)],
                        [state_gla[:, l, d] for d in range(N_DIR)])
    o_dir = [gla_out[d][0] for d in range(N_DIR)]
    s_gla_dir = [gla_out[d][1] for d in range(N_DIR)]
    y_dir, s_ssd_dir = [], []
    for d in range(N_DIR):
        rev = d == 1
        lane0 = SM_DT + d * SSD_HEADS
        dtb_r = _lane_row(dt_bias[l, d], lane0)
        nea_r = _lane_row(-jnp.exp(a_log[l, d]), lane0)
        s0_ssd = state_ssd[:, l, d].reshape(N_LAT_SEQ, SSD_GROUPS, SSD_GROUP_WIDTH, SSD_STATE)
        y_d, t_d = _ssd_scan(proj_main, proj_small, conv_w8, conv_b[l][None, :],
                             dtb_r, nea_r, dtb_r.T, nea_r.T, d_skip_x, s0_ssd, rev, d)
        y_dir.append(y_d)
        s_ssd_dir.append(t_d.reshape(N_CTX_SEQ, SSD_HEADS, SSD_HEAD_DIM, SSD_STATE))
    new_state_gla = jnp.stack(s_gla_dir, axis=1)[:, None]
    new_state_ssd = jnp.stack(s_ssd_dir, axis=1)[:, None]

    x1 = _out_proj(o_dir[0], o_dir[1], y_dir[0], y_dir[1], proj_main, x_ctx, x_lat, pos, mods,
                   g_gla, g_ssd, g_mix_post[l][None, :], w_out_bf)

    h_ffn, idx, wt_tok, rank, cnt = _router(x1, mods, g_ffn_pre[l][None, :], wr_hi, wr_lo,
                                            b_router[l][:, None])
    i32 = jnp.int32
    e_ids = jnp.arange(N_EXPERTS, dtype=i32)
    counts = cnt[:, 0].astype(i32)
    padded = (counts + MOE_BLK - 1) // MOE_BLK * MOE_BLK
    pad_end = jnp.sum(jnp.where(e_ids[None, :] <= e_ids[:, None], padded[None, :], 0), axis=1)
    pad_start = pad_end - padded
    slot0 = jnp.sum(jnp.where(idx[:, :, None] == e_ids, pad_start, 0), axis=-1)
    dest = slot0 + rank
    n_blk = pad_end[-1] // MOE_BLK
    blk_ids = jnp.arange(N_MOE_BLOCKS, dtype=i32)
    blk_src = jnp.minimum(blk_ids, n_blk - 1)
    blk_expert = jnp.minimum(jnp.sum((pad_end[None, :] <= (blk_src * MOE_BLK)[:, None]).astype(i32), axis=1),
                             N_EXPERTS - 1)
    blk_hot = blk_expert[:, None] == e_ids
    nonempty = counts > 0
    ordinal = jnp.sum(jnp.where((e_ids[None, :] < e_ids[:, None]) & nonempty[None, :], 1, 0), axis=1)
    next_e = jnp.min(jnp.where((e_ids[None, :] > e_ids[:, None]) & nonempty[None, :], e_ids[None, :],
                               N_EXPERTS), axis=1)
    next_e = jnp.where(next_e == N_EXPERTS, -1, next_e)
    blk_first = jnp.concatenate([jnp.ones((1,), i32), (blk_expert[1:] != blk_expert[:-1]).astype(i32)])
    blk_next = jnp.sum(jnp.where(blk_hot, next_e, 0), axis=1).astype(i32)
    blk_slot = (jnp.sum(jnp.where(blk_hot, ordinal, 0), axis=1) % 2).astype(i32)
    n_blk_arr = n_blk.astype(i32)[None]
    x_sorted = _dispatch((pad_start + counts).astype(i32), (padded - counts).astype(i32),
                         n_blk_arr, dest, h_ffn)
    y_sorted = _experts(blk_expert.astype(i32), blk_src.astype(i32), n_blk_arr, blk_first, blk_next,
                        blk_slot, x_sorted, w_exp_gate[l], w_exp_up[l], w_exp_down[l])
    out_ctx, out_lat = _combine(dest, y_sorted, wt_tok, h_ffn, x1, mods, g_ffn_post[l][None, :],
                                w_sh_gate[l].astype(BF16), w_sh_up[l].astype(BF16),
                                w_sh_down[l].astype(BF16))
    return (out_ctx.reshape(N_CTX_SEQ, CTX_LEN, D_MODEL), out_lat.reshape(N_LAT_SEQ, LAT_LEN, D_MODEL),
            new_state_gla, new_state_ssd)
```

```python
import functools

import numpy as np
import jax
import jax.numpy as jnp
from jax import lax
from jax.experimental import pallas as pl
from jax.experimental.pallas import tpu as pltpu

F32 = jnp.float32
BF16 = jnp.bfloat16

D_MODEL = 2048
N_CTX_SEQ = 16
CTX_LEN = 256
N_LAT_SEQ = 2
LAT_LEN = 4096
GRID_W = 64
EPS = 1e-6
N_CTX_TOK = N_CTX_SEQ * CTX_LEN
M_TOK = N_CTX_TOK + N_LAT_SEQ * LAT_LEN
SEQ_LENS = (CTX_LEN,) * N_CTX_SEQ + (LAT_LEN,) * N_LAT_SEQ
ROWS_PER_MOD = 4096
N_MOD = M_TOK // ROWS_PER_MOD

GLA_HEADS = 4
GLA_DK = 128
GLA_DV = 256
GLA_KEY_WIDTH = GLA_HEADS * GLA_DK
GLA_WIDTH = GLA_HEADS * GLA_DV
GLA_GATE_RANK = 16
GLA_GATE_TAU = 16.0
GLA_CHUNK = 64
GLA_SCAN_CHUNK = 2 * GLA_CHUNK
GLA_ROWS = 256

SSD_HEADS = 16
SSD_HEAD_DIM = 64
SSD_GROUPS = 2
SSD_HPG = SSD_HEADS // SSD_GROUPS
SSD_STATE = 128
SSD_WIDTH = SSD_HEADS * SSD_HEAD_DIM
SSD_GROUP_WIDTH = SSD_WIDTH // SSD_GROUPS
SSD_CHUNK = 128
SSD_CONV_CH = SSD_WIDTH + 2 * SSD_GROUPS * SSD_STATE
N_DIR = 2

N_EXPERTS = 256
TOP_K = 8
N_EXPERT_GROUPS = 8
GROUP_SIZE = N_EXPERTS // N_EXPERT_GROUPS
TOPK_GROUPS = 4
EXPERT_FF = 512
ROUTED_SCALE = 2.5

C_GOUT = 0
C_Z = C_GOUT + GLA_WIDTH
C_V = C_Z + SSD_WIDTH
C_XBC = C_V + GLA_WIDTH
C_Q = C_XBC + SSD_CONV_CH
C_K = C_Q + GLA_KEY_WIDTH
MAIN_WIDTH = C_K + GLA_KEY_WIDTH
LANES = 128
SUBLANES = 8
SUBLANES_BF16 = 2 * SUBLANES
SM_LR = 0
SM_DT = N_DIR * GLA_GATE_RANK

TM_PROJ = 256
TN_PROJ = 512
TM_OUT = 256
TM_ROUTE = 256
MOE_BLK = 128
N_PAIRS = M_TOK * TOP_K
N_MOE_BLOCKS = N_PAIRS // MOE_BLK + N_EXPERTS
N_SLOTS = N_MOE_BLOCKS * MOE_BLK
T_DISPATCH = 512
T_COMBINE = 128
ADA_TN = 1024
VMEM_LIMIT = 52 * 1024 * 1024


def _mm(a, b):
    return jnp.dot(a, b, preferred_element_type=F32)


def _mm_nt(a, b):
    return lax.dot_general(a, b, (((1,), (1,)), ((), ())), preferred_element_type=F32)


def _mm_tn(a, b):
    return lax.dot_general(a, b, (((0,), (0,)), ((), ())), preferred_element_type=F32)


def _split2(x):
    hi = x.astype(BF16)
    lo = (x - hi.astype(F32)).astype(BF16)
    return hi, lo


def _split3(x):
    hi = x.astype(BF16)
    r = x - hi.astype(F32)
    mid = r.astype(BF16)
    lo = (r - mid.astype(F32)).astype(BF16)
    return hi, mid, lo


def _mm_x3(a, b):
    a_hi, a_lo = _split2(a)
    b_hi, b_lo = _split2(b)
    return _mm(a_hi, b_hi) + _mm(a_lo, b_hi) + _mm(a_hi, b_lo)


def _mm_sel_left(sel_bf, x):
    hi, mid, lo = _split3(x)
    return _mm(sel_bf, hi) + _mm(sel_bf, mid) + _mm(sel_bf, lo)


def _mm_sel_right(x, sel_bf):
    hi, mid, lo = _split3(x)
    return _mm(hi, sel_bf) + _mm(mid, sel_bf) + _mm(lo, sel_bf)


def _sigmoid(x):
    return 1.0 / (1.0 + jnp.exp(-x))


def _silu(x):
    return x * _sigmoid(x)


def _softplus(x):
    return jnp.maximum(x, 0.0) + jnp.log1p(jnp.exp(-jnp.abs(x)))


def _rms(x, g):
    return x * lax.rsqrt(jnp.mean(x * x, axis=-1, keepdims=True) + EPS) * g


def _params(sem, vmem=VMEM_LIMIT):
    return pltpu.CompilerParams(dimension_semantics=sem, vmem_limit_bytes=vmem)


def _ada_kernel(c_ref, w_ref, b_ref, o_ref):
    o_ref[...] = _mm_x3(_silu(c_ref[...]), w_ref[...]) + b_ref[...]


def _ada_mod(cvecs, w_ada, b_ada):
    n_out = w_ada.shape[1]
    return pl.pallas_call(
        _ada_kernel,
        out_shape=jax.ShapeDtypeStruct((SUBLANES, n_out), F32),
        grid=(n_out // ADA_TN,),
        in_specs=[pl.BlockSpec((SUBLANES, D_MODEL), lambda j: (0, 0)),
                  pl.BlockSpec((D_MODEL, ADA_TN), lambda j: (0, j)),
                  pl.BlockSpec((1, ADA_TN), lambda j: (0, j))],
        out_specs=pl.BlockSpec((SUBLANES, ADA_TN), lambda j: (0, j)),
        compiler_params=_params(("arbitrary",)),
        name="ada_mod",
    )(cvecs, w_ada, b_ada)


def _token_specs(tm, buffers=2):
    n_ctx = N_CTX_TOK // tm
    n_pos = LAT_LEN // tm
    mode = dict(pipeline_mode=pl.Buffered(buffers)) if buffers != 2 else {}
    ctx = pl.BlockSpec((tm, D_MODEL), lambda i, *_: (jnp.minimum(i, n_ctx - 1), 0), **mode)
    lat = pl.BlockSpec((tm, D_MODEL), lambda i, *_: (jnp.maximum(i - n_ctx, 0), 0), **mode)
    pos = pl.BlockSpec((tm, D_MODEL), lambda i, *_: (jnp.maximum(i - n_ctx, 0) % n_pos, 0), **mode)
    return n_ctx, [ctx, lat, pos]


def _inproj_kernel(xc_ref, xl_ref, pos_ref, mod_ref, g_ref, w_ref, ws_ref, o_ref, os_ref, h_scr, *, n_ctx):
    def prologue(x):
        h = _rms(x, g_ref[...]) * (1.0 + mod_ref[0, 1:2, :]) + mod_ref[0, 0:1, :]
        h_hi, h_lo = _split2(h)
        h_scr[...] = h_hi
        ws_hi, ws_lo = _split2(ws_ref[...])
        os_ref[...] = _mm(h_hi, ws_hi) + _mm(h_lo, ws_hi) + _mm(h_hi, ws_lo)

    is_ctx = pl.program_id(0) < n_ctx

    @pl.when(is_ctx)
    def _():
        prologue(xc_ref[...])

    @pl.when(jnp.logical_not(is_ctx))
    def _():
        prologue(xl_ref[...] + pos_ref[...])

    for j in range(MAIN_WIDTH // TN_PROJ):
        cols = slice(j * TN_PROJ, (j + 1) * TN_PROJ)
        o_ref[:, cols] = _mm(h_scr[...], w_ref[:, cols]).astype(BF16)


def _in_proj(x_ctx, x_lat, pos, mods, g_pre, w_main, w_small):
    tm = TM_PROJ
    n_ctx, tok_specs = _token_specs(tm)
    const = lambda i: (0, 0)
    once = dict(pipeline_mode=pl.Buffered(1))
    return pl.pallas_call(
        functools.partial(_inproj_kernel, n_ctx=n_ctx),
        out_shape=(jax.ShapeDtypeStruct((M_TOK, MAIN_WIDTH), BF16),
                   jax.ShapeDtypeStruct((M_TOK, LANES), F32)),
        grid=(M_TOK // tm,),
        in_specs=tok_specs + [
                  pl.BlockSpec((1, SUBLANES, D_MODEL), lambda i: (i * tm // ROWS_PER_MOD, 0, 0)),
                  pl.BlockSpec((1, D_MODEL), const),
                  pl.BlockSpec((D_MODEL, MAIN_WIDTH), const, **once),
                  pl.BlockSpec((D_MODEL, LANES), const, **once)],
        out_specs=(pl.BlockSpec((tm, MAIN_WIDTH), lambda i: (i, 0)),
                   pl.BlockSpec((tm, LANES), lambda i: (i, 0))),
        scratch_shapes=[pltpu.VMEM((tm, D_MODEL), BF16)],
        compiler_params=_params(("arbitrary",)),
        name="in_proj",
    )(x_ctx, x_lat, pos, mods, g_pre, w_main, w_small)


def _scan_schedule(rows_per_step, reverse):
    blk, flag, s0i, soi, emit, has_prev, has_next = [], [], [], [], [], [], []
    start = 0
    for s, length in enumerate(SEQ_LENS):
        nb = length // rows_per_step
        is_ctx = s < N_CTX_SEQ
        order = range(nb - 1, -1, -1) if reverse else range(nb)
        for n, b in enumerate(order):
            blk.append(start + b)
            flag.append((1 if is_ctx else 2) if n == 0 else 0)
            s0i.append(0 if is_ctx else s - N_CTX_SEQ)
            soi.append(s if is_ctx else N_CTX_SEQ - 1)
            emit.append(1 if (is_ctx and n == nb - 1) else 0)
            has_prev.append(1 if b > 0 else 0)
            has_next.append(1 if b < nb - 1 else 0)
        start += nb
    return tuple(np.array(a, np.int32) for a in (blk, flag, s0i, soi, emit, has_prev, has_next))


def _gla_kernel(blkf_ref, blkb_ref, flag_ref, s0i_ref, soi_ref, emit_ref, *refs):
    n_in = 7
    ins = (refs[:n_in], refs[n_in:2 * n_in])
    outs = (refs[2 * n_in:2 * n_in + 2], refs[2 * n_in + 2:2 * n_in + 4])
    st_scr = refs[2 * n_in + 4]
    i = pl.program_id(0)
    flag = flag_ref[i]

    @pl.when(flag == 1)
    def _():
        st_scr[...] = jnp.zeros_like(st_scr)

    @pl.when(flag == 2)
    def _():
        for d in range(N_DIR):
            for h in range(GLA_HEADS):
                st_scr[d, h] = ins[d][6][0, h].T

    c = GLA_SCAN_CHUNK
    r_id = lax.broadcasted_iota(jnp.int32, (c, c), 0)
    c_id = lax.broadcasted_iota(jnp.int32, (c, c), 1)
    tri = (c_id <= r_id, c_id >= r_id)
    tri_bf = tuple(jnp.where(m, 1.0, 0.0).astype(BF16) for m in tri)

    log_a = []
    for d in range(N_DIR):
        sm_ref, wgk_ref, bgk_ref = ins[d][3:6]
        gk = _mm_x3(sm_ref[...], wgk_ref[...]) + bgk_ref[...]
        log_a.append((jnp.minimum(gk, 0.0) - jnp.log1p(jnp.exp(-jnp.abs(gk)))) * (1.0 / GLA_GATE_TAU))

    n_chunks = GLA_ROWS // c
    for step in range(n_chunks):
        for d in range(N_DIR):
            reverse = d == 1
            q_ref, k_ref, v_ref = ins[d][:3]
            o_ref = outs[d][0]
            lo = (n_chunks - 1 - step if reverse else step) * c
            b_all = _mm_sel_left(tri_bf[d], log_a[d][lo:lo + c])
            for h in range(GLA_HEADS):
                kc = slice(h * GLA_DK, (h + 1) * GLA_DK)
                vc = slice(h * GLA_DV, (h + 1) * GLA_DV)
                b = b_all[:, kc]
                b_end = b[0:1] if reverse else b[c - 1:c]
                mid = c // 2 if reverse else c // 2 - 1
                b_mid = b[mid:mid + 1]
                q = q_ref[lo:lo + c, kc].astype(F32) * (GLA_DK ** -0.5)
                k = k_ref[lo:lo + c, kc].astype(F32)
                v = v_ref[lo:lo + c, vc]
                q_m = (q * jnp.exp(b - b_mid)).astype(BF16)
                k_m = (k * jnp.exp(b_mid - b)).astype(BF16)
                q_e = (q * jnp.exp(b)).astype(BF16)
                att = jnp.where(tri[d], _mm_nt(q_m, k_m), 0.0).astype(BF16)
                st = st_scr[d, h]
                o_ref[lo:lo + c, vc] = (_mm(att, v) + _mm_nt(q_e, st.astype(BF16))).astype(BF16)
                k_end = (k * jnp.exp(b_end - b)).astype(BF16)
                st_scr[d, h] = st * jnp.exp(b_end) + _mm_tn(v, k_end)

    @pl.when(emit_ref[i] == 1)
    def _():
        for d in range(N_DIR):
            for h in range(GLA_HEADS):
                outs[d][1][0, h] = st_scr[d, h].T


def _gla_scan(proj_main, proj_small, wgk_pads, bgks, s0s):
    sched_f = _scan_schedule(GLA_ROWS, False)
    sched_b = _scan_schedule(GLA_ROWS, True)
    for a, b in zip(sched_f[1:5], sched_b[1:5]):
        assert np.array_equal(a, b)
    n_steps = M_TOK // GLA_ROWS
    t = GLA_ROWS
    q_blk, k_blk, v_blk = C_Q // GLA_KEY_WIDTH, C_K // GLA_KEY_WIDTH, C_V // GLA_WIDTH
    state_blk = (1, GLA_HEADS, GLA_DK, GLA_DV)

    def dir_in_specs(d):
        blk_of = lambda refs: refs[d]
        return [
            pl.BlockSpec((t, GLA_KEY_WIDTH), lambda i, *r: (blk_of(r)[i], q_blk)),
            pl.BlockSpec((t, GLA_KEY_WIDTH), lambda i, *r: (blk_of(r)[i], k_blk)),
            pl.BlockSpec((t, GLA_WIDTH), lambda i, *r: (blk_of(r)[i], v_blk)),
            pl.BlockSpec((t, LANES), lambda i, *r: (blk_of(r)[i], 0)),
            pl.BlockSpec((LANES, GLA_KEY_WIDTH), lambda i, *r: (0, 0)),
            pl.BlockSpec((1, GLA_KEY_WIDTH), lambda i, *r: (0, 0)),
            pl.BlockSpec(state_blk, lambda i, *r: (r[3][i], 0, 0, 0)),
        ]

    def dir_out_specs(d):
        blk_of = lambda refs: refs[d]
        return [pl.BlockSpec((t, GLA_WIDTH), lambda i, *r: (blk_of(r)[i], 0)),
                pl.BlockSpec(state_blk, lambda i, *r: (r[4][i], 0, 0, 0))]

    dir_out_shape = [jax.ShapeDtypeStruct((M_TOK, GLA_WIDTH), BF16),
                     jax.ShapeDtypeStruct((N_CTX_SEQ,) + state_blk[1:], F32)]
    operands = []
    for d in range(N_DIR):
        operands += [proj_main, proj_main, proj_main, proj_small, wgk_pads[d], bgks[d], s0s[d]]
    o_f, s_f, o_b, s_b = pl.pallas_call(
        _gla_kernel,
        out_shape=tuple(dir_out_shape * N_DIR),
        grid_spec=pltpu.PrefetchScalarGridSpec(
            num_scalar_prefetch=6,
            grid=(n_steps,),
            in_specs=dir_in_specs(0) + dir_in_specs(1),
            out_specs=tuple(dir_out_specs(0) + dir_out_specs(1)),
            scratch_shapes=[pltpu.VMEM((N_DIR, GLA_HEADS, GLA_DV, GLA_DK), F32)]),
        compiler_params=_params(("arbitrary",)),
        name="gla_scan",
    )(*(jnp.asarray(a) for a in (sched_f[0], sched_b[0]) + sched_f[1:5]), *operands)
    return (o_f, s_f), (o_b, s_b)


def _ssd_kernel(blk_ref, flag_ref, s0i_ref, soi_ref, emit_ref, hp_ref, hn_ref,
                xbc_ref, xprev_ref, xnext_ref, sm_ref, cw_ref, cb_ref,
                dtb_r_ref, nea_r_ref, dtb_c_ref, nea_c_ref, dsk_ref, expand_ref, s0_ref,
                y_ref, so_ref, st_scr, *, reverse, lane0, add_skip):
    i = pl.program_id(0)
    flag = flag_ref[i]
    t = SSD_CHUNK

    @pl.when(flag == 1)
    def _():
        st_scr[...] = jnp.zeros_like(st_scr)

    @pl.when(flag == 2)
    def _():
        for g in range(SSD_GROUPS):
            st_scr[g] = s0_ref[0, g].T

    xbc = xbc_ref[...].astype(F32)
    last = SUBLANES_BF16 - 1
    prev = jnp.where(hp_ref[i] == 1, xprev_ref[last:last + 1, :].astype(F32), 0.0)
    nxt = jnp.where(hn_ref[i] == 1, xnext_ref[0:1, :].astype(F32), 0.0)
    row = lax.broadcasted_iota(jnp.int32, xbc.shape, 0)
    x_m1 = jnp.where(row == 0, prev, pltpu.roll(xbc, 1, 0))
    x_p1 = jnp.where(row == t - 1, nxt, pltpu.roll(xbc, t - 1, 0))
    act = _silu(x_m1 * cw_ref[0:1, :] + xbc * cw_ref[1:2, :] + x_p1 * cw_ref[2:3, :] + cb_ref[...])
    xs = act[:, :SSD_WIDTH]
    bm = act[:, SSD_WIDTH:SSD_WIDTH + SSD_GROUPS * SSD_STATE]
    cm = act[:, SSD_WIDTH + SSD_GROUPS * SSD_STATE:]

    sm = sm_ref[...]
    dt = _softplus(sm + dtb_r_ref[...])
    a = dt * nea_r_ref[...]
    a_t = _softplus(sm.T + dtb_c_ref[...]) * nea_c_ref[...]

    r_id = lax.broadcasted_iota(jnp.int32, (t, t), 0)
    c_id = lax.broadcasted_iota(jnp.int32, (t, t), 1)
    tri = (c_id >= r_id) if reverse else (c_id <= r_id)
    tri_bf = jnp.where(tri, 1.0, 0.0).astype(BF16)
    tri_t_bf = jnp.where((r_id >= c_id) if reverse else (r_id <= c_id), 1.0, 0.0).astype(BF16)
    cs = _mm_sel_left(tri_bf, a)
    cs_t = _mm_sel_right(a_t, tri_t_bf)

    expand = expand_ref[...]
    dt_x = _mm_sel_right(dt, expand)
    cs_x = _mm_sel_right(cs, expand)
    cs_end_x = cs_x[0:1] if reverse else cs_x[t - 1:t]
    x_in = xs * dt_x
    x_bf = x_in.astype(BF16)
    x_w = (x_in * jnp.exp(cs_end_x - cs_x)).astype(BF16)
    decay_in = jnp.exp(cs_x)
    decay_end = jnp.exp(cs_end_x)

    for g in range(SSD_GROUPS):
        gc = slice(g * SSD_STATE, (g + 1) * SSD_STATE)
        gw = slice(g * SSD_GROUP_WIDTH, (g + 1) * SSD_GROUP_WIDTH)
        c_g = cm[:, gc].astype(BF16)
        b_g = bm[:, gc].astype(BF16)
        cb = _mm_nt(c_g, b_g)
        st = st_scr[g]
        y_state = _mm(c_g, st.astype(BF16)) * decay_in[:, gw]
        for r in range(SSD_HPG):
            hh = g * SSD_HPG + r
            hc = slice(hh * SSD_HEAD_DIM, (hh + 1) * SSD_HEAD_DIM)
            lane = lane0 + hh
            seg = cs[:, lane:lane + 1] - cs_t[lane:lane + 1, :]
            lm = jnp.exp(jnp.where(tri, seg, -jnp.inf))
            y_h = _mm((cb * lm).astype(BF16), x_bf[:, hc]) + y_state[:, r * SSD_HEAD_DIM:(r + 1) * SSD_HEAD_DIM]
            if add_skip:
                y_h = y_h + xs[:, hc] * dsk_ref[:, hc]
            y_ref[:, hc] = y_h.astype(BF16)
        st_scr[g] = st * decay_end[:, gw] + _mm_tn(b_g, x_w[:, gw])

    @pl.when(emit_ref[i] == 1)
    def _():
        for g in range(SSD_GROUPS):
            so_ref[0, g] = st_scr[g].T


def _ssd_scan(proj_main, proj_small, conv_w, conv_b, dtb_r, nea_r, dtb_c, nea_c, d_skip_x, s0,
              reverse, direction):
    sched = tuple(jnp.asarray(a) for a in _scan_schedule(SSD_CHUNK, reverse))
    t = SSD_CHUNK
    n_steps = M_TOK // t
    xbc_blk = C_XBC // SSD_CONV_CH
    halo = SUBLANES_BF16
    rb = t // halo
    n_rb = M_TOK // halo
    state_blk = (1, SSD_GROUPS, SSD_GROUP_WIDTH, SSD_STATE)
    lane0 = SM_DT + direction * SSD_HEADS
    expand = (np.arange(LANES)[:, None] - lane0 == np.arange(SSD_WIDTH)[None, :] // SSD_HEAD_DIM)
    expand = jnp.asarray(expand.astype(np.float32)).astype(BF16)
    return pl.pallas_call(
        functools.partial(_ssd_kernel, reverse=reverse, lane0=lane0,
                          add_skip=not reverse),
        out_shape=(jax.ShapeDtypeStruct((M_TOK, SSD_WIDTH), BF16),
                   jax.ShapeDtypeStruct((N_CTX_SEQ,) + state_blk[1:], F32)),
        grid_spec=pltpu.PrefetchScalarGridSpec(
            num_scalar_prefetch=7,
            grid=(n_steps,),
            in_specs=[
                pl.BlockSpec((t, SSD_CONV_CH), lambda i, blk, *_: (blk[i], xbc_blk)),
                pl.BlockSpec((halo, SSD_CONV_CH),
                             lambda i, blk, *_: (jnp.maximum(blk[i] * rb - 1, 0), xbc_blk)),
                pl.BlockSpec((halo, SSD_CONV_CH),
                             lambda i, blk, *_: (jnp.minimum((blk[i] + 1) * rb, n_rb - 1), xbc_blk)),
                pl.BlockSpec((t, LANES), lambda i, blk, *_: (blk[i], 0)),
                pl.BlockSpec((SUBLANES, SSD_CONV_CH), lambda i, *_: (0, 0)),
                pl.BlockSpec((1, SSD_CONV_CH), lambda i, *_: (0, 0)),
                pl.BlockSpec((1, LANES), lambda i, *_: (0, 0)),
                pl.BlockSpec((1, LANES), lambda i, *_: (0, 0)),
                pl.BlockSpec((LANES, 1), lambda i, *_: (0, 0)),
                pl.BlockSpec((LANES, 1), lambda i, *_: (0, 0)),
                pl.BlockSpec((1, SSD_WIDTH), lambda i, *_: (0, 0)),
                pl.BlockSpec((LANES, SSD_WIDTH), lambda i, *_: (0, 0)),
                pl.BlockSpec(state_blk, lambda i, blk, flag, s0i, *_: (s0i[i], 0, 0, 0)),
            ],
            out_specs=(
                pl.BlockSpec((t, SSD_WIDTH), lambda i, blk, *_: (blk[i], 0)),
                pl.BlockSpec(state_blk, lambda i, blk, flag, s0i, soi, *_: (soi[i], 0, 0, 0)),
            ),
            scratch_shapes=[pltpu.VMEM((SSD_GROUPS, SSD_STATE, SSD_GROUP_WIDTH), F32)]),
        compiler_params=_params(("arbitrary",)),
        name="ssd_bwd" if reverse else "ssd_fwd",
    )(*sched, proj_main, proj_main, proj_main, proj_small, conv_w, conv_b,
      dtb_r, nea_r, dtb_c, nea_c, d_skip_x, expand, s0)


def _outproj_kernel(of_ref, ob_ref, gout_ref, yf_ref, yb_ref, z_ref, xc_ref, xl_ref, pos_ref, mod_ref,
                    ggla_ref, gssd_ref, gpost_ref, w_ref, o_ref, *, n_ctx):
    o = of_ref[...].astype(F32) + ob_ref[...].astype(F32)
    gate = _silu(gout_ref[...].astype(F32))
    parts = []
    for h in range(GLA_HEADS):
        hc = slice(h * GLA_DV, (h + 1) * GLA_DV)
        parts.append((_rms(o[:, hc], ggla_ref[...]) * gate[:, hc]).astype(BF16))
    y = (yf_ref[...].astype(F32) + yb_ref[...].astype(F32)) * _silu(z_ref[...].astype(F32))
    for g in range(SSD_GROUPS):
        gw = slice(g * SSD_GROUP_WIDTH, (g + 1) * SSD_GROUP_WIDTH)
        parts.append(_rms(y[:, gw], gssd_ref[:, gw]).astype(BF16))
    acc = None
    col = 0
    for p in parts:
        term = _mm(p, w_ref[col:col + p.shape[1], :])
        acc = term if acc is None else acc + term
        col += p.shape[1]
    delta = mod_ref[0, 2:3, :] * _rms(acc, gpost_ref[...])
    is_ctx = pl.program_id(0) < n_ctx

    @pl.when(is_ctx)
    def _():
        o_ref[...] = xc_ref[...] + delta

    @pl.when(jnp.logical_not(is_ctx))
    def _():
        o_ref[...] = xl_ref[...] + pos_ref[...] + delta


def _out_proj(o_f, o_b, y_f, y_b, proj_main, x_ctx, x_lat, pos, mods, g_gla, g_ssd, g_post, w_out):
    tm = TM_OUT
    row = lambda i: (i, 0)
    const = lambda i: (0, 0)
    n_ctx, tok_specs = _token_specs(tm)
    return pl.pallas_call(
        functools.partial(_outproj_kernel, n_ctx=n_ctx),
        out_shape=jax.ShapeDtypeStruct((M_TOK, D_MODEL), F32),
        grid=(M_TOK // tm,),
        in_specs=[pl.BlockSpec((tm, GLA_WIDTH), row),
                  pl.BlockSpec((tm, GLA_WIDTH), row),
                  pl.BlockSpec((tm, GLA_WIDTH), lambda i: (i, C_GOUT // GLA_WIDTH)),
                  pl.BlockSpec((tm, SSD_WIDTH), row),
                  pl.BlockSpec((tm, SSD_WIDTH), row),
                  pl.BlockSpec((tm, SSD_WIDTH), lambda i: (i, C_Z // SSD_WIDTH))] + tok_specs + [
                  pl.BlockSpec((1, SUBLANES, D_MODEL), lambda i: (i * tm // ROWS_PER_MOD, 0, 0)),
                  pl.BlockSpec((1, GLA_DV), const),
                  pl.BlockSpec((1, SSD_WIDTH), const),
                  pl.BlockSpec((1, D_MODEL), const),
                  pl.BlockSpec((D_MODEL, D_MODEL), const)],
        out_specs=pl.BlockSpec((tm, D_MODEL), row),
        compiler_params=_params(("arbitrary",)),
        name="out_proj",
    )(o_f, o_b, proj_main, y_f, y_b, proj_main, x_ctx, x_lat, pos, mods, g_gla, g_ssd, g_post, w_out)


def _router_kernel(x_ref, mod_ref, g_ref, wr_hi_ref, wr_lo_ref, br_ref,
                   h_ref, idx_ref, wt_ref, rank_ref, cnt_ref, cnt_scr):
    i = pl.program_id(0)
    tm = TM_ROUTE

    @pl.when(i == 0)
    def _():
        cnt_scr[...] = jnp.zeros_like(cnt_scr)

    h = _rms(x_ref[...], g_ref[...]) * (1.0 + mod_ref[0, 4:5, :]) + mod_ref[0, 3:4, :]
    h_ref[...] = h
    h_hi, h_lo = _split2(h)
    wr_hi = wr_hi_ref[...]
    logits = _mm_nt(wr_hi, h_hi) + _mm_nt(wr_hi, h_lo) + _mm_nt(wr_lo_ref[...], h_hi)
    scores = _sigmoid(logits)
    sel = scores + br_ref[...]
    neg = -jnp.inf

    def first_argmax(x, ids, n):
        m = jnp.max(x, axis=0, keepdims=True)
        return m, jnp.min(jnp.where(x == m, ids, float(n)), axis=0, keepdims=True)

    ids_g = lax.broadcasted_iota(jnp.int32, (GROUP_SIZE, tm), 0).astype(F32)
    grp = []
    for g in range(N_EXPERT_GROUPS):
        xg = sel[g * GROUP_SIZE:(g + 1) * GROUP_SIZE]
        m1, a1 = first_argmax(xg, ids_g, GROUP_SIZE)
        m2 = jnp.max(jnp.where(ids_g == a1, neg, xg), axis=0, keepdims=True)
        grp.append(m1 + m2)
    gsc = jnp.concatenate(grp, axis=0)
    ids_8 = lax.broadcasted_iota(jnp.int32, (N_EXPERT_GROUPS, tm), 0).astype(F32)
    keep = jnp.zeros((N_EXPERT_GROUPS, tm), F32)
    for _ in range(TOPK_GROUPS):
        _, a = first_argmax(gsc, ids_8, N_EXPERT_GROUPS)
        pick = ids_8 == a
        keep = jnp.where(pick, 1.0, keep)
        gsc = jnp.where(pick, neg, gsc)
    selm = jnp.concatenate(
        [jnp.where(keep[g:g + 1] > 0.5, sel[g * GROUP_SIZE:(g + 1) * GROUP_SIZE], neg)
         for g in range(N_EXPERT_GROUPS)], axis=0)

    ids_e = lax.broadcasted_iota(jnp.int32, (N_EXPERTS, tm), 0).astype(F32)
    picks, wts = [], []
    chosen = jnp.zeros((N_EXPERTS, tm), F32)
    for _ in range(TOP_K):
        _, a = first_argmax(selm, ids_e, N_EXPERTS)
        hit = ids_e == a
        picks.append(a)
        wts.append(jnp.sum(jnp.where(hit, scores, 0.0), axis=0, keepdims=True))
        chosen = jnp.where(hit, 1.0, chosen)
        selm = jnp.where(hit, neg, selm)
    w = jnp.concatenate(wts, axis=0)
    w = w / jnp.sum(w, axis=0, keepdims=True) * ROUTED_SCALE
    idx_ref[...] = jnp.concatenate(picks, axis=0).astype(jnp.int32)
    wt_ref[...] = jnp.concatenate([w, jnp.zeros((LANES - TOP_K, tm), F32)], axis=0).T

    t_r = lax.broadcasted_iota(jnp.int32, (tm, tm), 0)
    t_c = lax.broadcasted_iota(jnp.int32, (tm, tm), 1)
    before = jnp.where(t_r < t_c, 1.0, 0.0).astype(BF16)
    base = _mm(chosen.astype(BF16), before) + cnt_scr[...]
    rank_ref[...] = jnp.concatenate(
        [jnp.sum(jnp.where(ids_e == a, base, 0.0), axis=0, keepdims=True) for a in picks],
        axis=0).astype(jnp.int32)
    cnt_scr[...] = cnt_scr[...] + jnp.sum(chosen, axis=1, keepdims=True)
    cnt_ref[...] = jnp.broadcast_to(cnt_scr[...], cnt_ref.shape)


def _router(x1, mods, g_pre, wr_hi, wr_lo, b_router):
    tm = TM_ROUTE
    const = lambda i: (0, 0)
    return pl.pallas_call(
        _router_kernel,
        out_shape=(jax.ShapeDtypeStruct((M_TOK, D_MODEL), F32),
                   jax.ShapeDtypeStruct((TOP_K, M_TOK), jnp.int32),
                   jax.ShapeDtypeStruct((M_TOK, LANES), F32),
                   jax.ShapeDtypeStruct((TOP_K, M_TOK), jnp.int32),
                   jax.ShapeDtypeStruct((N_EXPERTS, LANES), F32)),
        grid=(M_TOK // tm,),
        in_specs=[pl.BlockSpec((tm, D_MODEL), lambda i: (i, 0)),
                  pl.BlockSpec((1, SUBLANES, D_MODEL), lambda i: (i * tm // ROWS_PER_MOD, 0, 0)),
                  pl.BlockSpec((1, D_MODEL), const),
                  pl.BlockSpec((N_EXPERTS, D_MODEL), const),
                  pl.BlockSpec((N_EXPERTS, D_MODEL), const),
                  pl.BlockSpec((N_EXPERTS, 1), const)],
        out_specs=(pl.BlockSpec((tm, D_MODEL), lambda i: (i, 0)),
                   pl.BlockSpec((TOP_K, tm), lambda i: (0, i)),
                   pl.BlockSpec((tm, LANES), lambda i: (i, 0)),
                   pl.BlockSpec((TOP_K, tm), lambda i: (0, i)),
                   pl.BlockSpec((N_EXPERTS, LANES), const)),
        scratch_shapes=[pltpu.VMEM((N_EXPERTS, 1), F32)],
        compiler_params=_params(("arbitrary",)),
        name="router",
    )(x1, mods, g_pre, wr_hi, wr_lo, b_router)


def _dispatch_kernel(fill_start_ref, fill_len_ref, nb_ref, dest_ref, h_ref, xs_hbm, zero_scr, sem, zsem):
    @pl.when(pl.program_id(0) == 0)
    def _():
        zero_scr[...] = jnp.zeros_like(zero_scr)

        def for_each_fill(act):
            def pad_body(e, carry):
                start = fill_start_ref[e]
                length = fill_len_ref[e]
                head = jnp.minimum((-start) & (SUBLANES - 1), length)
                for j in range(SUBLANES - 1):
                    @pl.when(j < head)
                    def _():
                        act(pltpu.make_async_copy(zero_scr.at[pl.ds(0, 1)],
                                                  xs_hbm.at[pl.ds(start + j, 1)], zsem))
                body_start = start + head
                body_len = length - head
                for bit in (64, 32, 16, 8):
                    @pl.when((body_len & bit) != 0)
                    def _():
                        off = pl.multiple_of(body_start + (body_len & jnp.int32(~(2 * bit - 1))), SUBLANES)
                        act(pltpu.make_async_copy(zero_scr.at[pl.ds(0, bit)],
                                                  xs_hbm.at[pl.ds(off, bit)], zsem))
                return carry

            def tail_body(b, carry):
                act(pltpu.make_async_copy(zero_scr, xs_hbm.at[pl.ds(b * MOE_BLK, MOE_BLK)], zsem))
                return carry

            lax.fori_loop(0, N_EXPERTS, pad_body, 0)
            lax.fori_loop(nb_ref[0], N_MOE_BLOCKS, tail_body, 0)

        for_each_fill(lambda cp: cp.start())
        for_each_fill(lambda cp: cp.wait())

    def body(t, carry):
        for k in range(TOP_K):
            pltpu.make_async_copy(h_ref.at[pl.ds(t, 1)],
                                  xs_hbm.at[pl.ds(dest_ref[k, t], 1)], sem).start()
        return carry

    lax.fori_loop(0, T_DISPATCH, body, 0)
    pltpu.make_async_copy(xs_hbm.at[pl.ds(0, T_DISPATCH * TOP_K)],
                          xs_hbm.at[pl.ds(0, T_DISPATCH * TOP_K)], sem).wait()


def _dispatch(fill_start, fill_len, n_blk, dest, h):
    return pl.pallas_call(
        _dispatch_kernel,
        out_shape=jax.ShapeDtypeStruct((N_SLOTS, D_MODEL), F32),
        grid_spec=pltpu.PrefetchScalarGridSpec(
            num_scalar_prefetch=3,
            grid=(M_TOK // T_DISPATCH,),
            in_specs=[pl.BlockSpec((TOP_K, T_DISPATCH), lambda i, *_: (0, i), memory_space=pltpu.SMEM),
                      pl.BlockSpec((T_DISPATCH, D_MODEL), lambda i, *_: (i, 0))],
            out_specs=pl.BlockSpec(memory_space=pl.ANY),
            scratch_shapes=[pltpu.VMEM((MOE_BLK, D_MODEL), F32),
                            pltpu.SemaphoreType.DMA, pltpu.SemaphoreType.DMA]),
        compiler_params=_params(("arbitrary",)),
        name="moe_dispatch",
    )(fill_start, fill_len, n_blk, dest, h)


def _expert_weight_copies(e, s, w_hbm, w_f32, sem):
    return [pltpu.make_async_copy(w_hbm[j].at[e], w_f32[j].at[s], sem.at[s, j]) for j in range(3)]


def _expert_kernel(be_ref, bs_ref, nb_ref, first_ref, next_ref, slot_ref,
                   x_ref, wg_hbm, wu_hbm, wd_hbm, y_ref,
                   wg_f32, wu_f32, wd_f32, wg_scr, wu_scr, wd_scr, sem):
    b = pl.program_id(0)
    w_hbm = (wg_hbm, wu_hbm, wd_hbm)
    w_f32 = (wg_f32, wu_f32, wd_f32)

    @pl.when(b == 0)
    def _():
        for cp in _expert_weight_copies(be_ref[0], 0, w_hbm, w_f32, sem):
            cp.start()

    @pl.when(b < nb_ref[0])
    def _():
        @pl.when(first_ref[b] == 1)
        def _():
            s = slot_ref[b]
            for cp in _expert_weight_copies(be_ref[b], s, w_hbm, w_f32, sem):
                cp.wait()

            @pl.when(next_ref[b] >= 0)
            def _():
                for cp in _expert_weight_copies(next_ref[b], 1 - s, w_hbm, w_f32, sem):
                    cp.start()

            wg_scr[...] = wg_f32[s].astype(BF16)
            wu_scr[...] = wu_f32[s].astype(BF16)
            wd_scr[...] = wd_f32[s].astype(BF16)

        x = x_ref[...].astype(BF16)
        act = _silu(_mm(x, wg_scr[...])) * _mm(x, wu_scr[...])
        y_ref[...] = _mm(act.astype(BF16), wd_scr[...])

    @pl.when(b >= nb_ref[0])
    def _():
        y_ref[...] = jnp.zeros_like(y_ref)


def _experts(blk_expert, blk_src, n_blk, blk_first, blk_next, blk_slot, x_sorted, w_g, w_u, w_d):
    hbm = pl.BlockSpec(memory_space=pl.ANY)
    return pl.pallas_call(
        _expert_kernel,
        out_shape=jax.ShapeDtypeStruct((N_SLOTS, D_MODEL), F32),
        grid_spec=pltpu.PrefetchScalarGridSpec(
            num_scalar_prefetch=6,
            grid=(N_MOE_BLOCKS,),
            in_specs=[pl.BlockSpec((MOE_BLK, D_MODEL), lambda b, be, bs, *_: (bs[b], 0)), hbm, hbm, hbm],
            out_specs=pl.BlockSpec((MOE_BLK, D_MODEL), lambda b, *_: (b, 0)),
            scratch_shapes=[pltpu.VMEM((2, D_MODEL, EXPERT_FF), F32),
                            pltpu.VMEM((2, D_MODEL, EXPERT_FF), F32),
                            pltpu.VMEM((2, EXPERT_FF, D_MODEL), F32),
                            pltpu.VMEM((D_MODEL, EXPERT_FF), BF16),
                            pltpu.VMEM((D_MODEL, EXPERT_FF), BF16),
                            pltpu.VMEM((EXPERT_FF, D_MODEL), BF16),
                            pltpu.SemaphoreType.DMA((2, 3))]),
        compiler_params=_params(("arbitrary",)),
        name="moe_experts",
    )(blk_expert, blk_src, n_blk, blk_first, blk_next, blk_slot, x_sorted, w_g, w_u, w_d)


def _combine_copy(y_hbm, buf, sem, slot, k, t, src_row):
    return pltpu.make_async_copy(y_hbm.at[pl.ds(src_row, 1)], buf.at[slot, k, pl.ds(t, 1)], sem.at[slot])


def _combine_kernel(dest_ref, dest_next_ref, y_hbm, wt_ref, h_ref, x_ref, mod_ref, g_ref,
                    wsg_ref, wsu_ref, wsd_ref, oc_ref, ol_ref, buf, sem, *, n_ctx):
    i = pl.program_id(0)
    n = pl.num_programs(0)
    slot = i % 2

    def issue(d_ref, s):
        def body(t, carry):
            for k in range(TOP_K):
                _combine_copy(y_hbm, buf, sem, s, k, t, d_ref[k, t]).start()
            return carry
        lax.fori_loop(0, T_COMBINE, body, 0)

    @pl.when(i == 0)
    def _():
        issue(dest_ref, 0)

    for s in range(2):
        @pl.when((i + 1 < n) & (1 - slot == s))
        def _():
            issue(dest_next_ref, s)

    pltpu.make_async_copy(buf.at[slot], buf.at[slot], sem.at[slot]).wait()

    wt = wt_ref[...]
    acc = buf[slot, 0] * wt[:, 0:1]
    for k in range(1, TOP_K):
        acc = acc + buf[slot, k] * wt[:, k:k + 1]
    h = h_ref[...].astype(BF16)
    act = _silu(_mm(h, wsg_ref[...])) * _mm(h, wsu_ref[...])
    y = acc + _mm(act.astype(BF16), wsd_ref[...])
    out = x_ref[...] + mod_ref[0, 5:6, :] * _rms(y, g_ref[...])

    @pl.when(i < n_ctx)
    def _():
        oc_ref[...] = out

    @pl.when(i >= n_ctx)
    def _():
        ol_ref[...] = out


def _combine(dest, y_sorted, wt_tok, h, x1, mods, g_post, ws_g, ws_u, ws_d):
    tc = T_COMBINE
    n = M_TOK // tc
    n_ctx = N_CTX_TOK // tc
    row = lambda i: (i, 0)
    const = lambda i: (0, 0)
    return pl.pallas_call(
        functools.partial(_combine_kernel, n_ctx=n_ctx),
        out_shape=(jax.ShapeDtypeStruct((N_CTX_TOK, D_MODEL), F32),
                   jax.ShapeDtypeStruct((M_TOK - N_CTX_TOK, D_MODEL), F32)),
        grid=(n,),
        in_specs=[pl.BlockSpec((TOP_K, tc), lambda i: (0, i), memory_space=pltpu.SMEM),
                  pl.BlockSpec((TOP_K, tc), lambda i: (0, jnp.minimum(i + 1, n - 1)),
                               memory_space=pltpu.SMEM),
                  pl.BlockSpec(memory_space=pl.ANY),
                  pl.BlockSpec((tc, LANES), row),
                  pl.BlockSpec((tc, D_MODEL), row),
                  pl.BlockSpec((tc, D_MODEL), row),
                  pl.BlockSpec((1, SUBLANES, D_MODEL), lambda i: (i * tc // ROWS_PER_MOD, 0, 0)),
                  pl.BlockSpec((1, D_MODEL), const),
                  pl.BlockSpec((D_MODEL, EXPERT_FF), const),
                  pl.BlockSpec((D_MODEL, EXPERT_FF), const),
                  pl.BlockSpec((EXPERT_FF, D_MODEL), const)],
        out_specs=(pl.BlockSpec((tc, D_MODEL), lambda i: (jnp.minimum(i, n_ctx - 1), 0)),
                   pl.BlockSpec((tc, D_MODEL), lambda i: (jnp.maximum(i - n_ctx, 0), 0))),
        scratch_shapes=[pltpu.VMEM((2, TOP_K, tc, D_MODEL), F32),
                        pltpu.SemaphoreType.DMA((2,))],
        compiler_params=_params(("arbitrary",)),
        name="moe_combine",
    )(dest, dest, y_sorted, wt_tok, h, x1, mods, g_post, ws_g, ws_u, ws_d)


def _grid_pos_embed(n_tokens):
    rows = n_tokens // GRID_W
    half = D_MODEL // 2
    quarter = half // 2
    omega = 1.0 / (10000.0 ** (jnp.arange(quarter, dtype=F32) / quarter))

    def axis_embed(pos):
        ang = pos.astype(F32)[:, None] * omega
        return jnp.concatenate([jnp.sin(ang), jnp.cos(ang)], axis=-1)

    e_row = axis_embed(jnp.arange(rows))
    e_col = axis_embed(jnp.arange(GRID_W))
    emb = jnp.concatenate([jnp.broadcast_to(e_row[:, None], (rows, GRID_W, half)),
                           jnp.broadcast_to(e_col[None], (rows, GRID_W, half))], axis=-1)
    return emb.reshape(rows * GRID_W, D_MODEL)


def _lane_row(v, lane0):
    return jnp.zeros((1, LANES), F32).at[0, lane0:lane0 + v.shape[0]].set(v)


def kernel(x_prompt, x_sample, state_gla, state_ssd, c, c_ctx, w_ada, b_ada, g_mix_pre, g_mix_post, w_in, w_gk_up, b_gk, g_gla_norm, conv_w, conv_b, dt_bias, a_log, d_skip, g_ssd_norm, w_out, g_ffn_pre, g_ffn_post, w_router, b_router, w_exp_gate, w_exp_up, w_exp_down, w_sh_gate, w_sh_up, w_sh_down):
    assert x_prompt.shape == (N_CTX_SEQ, CTX_LEN, D_MODEL) and x_sample.shape == (N_LAT_SEQ, LAT_LEN, D_MODEL)
    assert w_ada.shape[0] == 1, "single layer"
    l = 0

    w_in_l = w_in[l]
    o_q = 0
    o_k = o_q + GLA_KEY_WIDTH
    o_v = o_k + GLA_KEY_WIDTH
    o_gout = o_v + GLA_WIDTH
    o_lr = o_gout + GLA_WIDTH
    o_z = o_lr + N_DIR * GLA_GATE_RANK
    o_xbc = o_z + SSD_WIDTH
    o_dt = o_xbc + SSD_CONV_CH
    o_end = o_dt + N_DIR * SSD_HEADS
    assert o_end == w_in_l.shape[1]
    w_main = jnp.concatenate([w_in_l[:, o_gout:o_lr], w_in_l[:, o_z:o_xbc], w_in_l[:, o_v:o_gout],
                              w_in_l[:, o_xbc:o_dt], w_in_l[:, o_q:o_k], w_in_l[:, o_k:o_v]],
                             axis=1).astype(BF16)
    w_small = jnp.concatenate([w_in_l[:, o_lr:o_z], w_in_l[:, o_dt:o_end],
                               jnp.zeros((D_MODEL, LANES - SM_DT - N_DIR * SSD_HEADS), F32)], axis=1)
    w_out_bf = w_out[l].astype(BF16)
    wr_t = w_router[l].T
    wr_hi = wr_t.astype(BF16)
    wr_lo = (wr_t - wr_hi.astype(F32)).astype(BF16)
    conv_w8 = jnp.zeros((SUBLANES, SSD_CONV_CH), F32).at[:conv_w.shape[1]].set(conv_w[l])
    d_skip_x = jnp.repeat(d_skip[l], SSD_HEAD_DIM)[None, :]
    g_gla = g_gla_norm[l][None, :]
    g_ssd = g_ssd_norm[l][None, :]

    cvecs = jnp.zeros((SUBLANES, D_MODEL), F32).at[0].set(c_ctx).at[1:1 + N_LAT_SEQ].set(c)
    mod_flat = _ada_mod(cvecs, w_ada[l], b_ada[l][None, :])
    mods = jnp.zeros((N_MOD, SUBLANES, D_MODEL), F32).at[:, :6].set(
        mod_flat[:N_MOD].reshape(N_MOD, 6, D_MODEL))

    x_ctx = x_prompt.reshape(N_CTX_TOK, D_MODEL)
    x_lat = x_sample.reshape(N_LAT_SEQ * LAT_LEN, D_MODEL)
    pos = _grid_pos_embed(LAT_LEN)

    proj_main, proj_small = _in_proj(x_ctx, x_lat, pos, mods, g_mix_pre[l][None, :], w_main, w_small)

    wgk_pads = [jnp.zeros((LANES, GLA_KEY_WIDTH), F32).at[
        SM_LR + d * GLA_GATE_RANK:SM_LR + (d + 1) * GLA_GATE_RANK].set(w_gk_up[l, d]) for d in range(N_DIR)]
    gla_out = _gla_scan(proj_main, proj_small, wgk_pads, [b_gk[l, d][None, :] for d in range(N_DIR)],
                        [state_gla[:, l, d] for d in range(N_DIR)])
    o_dir = [gla_out[d][0] for d in range(N_DIR)]
    s_gla_dir = [gla_out[d][1] for d in range(N_DIR)]
    y_dir, s_ssd_dir = [], []
    for d in range(N_DIR):
        rev = d == 1
        lane0 = SM_DT + d * SSD_HEADS
        dtb_r = _lane_row(dt_bias[l, d], lane0)
        nea_r = _lane_row(-jnp.exp(a_log[l, d]), lane0)
        s0_ssd = state_ssd[:, l, d].reshape(N_LAT_SEQ, SSD_GROUPS, SSD_GROUP_WIDTH, SSD_STATE)
        y_d, t_d = _ssd_scan(proj_main, proj_small, conv_w8, conv_b[l][None, :],
                             dtb_r, nea_r, dtb_r.T, nea_r.T, d_skip_x, s0_ssd, rev, d)
        y_dir.append(y_d)
        s_ssd_dir.append(t_d.reshape(N_CTX_SEQ, SSD_HEADS, SSD_HEAD_DIM, SSD_STATE))
    new_state_gla = jnp.stack(s_gla_dir, axis=1)[:, None]
    new_state_ssd = jnp.stack(s_ssd_dir, axis=1)[:, None]

    x1 = _out_proj(o_dir[0], o_dir[1], y_dir[0], y_dir[1], proj_main, x_ctx, x_lat, pos, mods,
                   g_gla, g_ssd, g_mix_post[l][None, :], w_out_bf)

    h_ffn, idx, wt_tok, rank, cnt = _router(x1, mods, g_ffn_pre[l][None, :], wr_hi, wr_lo,
                                            b_router[l][:, None])
    i32 = jnp.int32
    e_ids = jnp.arange(N_EXPERTS, dtype=i32)
    counts = cnt[:, 0].astype(i32)
    padded = (counts + MOE_BLK - 1) // MOE_BLK * MOE_BLK
    pad_end = jnp.sum(jnp.where(e_ids[None, :] <= e_ids[:, None], padded[None, :], 0), axis=1)
    pad_start = pad_end - padded
    slot0 = jnp.sum(jnp.where(idx[:, :, None] == e_ids, pad_start, 0), axis=-1)
    dest = slot0 + rank
    n_blk = pad_end[-1] // MOE_BLK
    blk_ids = jnp.arange(N_MOE_BLOCKS, dtype=i32)
    blk_src = jnp.minimum(blk_ids, n_blk - 1)
    blk_expert = jnp.minimum(jnp.sum((pad_end[None, :] <= (blk_src * MOE_BLK)[:, None]).astype(i32), axis=1),
                             N_EXPERTS - 1)
    blk_hot = blk_expert[:, None] == e_ids
    nonempty = counts > 0
    ordinal = jnp.sum(jnp.where((e_ids[None, :] < e_ids[:, None]) & nonempty[None, :], 1, 0), axis=1)
    next_e = jnp.min(jnp.where((e_ids[None, :] > e_ids[:, None]) & nonempty[None, :], e_ids[None, :],
                               N_EXPERTS), axis=1)
    next_e = jnp.where(next_e == N_EXPERTS, -1, next_e)
    blk_first = jnp.concatenate([jnp.ones((1,), i32), (blk_expert[1:] != blk_expert[:-1]).astype(i32)])
    blk_next = jnp.sum(jnp.where(blk_hot, next_e, 0), axis=1).astype(i32)
    blk_slot = (jnp.sum(jnp.where(blk_hot, ordinal, 0), axis=1) % 2).astype(i32)
    n_blk_arr = n_blk.astype(i32)[None]
    x_sorted = _dispatch((pad_start + counts).astype(i32), (padded - counts).astype(i32),
                         n_blk_arr, dest, h_ffn)
    y_sorted = _experts(blk_expert.astype(i32), blk_src.astype(i32), n_blk_arr, blk_first, blk_next,
                        blk_slot, x_sorted, w_exp_gate[l], w_exp_up[l], w_exp_down[l])
    out_ctx, out_lat = _combine(dest, y_sorted, wt_tok, h_ffn, x1, mods, g_ffn_post[l][None, :],
                                w_sh_gate[l].astype(BF16), w_sh_up[l].astype(BF16),
                                w_sh_down[l].astype(BF16))
    return (out_ctx.reshape(N_CTX_SEQ, CTX_LEN, D_MODEL), out_lat.reshape(N_LAT_SEQ, LAT_LEN, D_MODEL),
            new_state_gla, new_state_ssd)
```

```python
import functools

import numpy as np
import jax
import jax.numpy as jnp
from jax import lax
from jax.experimental import pallas as pl
from jax.experimental.pallas import tpu as pltpu

F32 = jnp.float32
BF16 = jnp.bfloat16

D_MODEL = 2048
N_CTX_SEQ = 16
CTX_LEN = 256
N_LAT_SEQ = 2
LAT_LEN = 4096
GRID_W = 64
EPS = 1e-6
N_CTX_TOK = N_CTX_SEQ * CTX_LEN
M_TOK = N_CTX_TOK + N_LAT_SEQ * LAT_LEN
SEQ_LENS = (CTX_LEN,) * N_CTX_SEQ + (LAT_LEN,) * N_LAT_SEQ
ROWS_PER_MOD = 4096
N_MOD = M_TOK // ROWS_PER_MOD

GLA_HEADS = 4
GLA_DK = 128
GLA_DV = 256
GLA_KEY_WIDTH = GLA_HEADS * GLA_DK
GLA_WIDTH = GLA_HEADS * GLA_DV
GLA_GATE_RANK = 16
GLA_GATE_TAU = 16.0
GLA_CHUNK = 64
GLA_SCAN_CHUNK = 2 * GLA_CHUNK
GLA_ROWS = 256

SSD_HEADS = 16
SSD_HEAD_DIM = 64
SSD_GROUPS = 2
SSD_HPG = SSD_HEADS // SSD_GROUPS
SSD_STATE = 128
SSD_WIDTH = SSD_HEADS * SSD_HEAD_DIM
SSD_GROUP_WIDTH = SSD_WIDTH // SSD_GROUPS
SSD_CHUNK = 128
SSD_CONV_CH = SSD_WIDTH + 2 * SSD_GROUPS * SSD_STATE
N_DIR = 2

N_EXPERTS = 256
TOP_K = 8
N_EXPERT_GROUPS = 8
GROUP_SIZE = N_EXPERTS // N_EXPERT_GROUPS
TOPK_GROUPS = 4
EXPERT_FF = 512
ROUTED_SCALE = 2.5

C_GOUT = 0
C_Z = C_GOUT + GLA_WIDTH
C_V = C_Z + SSD_WIDTH
C_XBC = C_V + GLA_WIDTH
C_Q = C_XBC + SSD_CONV_CH
C_K = C_Q + GLA_KEY_WIDTH
MAIN_WIDTH = C_K + GLA_KEY_WIDTH
LANES = 128
SUBLANES = 8
SUBLANES_BF16 = 2 * SUBLANES
SM_LR = 0
SM_DT = N_DIR * GLA_GATE_RANK

TM_PROJ = 256
TN_PROJ = 512
TM_OUT = 256
TM_ROUTE = 256
MOE_BLK = 128
N_PAIRS = M_TOK * TOP_K
N_MOE_BLOCKS = N_PAIRS // MOE_BLK + N_EXPERTS
N_SLOTS = N_MOE_BLOCKS * MOE_BLK
T_DISPATCH = 512
T_COMBINE = 128
ADA_TN = 1024
VMEM_LIMIT = 52 * 1024 * 1024
W_SLOTS = 3
VMEM_LIMIT_EXPERTS = 58 * 1024 * 1024


def _mm(a, b):
    return jnp.dot(a, b, preferred_element_type=F32)


def _mm_nt(a, b):
    return lax.dot_general(a, b, (((1,), (1,)), ((), ())), preferred_element_type=F32)


def _mm_tn(a, b):
    return lax.dot_general(a, b, (((0,), (0,)), ((), ())), preferred_element_type=F32)


def _split2(x):
    hi = x.astype(BF16)
    lo = (x - hi.astype(F32)).astype(BF16)
    return hi, lo


def _split3(x):
    hi = x.astype(BF16)
    r = x - hi.astype(F32)
    mid = r.astype(BF16)
    lo = (r - mid.astype(F32)).astype(BF16)
    return hi, mid, lo


def _mm_x3(a, b):
    a_hi, a_lo = _split2(a)
    b_hi, b_lo = _split2(b)
    return _mm(a_hi, b_hi) + _mm(a_lo, b_hi) + _mm(a_hi, b_lo)


def _mm_sel_left(sel_bf, x):
    hi, mid, lo = _split3(x)
    return _mm(sel_bf, hi) + _mm(sel_bf, mid) + _mm(sel_bf, lo)


def _mm_sel_right(x, sel_bf):
    hi, mid, lo = _split3(x)
    return _mm(hi, sel_bf) + _mm(mid, sel_bf) + _mm(lo, sel_bf)


def _sigmoid(x):
    return 1.0 / (1.0 + jnp.exp(-x))


def _silu(x):
    return x * _sigmoid(x)


def _softplus(x):
    return jnp.maximum(x, 0.0) + jnp.log1p(jnp.exp(-jnp.abs(x)))


def _rms(x, g):
    return x * lax.rsqrt(jnp.mean(x * x, axis=-1, keepdims=True) + EPS) * g


def _params(sem, vmem=VMEM_LIMIT):
    return pltpu.CompilerParams(dimension_semantics=sem, vmem_limit_bytes=vmem)


def _ada_kernel(c_ref, w_ref, b_ref, o_ref):
    o_ref[...] = _mm_x3(_silu(c_ref[...]), w_ref[...]) + b_ref[...]


def _ada_mod(cvecs, w_ada, b_ada):
    n_out = w_ada.shape[1]
    return pl.pallas_call(
        _ada_kernel,
        out_shape=jax.ShapeDtypeStruct((SUBLANES, n_out), F32),
        grid=(n_out // ADA_TN,),
        in_specs=[pl.BlockSpec((SUBLANES, D_MODEL), lambda j: (0, 0)),
                  pl.BlockSpec((D_MODEL, ADA_TN), lambda j: (0, j)),
                  pl.BlockSpec((1, ADA_TN), lambda j: (0, j))],
        out_specs=pl.BlockSpec((SUBLANES, ADA_TN), lambda j: (0, j)),
        compiler_params=_params(("arbitrary",)),
        name="ada_mod",
    )(cvecs, w_ada, b_ada)


def _token_specs(tm, buffers=2):
    n_ctx = N_CTX_TOK // tm
    n_pos = LAT_LEN // tm
    mode = dict(pipeline_mode=pl.Buffered(buffers)) if buffers != 2 else {}
    ctx = pl.BlockSpec((tm, D_MODEL), lambda i, *_: (jnp.minimum(i, n_ctx - 1), 0), **mode)
    lat = pl.BlockSpec((tm, D_MODEL), lambda i, *_: (jnp.maximum(i - n_ctx, 0), 0), **mode)
    pos = pl.BlockSpec((tm, D_MODEL), lambda i, *_: (jnp.maximum(i - n_ctx, 0) % n_pos, 0), **mode)
    return n_ctx, [ctx, lat, pos]


def _inproj_kernel(xc_ref, xl_ref, pos_ref, mod_ref, g_ref, w_ref, ws_ref, o_ref, os_ref, h_scr, *, n_ctx):
    def prologue(x):
        h = _rms(x, g_ref[...]) * (1.0 + mod_ref[0, 1:2, :]) + mod_ref[0, 0:1, :]
        h_hi, h_lo = _split2(h)
        h_scr[...] = h_hi
        ws_hi, ws_lo = _split2(ws_ref[...])
        os_ref[...] = _mm(h_hi, ws_hi) + _mm(h_lo, ws_hi) + _mm(h_hi, ws_lo)

    is_ctx = pl.program_id(0) < n_ctx

    @pl.when(is_ctx)
    def _():
        prologue(xc_ref[...])

    @pl.when(jnp.logical_not(is_ctx))
    def _():
        prologue(xl_ref[...] + pos_ref[...])

    for j in range(MAIN_WIDTH // TN_PROJ):
        cols = slice(j * TN_PROJ, (j + 1) * TN_PROJ)
        o_ref[:, cols] = _mm(h_scr[...], w_ref[:, cols]).astype(BF16)


def _in_proj(x_ctx, x_lat, pos, mods, g_pre, w_main, w_small):
    tm = TM_PROJ
    n_ctx, tok_specs = _token_specs(tm)
    const = lambda i: (0, 0)
    once = dict(pipeline_mode=pl.Buffered(1))
    return pl.pallas_call(
        functools.partial(_inproj_kernel, n_ctx=n_ctx),
        out_shape=(jax.ShapeDtypeStruct((M_TOK, MAIN_WIDTH), BF16),
                   jax.ShapeDtypeStruct((M_TOK, LANES), F32)),
        grid=(M_TOK // tm,),
        in_specs=tok_specs + [
                  pl.BlockSpec((1, SUBLANES, D_MODEL), lambda i: (i * tm // ROWS_PER_MOD, 0, 0)),
                  pl.BlockSpec((1, D_MODEL), const),
                  pl.BlockSpec((D_MODEL, MAIN_WIDTH), const, **once),
                  pl.BlockSpec((D_MODEL, LANES), const, **once)],
        out_specs=(pl.BlockSpec((tm, MAIN_WIDTH), lambda i: (i, 0)),
                   pl.BlockSpec((tm, LANES), lambda i: (i, 0))),
        scratch_shapes=[pltpu.VMEM((tm, D_MODEL), BF16)],
        compiler_params=_params(("arbitrary",)),
        name="in_proj",
    )(x_ctx, x_lat, pos, mods, g_pre, w_main, w_small)


def _scan_schedule(rows_per_step, reverse):
    blk, flag, s0i, soi, emit, has_prev, has_next = [], [], [], [], [], [], []
    start = 0
    for s, length in enumerate(SEQ_LENS):
        nb = length // rows_per_step
        is_ctx = s < N_CTX_SEQ
        order = range(nb - 1, -1, -1) if reverse else range(nb)
        for n, b in enumerate(order):
            blk.append(start + b)
            flag.append((1 if is_ctx else 2) if n == 0 else 0)
            s0i.append(0 if is_ctx else s - N_CTX_SEQ)
            soi.append(s if is_ctx else N_CTX_SEQ - 1)
            emit.append(1 if (is_ctx and n == nb - 1) else 0)
            has_prev.append(1 if b > 0 else 0)
            has_next.append(1 if b < nb - 1 else 0)
        start += nb
    return tuple(np.array(a, np.int32) for a in (blk, flag, s0i, soi, emit, has_prev, has_next))


def _gla_kernel(blkf_ref, blkb_ref, flag_ref, s0i_ref, soi_ref, emit_ref, *refs):
    n_in = 7
    ins = (refs[:n_in], refs[n_in:2 * n_in])
    outs = (refs[2 * n_in:2 * n_in + 2], refs[2 * n_in + 2:2 * n_in + 4])
    st_scr = refs[2 * n_in + 4]
    i = pl.program_id(0)
    flag = flag_ref[i]

    @pl.when(flag == 1)
    def _():
        st_scr[...] = jnp.zeros_like(st_scr)

    @pl.when(flag == 2)
    def _():
        for d in range(N_DIR):
            for h in range(GLA_HEADS):
                st_scr[d, h] = ins[d][6][0, h].T

    c = GLA_SCAN_CHUNK
    r_id = lax.broadcasted_iota(jnp.int32, (c, c), 0)
    c_id = lax.broadcasted_iota(jnp.int32, (c, c), 1)
    tri = (c_id <= r_id, c_id >= r_id)
    tri_bf = tuple(jnp.where(m, 1.0, 0.0).astype(BF16) for m in tri)

    log_a = []
    for d in range(N_DIR):
        sm_ref, wgk_ref, bgk_ref = ins[d][3:6]
        gk = _mm_x3(sm_ref[...], wgk_ref[...]) + bgk_ref[...]
        log_a.append((jnp.minimum(gk, 0.0) - jnp.log1p(jnp.exp(-jnp.abs(gk)))) * (1.0 / GLA_GATE_TAU))

    n_chunks = GLA_ROWS // c
    for step in range(n_chunks):
        for d in range(N_DIR):
            reverse = d == 1
            q_ref, k_ref, v_ref = ins[d][:3]
            o_ref = outs[d][0]
            lo = (n_chunks - 1 - step if reverse else step) * c
            b_all = _mm_sel_left(tri_bf[d], log_a[d][lo:lo + c])
            for h in range(GLA_HEADS):
                kc = slice(h * GLA_DK, (h + 1) * GLA_DK)
                vc = slice(h * GLA_DV, (h + 1) * GLA_DV)
                b = b_all[:, kc]
                b_end = b[0:1] if reverse else b[c - 1:c]
                mid = c // 2 if reverse else c // 2 - 1
                b_mid = b[mid:mid + 1]
                q = q_ref[lo:lo + c, kc].astype(F32) * (GLA_DK ** -0.5)
                k = k_ref[lo:lo + c, kc].astype(F32)
                v = v_ref[lo:lo + c, vc]
                q_m = (q * jnp.exp(b - b_mid)).astype(BF16)
                k_m = (k * jnp.exp(b_mid - b)).astype(BF16)
                q_e = (q * jnp.exp(b)).astype(BF16)
                att = jnp.where(tri[d], _mm_nt(q_m, k_m), 0.0).astype(BF16)
                st = st_scr[d, h]
                o_ref[lo:lo + c, vc] = (_mm(att, v) + _mm_nt(q_e, st.astype(BF16))).astype(BF16)
                k_end = (k * jnp.exp(b_end - b)).astype(BF16)
                st_scr[d, h] = st * jnp.exp(b_end) + _mm_tn(v, k_end)

    @pl.when(emit_ref[i] == 1)
    def _():
        for d in range(N_DIR):
            for h in range(GLA_HEADS):
                outs[d][1][0, h] = st_scr[d, h].T


def _gla_scan(proj_main, proj_small, wgk_pads, bgks, s0s):
    sched_f = _scan_schedule(GLA_ROWS, False)
    sched_b = _scan_schedule(GLA_ROWS, True)
    for a, b in zip(sched_f[1:5], sched_b[1:5]):
        assert np.array_equal(a, b)
    n_steps = M_TOK // GLA_ROWS
    t = GLA_ROWS
    q_blk, k_blk, v_blk = C_Q // GLA_KEY_WIDTH, C_K // GLA_KEY_WIDTH, C_V // GLA_WIDTH
    state_blk = (1, GLA_HEADS, GLA_DK, GLA_DV)

    def dir_in_specs(d):
        blk_of = lambda refs: refs[d]
        return [
            pl.BlockSpec((t, GLA_KEY_WIDTH), lambda i, *r: (blk_of(r)[i], q_blk)),
            pl.BlockSpec((t, GLA_KEY_WIDTH), lambda i, *r: (blk_of(r)[i], k_blk)),
            pl.BlockSpec((t, GLA_WIDTH), lambda i, *r: (blk_of(r)[i], v_blk)),
            pl.BlockSpec((t, LANES), lambda i, *r: (blk_of(r)[i], 0)),
            pl.BlockSpec((LANES, GLA_KEY_WIDTH), lambda i, *r: (0, 0)),
            pl.BlockSpec((1, GLA_KEY_WIDTH), lambda i, *r: (0, 0)),
            pl.BlockSpec(state_blk, lambda i, *r: (r[3][i], 0, 0, 0)),
        ]

    def dir_out_specs(d):
        blk_of = lambda refs: refs[d]
        return [pl.BlockSpec((t, GLA_WIDTH), lambda i, *r: (blk_of(r)[i], 0)),
                pl.BlockSpec(state_blk, lambda i, *r: (r[4][i], 0, 0, 0))]

    dir_out_shape = [jax.ShapeDtypeStruct((M_TOK, GLA_WIDTH), BF16),
                     jax.ShapeDtypeStruct((N_CTX_SEQ,) + state_blk[1:], F32)]
    operands = []
    for d in range(N_DIR):
        operands += [proj_main, proj_main, proj_main, proj_small, wgk_pads[d], bgks[d], s0s[d]]
    o_f, s_f, o_b, s_b = pl.pallas_call(
        _gla_kernel,
        out_shape=tuple(dir_out_shape * N_DIR),
        grid_spec=pltpu.PrefetchScalarGridSpec(
            num_scalar_prefetch=6,
            grid=(n_steps,),
            in_specs=dir_in_specs(0) + dir_in_specs(1),
            out_specs=tuple(dir_out_specs(0) + dir_out_specs(1)),
            scratch_shapes=[pltpu.VMEM((N_DIR, GLA_HEADS, GLA_DV, GLA_DK), F32)]),
        compiler_params=_params(("arbitrary",)),
        name="gla_scan",
    )(*(jnp.asarray(a) for a in (sched_f[0], sched_b[0]) + sched_f[1:5]), *operands)
    return (o_f, s_f), (o_b, s_b)


def _ssd_kernel(blk_ref, flag_ref, s0i_ref, soi_ref, emit_ref, hp_ref, hn_ref,
                xbc_ref, xprev_ref, xnext_ref, sm_ref, cw_ref, cb_ref,
                dtb_r_ref, nea_r_ref, dtb_c_ref, nea_c_ref, dsk_ref, expand_ref, s0_ref,
                y_ref, so_ref, st_scr, *, reverse, lane0, add_skip):
    i = pl.program_id(0)
    flag = flag_ref[i]
    t = SSD_CHUNK

    @pl.when(flag == 1)
    def _():
        st_scr[...] = jnp.zeros_like(st_scr)

    @pl.when(flag == 2)
    def _():
        for g in range(SSD_GROUPS):
            st_scr[g] = s0_ref[0, g].T

    xbc = xbc_ref[...].astype(F32)
    last = SUBLANES_BF16 - 1
    prev = jnp.where(hp_ref[i] == 1, xprev_ref[last:last + 1, :].astype(F32), 0.0)
    nxt = jnp.where(hn_ref[i] == 1, xnext_ref[0:1, :].astype(F32), 0.0)
    row = lax.broadcasted_iota(jnp.int32, xbc.shape, 0)
    x_m1 = jnp.where(row == 0, prev, pltpu.roll(xbc, 1, 0))
    x_p1 = jnp.where(row == t - 1, nxt, pltpu.roll(xbc, t - 1, 0))
    act = _silu(x_m1 * cw_ref[0:1, :] + xbc * cw_ref[1:2, :] + x_p1 * cw_ref[2:3, :] + cb_ref[...])
    xs = act[:, :SSD_WIDTH]
    bm = act[:, SSD_WIDTH:SSD_WIDTH + SSD_GROUPS * SSD_STATE]
    cm = act[:, SSD_WIDTH + SSD_GROUPS * SSD_STATE:]

    sm = sm_ref[...]
    dt = _softplus(sm + dtb_r_ref[...])
    a = dt * nea_r_ref[...]
    a_t = _softplus(sm.T + dtb_c_ref[...]) * nea_c_ref[...]

    r_id = lax.broadcasted_iota(jnp.int32, (t, t), 0)
    c_id = lax.broadcasted_iota(jnp.int32, (t, t), 1)
    tri = (c_id >= r_id) if reverse else (c_id <= r_id)
    tri_bf = jnp.where(tri, 1.0, 0.0).astype(BF16)
    tri_t_bf = jnp.where((r_id >= c_id) if reverse else (r_id <= c_id), 1.0, 0.0).astype(BF16)
    cs = _mm_sel_left(tri_bf, a)
    cs_t = _mm_sel_right(a_t, tri_t_bf)

    expand = expand_ref[...]
    dt_x = _mm_sel_right(dt, expand)
    cs_x = _mm_sel_right(cs, expand)
    cs_end_x = cs_x[0:1] if reverse else cs_x[t - 1:t]
    x_in = xs * dt_x
    x_bf = x_in.astype(BF16)
    x_w = (x_in * jnp.exp(cs_end_x - cs_x)).astype(BF16)
    decay_in = jnp.exp(cs_x)
    decay_end = jnp.exp(cs_end_x)

    for g in range(SSD_GROUPS):
        gc = slice(g * SSD_STATE, (g + 1) * SSD_STATE)
        gw = slice(g * SSD_GROUP_WIDTH, (g + 1) * SSD_GROUP_WIDTH)
        c_g = cm[:, gc].astype(BF16)
        b_g = bm[:, gc].astype(BF16)
        cb = _mm_nt(c_g, b_g)
        st = st_scr[g]
        y_state = _mm(c_g, st.astype(BF16)) * decay_in[:, gw]
        for r in range(SSD_HPG):
            hh = g * SSD_HPG + r
            hc = slice(hh * SSD_HEAD_DIM, (hh + 1) * SSD_HEAD_DIM)
            lane = lane0 + hh
            seg = cs[:, lane:lane + 1] - cs_t[lane:lane + 1, :]
            lm = jnp.exp(jnp.where(tri, seg, -jnp.inf))
            y_h = _mm((cb * lm).astype(BF16), x_bf[:, hc]) + y_state[:, r * SSD_HEAD_DIM:(r + 1) * SSD_HEAD_DIM]
            if add_skip:
                y_h = y_h + xs[:, hc] * dsk_ref[:, hc]
            y_ref[:, hc] = y_h.astype(BF16)
        st_scr[g] = st * decay_end[:, gw] + _mm_tn(b_g, x_w[:, gw])

    @pl.when(emit_ref[i] == 1)
    def _():
        for g in range(SSD_GROUPS):
            so_ref[0, g] = st_scr[g].T


def _ssd_scan(proj_main, proj_small, conv_w, conv_b, dtb_r, nea_r, dtb_c, nea_c, d_skip_x, s0,
              reverse, direction):
    sched = tuple(jnp.asarray(a) for a in _scan_schedule(SSD_CHUNK, reverse))
    t = SSD_CHUNK
    n_steps = M_TOK // t
    xbc_blk = C_XBC // SSD_CONV_CH
    halo = SUBLANES_BF16
    rb = t // halo
    n_rb = M_TOK // halo
    state_blk = (1, SSD_GROUPS, SSD_GROUP_WIDTH, SSD_STATE)
    lane0 = SM_DT + direction * SSD_HEADS
    expand = (np.arange(LANES)[:, None] - lane0 == np.arange(SSD_WIDTH)[None, :] // SSD_HEAD_DIM)
    expand = jnp.asarray(expand.astype(np.float32)).astype(BF16)
    return pl.pallas_call(
        functools.partial(_ssd_kernel, reverse=reverse, lane0=lane0,
                          add_skip=not reverse),
        out_shape=(jax.ShapeDtypeStruct((M_TOK, SSD_WIDTH), BF16),
                   jax.ShapeDtypeStruct((N_CTX_SEQ,) + state_blk[1:], F32)),
        grid_spec=pltpu.PrefetchScalarGridSpec(
            num_scalar_prefetch=7,
            grid=(n_steps,),
            in_specs=[
                pl.BlockSpec((t, SSD_CONV_CH), lambda i, blk, *_: (blk[i], xbc_blk)),
                pl.BlockSpec((halo, SSD_CONV_CH),
                             lambda i, blk, *_: (jnp.maximum(blk[i] * rb - 1, 0), xbc_blk)),
                pl.BlockSpec((halo, SSD_CONV_CH),
                             lambda i, blk, *_: (jnp.minimum((blk[i] + 1) * rb, n_rb - 1), xbc_blk)),
                pl.BlockSpec((t, LANES), lambda i, blk, *_: (blk[i], 0)),
                pl.BlockSpec((SUBLANES, SSD_CONV_CH), lambda i, *_: (0, 0)),
                pl.BlockSpec((1, SSD_CONV_CH), lambda i, *_: (0, 0)),
                pl.BlockSpec((1, LANES), lambda i, *_: (0, 0)),
                pl.BlockSpec((1, LANES), lambda i, *_: (0, 0)),
                pl.BlockSpec((LANES, 1), lambda i, *_: (0, 0)),
                pl.BlockSpec((LANES, 1), lambda i, *_: (0, 0)),
                pl.BlockSpec((1, SSD_WIDTH), lambda i, *_: (0, 0)),
                pl.BlockSpec((LANES, SSD_WIDTH), lambda i, *_: (0, 0)),
                pl.BlockSpec(state_blk, lambda i, blk, flag, s0i, *_: (s0i[i], 0, 0, 0)),
            ],
            out_specs=(
                pl.BlockSpec((t, SSD_WIDTH), lambda i, blk, *_: (blk[i], 0)),
                pl.BlockSpec(state_blk, lambda i, blk, flag, s0i, soi, *_: (soi[i], 0, 0, 0)),
            ),
            scratch_shapes=[pltpu.VMEM((SSD_GROUPS, SSD_STATE, SSD_GROUP_WIDTH), F32)]),
        compiler_params=_params(("arbitrary",)),
        name="ssd_bwd" if reverse else "ssd_fwd",
    )(*sched, proj_main, proj_main, proj_main, proj_small, conv_w, conv_b,
      dtb_r, nea_r, dtb_c, nea_c, d_skip_x, expand, s0)


def _outproj_kernel(of_ref, ob_ref, gout_ref, yf_ref, yb_ref, z_ref, xc_ref, xl_ref, pos_ref, mod_ref,
                    ggla_ref, gssd_ref, gpost_ref, w_ref, o_ref, *, n_ctx):
    o = of_ref[...].astype(F32) + ob_ref[...].astype(F32)
    gate = _silu(gout_ref[...].astype(F32))
    parts = []
    for h in range(GLA_HEADS):
        hc = slice(h * GLA_DV, (h + 1) * GLA_DV)
        parts.append((_rms(o[:, hc], ggla_ref[...]) * gate[:, hc]).astype(BF16))
    y = (yf_ref[...].astype(F32) + yb_ref[...].astype(F32)) * _silu(z_ref[...].astype(F32))
    for g in range(SSD_GROUPS):
        gw = slice(g * SSD_GROUP_WIDTH, (g + 1) * SSD_GROUP_WIDTH)
        parts.append(_rms(y[:, gw], gssd_ref[:, gw]).astype(BF16))
    acc = None
    col = 0
    for p in parts:
        term = _mm(p, w_ref[col:col + p.shape[1], :])
        acc = term if acc is None else acc + term
        col += p.shape[1]
    delta = mod_ref[0, 2:3, :] * _rms(acc, gpost_ref[...])
    is_ctx = pl.program_id(0) < n_ctx

    @pl.when(is_ctx)
    def _():
        o_ref[...] = xc_ref[...] + delta

    @pl.when(jnp.logical_not(is_ctx))
    def _():
        o_ref[...] = xl_ref[...] + pos_ref[...] + delta


def _out_proj(o_f, o_b, y_f, y_b, proj_main, x_ctx, x_lat, pos, mods, g_gla, g_ssd, g_post, w_out):
    tm = TM_OUT
    row = lambda i: (i, 0)
    const = lambda i: (0, 0)
    n_ctx, tok_specs = _token_specs(tm)
    return pl.pallas_call(
        functools.partial(_outproj_kernel, n_ctx=n_ctx),
        out_shape=jax.ShapeDtypeStruct((M_TOK, D_MODEL), F32),
        grid=(M_TOK // tm,),
        in_specs=[pl.BlockSpec((tm, GLA_WIDTH), row),
                  pl.BlockSpec((tm, GLA_WIDTH), row),
                  pl.BlockSpec((tm, GLA_WIDTH), lambda i: (i, C_GOUT // GLA_WIDTH)),
                  pl.BlockSpec((tm, SSD_WIDTH), row),
                  pl.BlockSpec((tm, SSD_WIDTH), row),
                  pl.BlockSpec((tm, SSD_WIDTH), lambda i: (i, C_Z // SSD_WIDTH))] + tok_specs + [
                  pl.BlockSpec((1, SUBLANES, D_MODEL), lambda i: (i * tm // ROWS_PER_MOD, 0, 0)),
                  pl.BlockSpec((1, GLA_DV), const),
                  pl.BlockSpec((1, SSD_WIDTH), const),
                  pl.BlockSpec((1, D_MODEL), const),
                  pl.BlockSpec((D_MODEL, D_MODEL), const)],
        out_specs=pl.BlockSpec((tm, D_MODEL), row),
        compiler_params=_params(("arbitrary",)),
        name="out_proj",
    )(o_f, o_b, proj_main, y_f, y_b, proj_main, x_ctx, x_lat, pos, mods, g_gla, g_ssd, g_post, w_out)


def _router_kernel(x_ref, mod_ref, g_ref, wr_hi_ref, wr_lo_ref, br_ref,
                   h_ref, idx_ref, wt_ref, rank_ref, cnt_ref, cnt_scr):
    i = pl.program_id(0)
    tm = TM_ROUTE

    @pl.when(i == 0)
    def _():
        cnt_scr[...] = jnp.zeros_like(cnt_scr)

    h = _rms(x_ref[...], g_ref[...]) * (1.0 + mod_ref[0, 4:5, :]) + mod_ref[0, 3:4, :]
    h_ref[...] = h
    h_hi, h_lo = _split2(h)
    wr_hi = wr_hi_ref[...]
    logits = _mm_nt(wr_hi, h_hi) + _mm_nt(wr_hi, h_lo) + _mm_nt(wr_lo_ref[...], h_hi)
    scores = _sigmoid(logits)
    sel = scores + br_ref[...]
    neg = -jnp.inf

    def first_argmax(x, ids, n):
        m = jnp.max(x, axis=0, keepdims=True)
        return m, jnp.min(jnp.where(x == m, ids, float(n)), axis=0, keepdims=True)

    ids_g = lax.broadcasted_iota(jnp.int32, (GROUP_SIZE, tm), 0).astype(F32)
    grp = []
    for g in range(N_EXPERT_GROUPS):
        xg = sel[g * GROUP_SIZE:(g + 1) * GROUP_SIZE]
        m1, a1 = first_argmax(xg, ids_g, GROUP_SIZE)
        m2 = jnp.max(jnp.where(ids_g == a1, neg, xg), axis=0, keepdims=True)
        grp.append(m1 + m2)
    gsc = jnp.concatenate(grp, axis=0)
    ids_8 = lax.broadcasted_iota(jnp.int32, (N_EXPERT_GROUPS, tm), 0).astype(F32)
    keep = jnp.zeros((N_EXPERT_GROUPS, tm), F32)
    for _ in range(TOPK_GROUPS):
        _, a = first_argmax(gsc, ids_8, N_EXPERT_GROUPS)
        pick = ids_8 == a
        keep = jnp.where(pick, 1.0, keep)
        gsc = jnp.where(pick, neg, gsc)
    selm = jnp.concatenate(
        [jnp.where(keep[g:g + 1] > 0.5, sel[g * GROUP_SIZE:(g + 1) * GROUP_SIZE], neg)
         for g in range(N_EXPERT_GROUPS)], axis=0)

    ids_e = lax.broadcasted_iota(jnp.int32, (N_EXPERTS, tm), 0).astype(F32)
    picks, wts = [], []
    chosen = jnp.zeros((N_EXPERTS, tm), F32)
    for _ in range(TOP_K):
        _, a = first_argmax(selm, ids_e, N_EXPERTS)
        hit = ids_e == a
        picks.append(a)
        wts.append(jnp.sum(jnp.where(hit, scores, 0.0), axis=0, keepdims=True))
        chosen = jnp.where(hit, 1.0, chosen)
        selm = jnp.where(hit, neg, selm)
    w = jnp.concatenate(wts, axis=0)
    w = w / jnp.sum(w, axis=0, keepdims=True) * ROUTED_SCALE
    idx_ref[...] = jnp.concatenate(picks, axis=0).astype(jnp.int32)
    wt_ref[...] = jnp.concatenate([w, jnp.zeros((LANES - TOP_K, tm), F32)], axis=0).T

    t_r = lax.broadcasted_iota(jnp.int32, (tm, tm), 0)
    t_c = lax.broadcasted_iota(jnp.int32, (tm, tm), 1)
    before = jnp.where(t_r < t_c, 1.0, 0.0).astype(BF16)
    base = _mm(chosen.astype(BF16), before) + cnt_scr[...]
    rank_ref[...] = jnp.concatenate(
        [jnp.sum(jnp.where(ids_e == a, base, 0.0), axis=0, keepdims=True) for a in picks],
        axis=0).astype(jnp.int32)
    cnt_scr[...] = cnt_scr[...] + jnp.sum(chosen, axis=1, keepdims=True)
    cnt_ref[...] = jnp.broadcast_to(cnt_scr[...], cnt_ref.shape)


def _router(x1, mods, g_pre, wr_hi, wr_lo, b_router):
    tm = TM_ROUTE
    const = lambda i: (0, 0)
    return pl.pallas_call(
        _router_kernel,
        out_shape=(jax.ShapeDtypeStruct((M_TOK, D_MODEL), F32),
                   jax.ShapeDtypeStruct((TOP_K, M_TOK), jnp.int32),
                   jax.ShapeDtypeStruct((M_TOK, LANES), F32),
                   jax.ShapeDtypeStruct((TOP_K, M_TOK), jnp.int32),
                   jax.ShapeDtypeStruct((N_EXPERTS, LANES), F32)),
        grid=(M_TOK // tm,),
        in_specs=[pl.BlockSpec((tm, D_MODEL), lambda i: (i, 0)),
                  pl.BlockSpec((1, SUBLANES, D_MODEL), lambda i: (i * tm // ROWS_PER_MOD, 0, 0)),
                  pl.BlockSpec((1, D_MODEL), const),
                  pl.BlockSpec((N_EXPERTS, D_MODEL), const),
                  pl.BlockSpec((N_EXPERTS, D_MODEL), const),
                  pl.BlockSpec((N_EXPERTS, 1), const)],
        out_specs=(pl.BlockSpec((tm, D_MODEL), lambda i: (i, 0)),
                   pl.BlockSpec((TOP_K, tm), lambda i: (0, i)),
                   pl.BlockSpec((tm, LANES), lambda i: (i, 0)),
                   pl.BlockSpec((TOP_K, tm), lambda i: (0, i)),
                   pl.BlockSpec((N_EXPERTS, LANES), const)),
        scratch_shapes=[pltpu.VMEM((N_EXPERTS, 1), F32)],
        compiler_params=_params(("arbitrary",)),
        name="router",
    )(x1, mods, g_pre, wr_hi, wr_lo, b_router)


def _dispatch_kernel(fill_start_ref, fill_len_ref, nb_ref, dest_ref, h_ref, xs_hbm, zero_scr, sem, zsem):
    @pl.when(pl.program_id(0) == 0)
    def _():
        zero_scr[...] = jnp.zeros_like(zero_scr)

        def for_each_fill(act):
            def pad_body(e, carry):
                start = fill_start_ref[e]
                length = fill_len_ref[e]
                head = jnp.minimum((-start) & (SUBLANES - 1), length)
                for j in range(SUBLANES - 1):
                    @pl.when(j < head)
                    def _():
                        act(pltpu.make_async_copy(zero_scr.at[pl.ds(0, 1)],
                                                  xs_hbm.at[pl.ds(start + j, 1)], zsem))
                body_start = start + head
                body_len = length - head
                for bit in (64, 32, 16, 8):
                    @pl.when((body_len & bit) != 0)
                    def _():
                        off = pl.multiple_of(body_start + (body_len & jnp.int32(~(2 * bit - 1))), SUBLANES)
                        act(pltpu.make_async_copy(zero_scr.at[pl.ds(0, bit)],
                                                  xs_hbm.at[pl.ds(off, bit)], zsem))
                return carry

            def tail_body(b, carry):
                act(pltpu.make_async_copy(zero_scr, xs_hbm.at[pl.ds(b * MOE_BLK, MOE_BLK)], zsem))
                return carry

            lax.fori_loop(0, N_EXPERTS, pad_body, 0)
            lax.fori_loop(nb_ref[0], N_MOE_BLOCKS, tail_body, 0)

        for_each_fill(lambda cp: cp.start())
        for_each_fill(lambda cp: cp.wait())

    def body(t, carry):
        for k in range(TOP_K):
            pltpu.make_async_copy(h_ref.at[pl.ds(t, 1)],
                                  xs_hbm.at[pl.ds(dest_ref[k, t], 1)], sem).start()
        return carry

    lax.fori_loop(0, T_DISPATCH, body, 0)
    pltpu.make_async_copy(xs_hbm.at[pl.ds(0, T_DISPATCH * TOP_K)],
                          xs_hbm.at[pl.ds(0, T_DISPATCH * TOP_K)], sem).wait()


def _dispatch(fill_start, fill_len, n_blk, dest, h):
    return pl.pallas_call(
        _dispatch_kernel,
        out_shape=jax.ShapeDtypeStruct((N_SLOTS, D_MODEL), F32),
        grid_spec=pltpu.PrefetchScalarGridSpec(
            num_scalar_prefetch=3,
            grid=(M_TOK // T_DISPATCH,),
            in_specs=[pl.BlockSpec((TOP_K, T_DISPATCH), lambda i, *_: (0, i), memory_space=pltpu.SMEM),
                      pl.BlockSpec((T_DISPATCH, D_MODEL), lambda i, *_: (i, 0))],
            out_specs=pl.BlockSpec(memory_space=pl.ANY),
            scratch_shapes=[pltpu.VMEM((MOE_BLK, D_MODEL), F32),
                            pltpu.SemaphoreType.DMA, pltpu.SemaphoreType.DMA]),
        compiler_params=_params(("arbitrary",)),
        name="moe_dispatch",
    )(fill_start, fill_len, n_blk, dest, h)


def _expert_weight_copies(e, s, w_hbm, w_f32, sem):
    return [pltpu.make_async_copy(w_hbm[j].at[e], w_f32[j].at[s], sem.at[s, j]) for j in range(3)]


def _expert_kernel(be_ref, bs_ref, nb_ref, first_ref, next_ref, next2_ref, slot_ref,
                   x_ref, wg_hbm, wu_hbm, wd_hbm, y_ref,
                   wg_f32, wu_f32, wd_f32, wg_scr, wu_scr, wd_scr, sem):
    b = pl.program_id(0)
    w_hbm = (wg_hbm, wu_hbm, wd_hbm)
    w_f32 = (wg_f32, wu_f32, wd_f32)

    @pl.when(b == 0)
    def _():
        for cp in _expert_weight_copies(be_ref[0], 0, w_hbm, w_f32, sem):
            cp.start()

        @pl.when(next_ref[0] >= 0)
        def _():
            for cp in _expert_weight_copies(next_ref[0], 1, w_hbm, w_f32, sem):
                cp.start()

    @pl.when(b < nb_ref[0])
    def _():
        @pl.when(first_ref[b] == 1)
        def _():
            s = slot_ref[b]
            for cp in _expert_weight_copies(be_ref[b], s, w_hbm, w_f32, sem):
                cp.wait()

            @pl.when(next2_ref[b] >= 0)
            def _():
                for cp in _expert_weight_copies(next2_ref[b], lax.rem(s + 2, W_SLOTS), w_hbm, w_f32, sem):
                    cp.start()

            wg_scr[...] = wg_f32[s].astype(BF16)
            wu_scr[...] = wu_f32[s].astype(BF16)
            wd_scr[...] = wd_f32[s].astype(BF16)

        x = x_ref[...].astype(BF16)
        act = _silu(_mm(x, wg_scr[...])) * _mm(x, wu_scr[...])
        y_ref[...] = _mm(act.astype(BF16), wd_scr[...])

    @pl.when(b >= nb_ref[0])
    def _():
        y_ref[...] = jnp.zeros_like(y_ref)


def _experts(blk_expert, blk_src, n_blk, blk_first, blk_next, blk_next2, blk_slot, x_sorted, w_g, w_u, w_d):
    hbm = pl.BlockSpec(memory_space=pl.ANY)
    return pl.pallas_call(
        _expert_kernel,
        out_shape=jax.ShapeDtypeStruct((N_SLOTS, D_MODEL), F32),
        grid_spec=pltpu.PrefetchScalarGridSpec(
            num_scalar_prefetch=7,
            grid=(N_MOE_BLOCKS,),
            in_specs=[pl.BlockSpec((MOE_BLK, D_MODEL), lambda b, be, bs, *_: (bs[b], 0)), hbm, hbm, hbm],
            out_specs=pl.BlockSpec((MOE_BLK, D_MODEL), lambda b, *_: (b, 0)),
            scratch_shapes=[pltpu.VMEM((W_SLOTS, D_MODEL, EXPERT_FF), F32),
                            pltpu.VMEM((W_SLOTS, D_MODEL, EXPERT_FF), F32),
                            pltpu.VMEM((W_SLOTS, EXPERT_FF, D_MODEL), F32),
                            pltpu.VMEM((D_MODEL, EXPERT_FF), BF16),
                            pltpu.VMEM((D_MODEL, EXPERT_FF), BF16),
                            pltpu.VMEM((EXPERT_FF, D_MODEL), BF16),
                            pltpu.SemaphoreType.DMA((W_SLOTS, 3))]),
        compiler_params=_params(("arbitrary",), vmem=VMEM_LIMIT_EXPERTS),
        name="moe_experts",
    )(blk_expert, blk_src, n_blk, blk_first, blk_next, blk_next2, blk_slot, x_sorted, w_g, w_u, w_d)


def _combine_copy(y_hbm, buf, sem, slot, k, t, src_row):
    return pltpu.make_async_copy(y_hbm.at[pl.ds(src_row, 1)], buf.at[slot, k, pl.ds(t, 1)], sem.at[slot])


def _combine_kernel(dest_ref, dest_next_ref, y_hbm, wt_ref, h_ref, x_ref, mod_ref, g_ref,
                    wsg_ref, wsu_ref, wsd_ref, oc_ref, ol_ref, buf, sem, *, n_ctx):
    i = pl.program_id(0)
    n = pl.num_programs(0)
    slot = i % 2

    def issue(d_ref, s):
        def body(t, carry):
            for k in range(TOP_K):
                _combine_copy(y_hbm, buf, sem, s, k, t, d_ref[k, t]).start()
            return carry
        lax.fori_loop(0, T_COMBINE, body, 0)

    @pl.when(i == 0)
    def _():
        issue(dest_ref, 0)

    for s in range(2):
        @pl.when((i + 1 < n) & (1 - slot == s))
        def _():
            issue(dest_next_ref, s)

    pltpu.make_async_copy(buf.at[slot], buf.at[slot], sem.at[slot]).wait()

    wt = wt_ref[...]
    acc = buf[slot, 0] * wt[:, 0:1]
    for k in range(1, TOP_K):
        acc = acc + buf[slot, k] * wt[:, k:k + 1]
    h = h_ref[...].astype(BF16)
    act = _silu(_mm(h, wsg_ref[...])) * _mm(h, wsu_ref[...])
    y = acc + _mm(act.astype(BF16), wsd_ref[...])
    out = x_ref[...] + mod_ref[0, 5:6, :] * _rms(y, g_ref[...])

    @pl.when(i < n_ctx)
    def _():
        oc_ref[...] = out

    @pl.when(i >= n_ctx)
    def _():
        ol_ref[...] = out


def _combine(dest, y_sorted, wt_tok, h, x1, mods, g_post, ws_g, ws_u, ws_d):
    tc = T_COMBINE
    n = M_TOK // tc
    n_ctx = N_CTX_TOK // tc
    row = lambda i: (i, 0)
    const = lambda i: (0, 0)
    return pl.pallas_call(
        functools.partial(_combine_kernel, n_ctx=n_ctx),
        out_shape=(jax.ShapeDtypeStruct((N_CTX_TOK, D_MODEL), F32),
                   jax.ShapeDtypeStruct((M_TOK - N_CTX_TOK, D_MODEL), F32)),
        grid=(n,),
        in_specs=[pl.BlockSpec((TOP_K, tc), lambda i: (0, i), memory_space=pltpu.SMEM),
                  pl.BlockSpec((TOP_K, tc), lambda i: (0, jnp.minimum(i + 1, n - 1)),
                               memory_space=pltpu.SMEM),
                  pl.BlockSpec(memory_space=pl.ANY),
                  pl.BlockSpec((tc, LANES), row),
                  pl.BlockSpec((tc, D_MODEL), row),
                  pl.BlockSpec((tc, D_MODEL), row),
                  pl.BlockSpec((1, SUBLANES, D_MODEL), lambda i: (i * tc // ROWS_PER_MOD, 0, 0)),
                  pl.BlockSpec((1, D_MODEL), const),
                  pl.BlockSpec((D_MODEL, EXPERT_FF), const),
                  pl.BlockSpec((D_MODEL, EXPERT_FF), const),
                  pl.BlockSpec((EXPERT_FF, D_MODEL), const)],
        out_specs=(pl.BlockSpec((tc, D_MODEL), lambda i: (jnp.minimum(i, n_ctx - 1), 0)),
                   pl.BlockSpec((tc, D_MODEL), lambda i: (jnp.maximum(i - n_ctx, 0), 0))),
        scratch_shapes=[pltpu.VMEM((2, TOP_K, tc, D_MODEL), F32),
                        pltpu.SemaphoreType.DMA((2,))],
        compiler_params=_params(("arbitrary",)),
        name="moe_combine",
    )(dest, dest, y_sorted, wt_tok, h, x1, mods, g_post, ws_g, ws_u, ws_d)


def _grid_pos_embed(n_tokens):
    rows = n_tokens // GRID_W
    half = D_MODEL // 2
    quarter = half // 2
    omega = 1.0 / (10000.0 ** (jnp.arange(quarter, dtype=F32) / quarter))

    def axis_embed(pos):
        ang = pos.astype(F32)[:, None] * omega
        return jnp.concatenate([jnp.sin(ang), jnp.cos(ang)], axis=-1)

    e_row = axis_embed(jnp.arange(rows))
    e_col = axis_embed(jnp.arange(GRID_W))
    emb = jnp.concatenate([jnp.broadcast_to(e_row[:, None], (rows, GRID_W, half)),
                           jnp.broadcast_to(e_col[None], (rows, GRID_W, half))], axis=-1)
    return emb.reshape(rows * GRID_W, D_MODEL)


def _lane_row(v, lane0):
    return jnp.zeros((1, LANES), F32).at[0, lane0:lane0 + v.shape[0]].set(v)


def kernel(x_prompt, x_sample, state_gla, state_ssd, c, c_ctx, w_ada, b_ada, g_mix_pre, g_mix_post, w_in, w_gk_up, b_gk, g_gla_norm, conv_w, conv_b, dt_bias, a_log, d_skip, g_ssd_norm, w_out, g_ffn_pre, g_ffn_post, w_router, b_router, w_exp_gate, w_exp_up, w_exp_down, w_sh_gate, w_sh_up, w_sh_down):
    assert x_prompt.shape == (N_CTX_SEQ, CTX_LEN, D_MODEL) and x_sample.shape == (N_LAT_SEQ, LAT_LEN, D_MODEL)
    assert w_ada.shape[0] == 1, "single layer"
    l = 0

    w_in_l = w_in[l]
    o_q = 0
    o_k = o_q + GLA_KEY_WIDTH
    o_v = o_k + GLA_KEY_WIDTH
    o_gout = o_v + GLA_WIDTH
    o_lr = o_gout + GLA_WIDTH
    o_z = o_lr + N_DIR * GLA_GATE_RANK
    o_xbc = o_z + SSD_WIDTH
    o_dt = o_xbc + SSD_CONV_CH
    o_end = o_dt + N_DIR * SSD_HEADS
    assert o_end == w_in_l.shape[1]
    w_main = jnp.concatenate([w_in_l[:, o_gout:o_lr], w_in_l[:, o_z:o_xbc], w_in_l[:, o_v:o_gout],
                              w_in_l[:, o_xbc:o_dt], w_in_l[:, o_q:o_k], w_in_l[:, o_k:o_v]],
                             axis=1).astype(BF16)
    w_small = jnp.concatenate([w_in_l[:, o_lr:o_z], w_in_l[:, o_dt:o_end],
                               jnp.zeros((D_MODEL, LANES - SM_DT - N_DIR * SSD_HEADS), F32)], axis=1)
    w_out_bf = w_out[l].astype(BF16)
    wr_t = w_router[l].T
    wr_hi = wr_t.astype(BF16)
    wr_lo = (wr_t - wr_hi.astype(F32)).astype(BF16)
    conv_w8 = jnp.zeros((SUBLANES, SSD_CONV_CH), F32).at[:conv_w.shape[1]].set(conv_w[l])
    d_skip_x = jnp.repeat(d_skip[l], SSD_HEAD_DIM)[None, :]
    g_gla = g_gla_norm[l][None, :]
    g_ssd = g_ssd_norm[l][None, :]

    cvecs = jnp.zeros((SUBLANES, D_MODEL), F32).at[0].set(c_ctx).at[1:1 + N_LAT_SEQ].set(c)
    mod_flat = _ada_mod(cvecs, w_ada[l], b_ada[l][None, :])
    mods = jnp.zeros((N_MOD, SUBLANES, D_MODEL), F32).at[:, :6].set(
        mod_flat[:N_MOD].reshape(N_MOD, 6, D_MODEL))

    x_ctx = x_prompt.reshape(N_CTX_TOK, D_MODEL)
    x_lat = x_sample.reshape(N_LAT_SEQ * LAT_LEN, D_MODEL)
    pos = _grid_pos_embed(LAT_LEN)

    proj_main, proj_small = _in_proj(x_ctx, x_lat, pos, mods, g_mix_pre[l][None, :], w_main, w_small)

    wgk_pads = [jnp.zeros((LANES, GLA_KEY_WIDTH), F32).at[
        SM_LR + d * GLA_GATE_RANK:SM_LR + (d + 1) * GLA_GATE_RANK].set(w_gk_up[l, d]) for d in range(N_DIR)]
    gla_out = _gla_scan(proj_main, proj_small, wgk_pads, [b_gk[l, d][None, :] for d in range(N_DIR)],
                        [state_gla[:, l, d] for d in range(N_DIR)])
    o_dir = [gla_out[d][0] for d in range(N_DIR)]
    s_gla_dir = [gla_out[d][1] for d in range(N_DIR)]
    y_dir, s_ssd_dir = [], []
    for d in range(N_DIR):
        rev = d == 1
        lane0 = SM_DT + d * SSD_HEADS
        dtb_r = _lane_row(dt_bias[l, d], lane0)
        nea_r = _lane_row(-jnp.exp(a_log[l, d]), lane0)
        s0_ssd = state_ssd[:, l, d].reshape(N_LAT_SEQ, SSD_GROUPS, SSD_GROUP_WIDTH, SSD_STATE)
        y_d, t_d = _ssd_scan(proj_main, proj_small, conv_w8, conv_b[l][None, :],
                             dtb_r, nea_r, dtb_r.T, nea_r.T, d_skip_x, s0_ssd, rev, d)
        y_dir.append(y_d)
        s_ssd_dir.append(t_d.reshape(N_CTX_SEQ, SSD_HEADS, SSD_HEAD_DIM, SSD_STATE))
    new_state_gla = jnp.stack(s_gla_dir, axis=1)[:, None]
    new_state_ssd = jnp.stack(s_ssd_dir, axis=1)[:, None]

    x1 = _out_proj(o_dir[0], o_dir[1], y_dir[0], y_dir[1], proj_main, x_ctx, x_lat, pos, mods,
                   g_gla, g_ssd, g_mix_post[l][None, :], w_out_bf)

    h_ffn, idx, wt_tok, rank, cnt = _router(x1, mods, g_ffn_pre[l][None, :], wr_hi, wr_lo,
                                            b_router[l][:, None])
    i32 = jnp.int32
    e_ids = jnp.arange(N_EXPERTS, dtype=i32)
    counts = cnt[:, 0].astype(i32)
    padded = (counts + MOE_BLK - 1) // MOE_BLK * MOE_BLK
    pad_end = jnp.sum(jnp.where(e_ids[None, :] <= e_ids[:, None], padded[None, :], 0), axis=1)
    pad_start = pad_end - padded
    slot0 = jnp.sum(jnp.where(idx[:, :, None] == e_ids, pad_start, 0), axis=-1)
    dest = slot0 + rank
    n_blk = pad_end[-1] // MOE_BLK
    blk_ids = jnp.arange(N_MOE_BLOCKS, dtype=i32)
    blk_src = jnp.minimum(blk_ids, n_blk - 1)
    blk_expert = jnp.minimum(jnp.sum((pad_end[None, :] <= (blk_src * MOE_BLK)[:, None]).astype(i32), axis=1),
                             N_EXPERTS - 1)
    blk_hot = blk_expert[:, None] == e_ids
    nonempty = counts > 0
    ordinal = jnp.sum(jnp.where((e_ids[None, :] < e_ids[:, None]) & nonempty[None, :], 1, 0), axis=1)
    next_e = jnp.min(jnp.where((e_ids[None, :] > e_ids[:, None]) & nonempty[None, :], e_ids[None, :],
                               N_EXPERTS), axis=1)
    next_e = jnp.where(next_e == N_EXPERTS, -1, next_e)
    blk_first = jnp.concatenate([jnp.ones((1,), i32), (blk_expert[1:] != blk_expert[:-1]).astype(i32)])
    next2_e = jnp.where(next_e >= 0,
                        jnp.sum(jnp.where(e_ids[None, :] == next_e[:, None], next_e[None, :], 0), axis=1), -1)
    blk_next = jnp.sum(jnp.where(blk_hot, next_e, 0), axis=1).astype(i32)
    blk_next2 = jnp.sum(jnp.where(blk_hot, next2_e, 0), axis=1).astype(i32)
    blk_slot = (jnp.sum(jnp.where(blk_hot, ordinal, 0), axis=1) % W_SLOTS).astype(i32)
    n_blk_arr = n_blk.astype(i32)[None]
    x_sorted = _dispatch((pad_start + counts).astype(i32), (padded - counts).astype(i32),
                         n_blk_arr, dest, h_ffn)
    y_sorted = _experts(blk_expert.astype(i32), blk_src.astype(i32), n_blk_arr, blk_first, blk_next,
                        blk_next2, blk_slot, x_sorted, w_exp_gate[l], w_exp_up[l], w_exp_down[l])
    out_ctx, out_lat = _combine(dest, y_sorted, wt_tok, h_ffn, x1, mods, g_ffn_post[l][None, :],
                                w_sh_gate[l].astype(BF16), w_sh_up[l].astype(BF16),
                                w_sh_down[l].astype(BF16))
    return (out_ctx.reshape(N_CTX_SEQ, CTX_LEN, D_MODEL), out_lat.reshape(N_LAT_SEQ, LAT_LEN, D_MODEL),
            new_state_gla, new_state_ssd)
```

```python
import functools

import numpy as np
import jax
import jax.numpy as jnp
from jax import lax
from jax.experimental import pallas as pl
from jax.experimental.pallas import tpu as pltpu

F32 = jnp.float32
BF16 = jnp.bfloat16

D_MODEL = 2048
N_CTX_SEQ = 16
CTX_LEN = 256
N_LAT_SEQ = 2
LAT_LEN = 4096
GRID_W = 64
EPS = 1e-6
N_CTX_TOK = N_CTX_SEQ * CTX_LEN
M_TOK = N_CTX_TOK + N_LAT_SEQ * LAT_LEN
SEQ_LENS = (CTX_LEN,) * N_CTX_SEQ + (LAT_LEN,) * N_LAT_SEQ
ROWS_PER_MOD = 4096
N_MOD = M_TOK // ROWS_PER_MOD

GLA_HEADS = 4
GLA_DK = 128
GLA_DV = 256
GLA_KEY_WIDTH = GLA_HEADS * GLA_DK
GLA_WIDTH = GLA_HEADS * GLA_DV
GLA_GATE_RANK = 16
GLA_GATE_TAU = 16.0
GLA_CHUNK = 64
GLA_SCAN_CHUNK = 2 * GLA_CHUNK
GLA_ROWS = 256

SSD_HEADS = 16
SSD_HEAD_DIM = 64
SSD_GROUPS = 2
SSD_HPG = SSD_HEADS // SSD_GROUPS
SSD_STATE = 128
SSD_WIDTH = SSD_HEADS * SSD_HEAD_DIM
SSD_GROUP_WIDTH = SSD_WIDTH // SSD_GROUPS
SSD_CHUNK = 128
SSD_CONV_CH = SSD_WIDTH + 2 * SSD_GROUPS * SSD_STATE
N_DIR = 2

N_EXPERTS = 256
TOP_K = 8
N_EXPERT_GROUPS = 8
GROUP_SIZE = N_EXPERTS // N_EXPERT_GROUPS
TOPK_GROUPS = 4
EXPERT_FF = 512
ROUTED_SCALE = 2.5

C_GOUT = 0
C_Z = C_GOUT + GLA_WIDTH
C_V = C_Z + SSD_WIDTH
C_XBC = C_V + GLA_WIDTH
C_Q = C_XBC + SSD_CONV_CH
C_K = C_Q + GLA_KEY_WIDTH
MAIN_WIDTH = C_K + GLA_KEY_WIDTH
LANES = 128
SUBLANES = 8
SUBLANES_BF16 = 2 * SUBLANES
SM_LR = 0
SM_DT = N_DIR * GLA_GATE_RANK

TM_PROJ = 256
TN_PROJ = 512
TM_OUT = 256
TM_ROUTE = 256
MOE_BLK = 128
N_PAIRS = M_TOK * TOP_K
N_MOE_BLOCKS = N_PAIRS // MOE_BLK + N_EXPERTS
N_SLOTS = N_MOE_BLOCKS * MOE_BLK
T_DISPATCH = 512
T_COMBINE = 128
ADA_TN = 1024
VMEM_LIMIT = 52 * 1024 * 1024
W_SLOTS = 3
VMEM_LIMIT_EXPERTS = 58 * 1024 * 1024


def _mm(a, b):
    return jnp.dot(a, b, preferred_element_type=F32)


def _mm_nt(a, b):
    return lax.dot_general(a, b, (((1,), (1,)), ((), ())), preferred_element_type=F32)


def _mm_tn(a, b):
    return lax.dot_general(a, b, (((0,), (0,)), ((), ())), preferred_element_type=F32)


def _split2(x):
    hi = x.astype(BF16)
    lo = (x - hi.astype(F32)).astype(BF16)
    return hi, lo


def _split3(x):
    hi = x.astype(BF16)
    r = x - hi.astype(F32)
    mid = r.astype(BF16)
    lo = (r - mid.astype(F32)).astype(BF16)
    return hi, mid, lo


def _mm_x3(a, b):
    a_hi, a_lo = _split2(a)
    b_hi, b_lo = _split2(b)
    return _mm(a_hi, b_hi) + _mm(a_lo, b_hi) + _mm(a_hi, b_lo)


def _mm_sel_left(sel_bf, x):
    hi, mid, lo = _split3(x)
    return _mm(sel_bf, hi) + _mm(sel_bf, mid) + _mm(sel_bf, lo)


def _mm_sel_right(x, sel_bf):
    hi, mid, lo = _split3(x)
    return _mm(hi, sel_bf) + _mm(mid, sel_bf) + _mm(lo, sel_bf)


def _sigmoid(x):
    return 1.0 / (1.0 + jnp.exp(-x))


def _silu(x):
    return x * _sigmoid(x)


def _softplus(x):
    return jnp.maximum(x, 0.0) + jnp.log1p(jnp.exp(-jnp.abs(x)))


def _rms(x, g):
    return x * lax.rsqrt(jnp.mean(x * x, axis=-1, keepdims=True) + EPS) * g


def _params(sem, vmem=VMEM_LIMIT):
    return pltpu.CompilerParams(dimension_semantics=sem, vmem_limit_bytes=vmem)


def _ada_kernel(c_ref, w_ref, b_ref, o_ref):
    o_ref[...] = _mm_x3(_silu(c_ref[...]), w_ref[...]) + b_ref[...]


def _ada_mod(cvecs, w_ada, b_ada):
    n_out = w_ada.shape[1]
    return pl.pallas_call(
        _ada_kernel,
        out_shape=jax.ShapeDtypeStruct((SUBLANES, n_out), F32),
        grid=(n_out // ADA_TN,),
        in_specs=[pl.BlockSpec((SUBLANES, D_MODEL), lambda j: (0, 0)),
                  pl.BlockSpec((D_MODEL, ADA_TN), lambda j: (0, j)),
                  pl.BlockSpec((1, ADA_TN), lambda j: (0, j))],
        out_specs=pl.BlockSpec((SUBLANES, ADA_TN), lambda j: (0, j)),
        compiler_params=_params(("arbitrary",)),
        name="ada_mod",
    )(cvecs, w_ada, b_ada)


def _token_specs(tm, buffers=2):
    n_ctx = N_CTX_TOK // tm
    n_pos = LAT_LEN // tm
    mode = dict(pipeline_mode=pl.Buffered(buffers)) if buffers != 2 else {}
    ctx = pl.BlockSpec((tm, D_MODEL), lambda i, *_: (jnp.minimum(i, n_ctx - 1), 0), **mode)
    lat = pl.BlockSpec((tm, D_MODEL), lambda i, *_: (jnp.maximum(i - n_ctx, 0), 0), **mode)
    pos = pl.BlockSpec((tm, D_MODEL), lambda i, *_: (jnp.maximum(i - n_ctx, 0) % n_pos, 0), **mode)
    return n_ctx, [ctx, lat, pos]


def _inproj_kernel(xc_ref, xl_ref, pos_ref, mod_ref, g_ref, w_ref, ws_ref, o_ref, os_ref, h_scr, *, n_ctx):
    def prologue(x):
        h = _rms(x, g_ref[...]) * (1.0 + mod_ref[0, 1:2, :]) + mod_ref[0, 0:1, :]
        h_hi, h_lo = _split2(h)
        h_scr[...] = h_hi
        ws_hi, ws_lo = _split2(ws_ref[...])
        os_ref[...] = _mm(h_hi, ws_hi) + _mm(h_lo, ws_hi) + _mm(h_hi, ws_lo)

    is_ctx = pl.program_id(0) < n_ctx

    @pl.when(is_ctx)
    def _():
        prologue(xc_ref[...])

    @pl.when(jnp.logical_not(is_ctx))
    def _():
        prologue(xl_ref[...] + pos_ref[...])

    for j in range(MAIN_WIDTH // TN_PROJ):
        cols = slice(j * TN_PROJ, (j + 1) * TN_PROJ)
        o_ref[:, cols] = _mm(h_scr[...], w_ref[:, cols]).astype(BF16)


def _in_proj(x_ctx, x_lat, pos, mods, g_pre, w_main, w_small):
    tm = TM_PROJ
    n_ctx, tok_specs = _token_specs(tm)
    const = lambda i: (0, 0)
    once = dict(pipeline_mode=pl.Buffered(1))
    return pl.pallas_call(
        functools.partial(_inproj_kernel, n_ctx=n_ctx),
        out_shape=(jax.ShapeDtypeStruct((M_TOK, MAIN_WIDTH), BF16),
                   jax.ShapeDtypeStruct((M_TOK, LANES), F32)),
        grid=(M_TOK // tm,),
        in_specs=tok_specs + [
                  pl.BlockSpec((1, SUBLANES, D_MODEL), lambda i: (i * tm // ROWS_PER_MOD, 0, 0)),
                  pl.BlockSpec((1, D_MODEL), const),
                  pl.BlockSpec((D_MODEL, MAIN_WIDTH), const, **once),
                  pl.BlockSpec((D_MODEL, LANES), const, **once)],
        out_specs=(pl.BlockSpec((tm, MAIN_WIDTH), lambda i: (i, 0)),
                   pl.BlockSpec((tm, LANES), lambda i: (i, 0))),
        scratch_shapes=[pltpu.VMEM((tm, D_MODEL), BF16)],
        compiler_params=_params(("arbitrary",)),
        name="in_proj",
    )(x_ctx, x_lat, pos, mods, g_pre, w_main, w_small)


def _scan_schedule(rows_per_step, reverse):
    blk, flag, s0i, soi, emit, has_prev, has_next = [], [], [], [], [], [], []
    start = 0
    for s, length in enumerate(SEQ_LENS):
        nb = length // rows_per_step
        is_ctx = s < N_CTX_SEQ
        order = range(nb - 1, -1, -1) if reverse else range(nb)
        for n, b in enumerate(order):
            blk.append(start + b)
            flag.append((1 if is_ctx else 2) if n == 0 else 0)
            s0i.append(0 if is_ctx else s - N_CTX_SEQ)
            soi.append(s if is_ctx else N_CTX_SEQ - 1)
            emit.append(1 if (is_ctx and n == nb - 1) else 0)
            has_prev.append(1 if b > 0 else 0)
            has_next.append(1 if b < nb - 1 else 0)
        start += nb
    return tuple(np.array(a, np.int32) for a in (blk, flag, s0i, soi, emit, has_prev, has_next))


def _gla_kernel(blkf_ref, blkb_ref, flag_ref, s0i_ref, soi_ref, emit_ref, *refs):
    n_in = 7
    ins = (refs[:n_in], refs[n_in:2 * n_in])
    outs = (refs[2 * n_in:2 * n_in + 2], refs[2 * n_in + 2:2 * n_in + 4])
    st_scr = refs[2 * n_in + 4]
    i = pl.program_id(0)
    flag = flag_ref[i]

    @pl.when(flag == 1)
    def _():
        st_scr[...] = jnp.zeros_like(st_scr)

    @pl.when(flag == 2)
    def _():
        for d in range(N_DIR):
            for h in range(GLA_HEADS):
                st_scr[d, h] = ins[d][6][0, h].T

    c = GLA_SCAN_CHUNK
    r_id = lax.broadcasted_iota(jnp.int32, (c, c), 0)
    c_id = lax.broadcasted_iota(jnp.int32, (c, c), 1)
    tri = (c_id <= r_id, c_id >= r_id)
    tri_bf = tuple(jnp.where(m, 1.0, 0.0).astype(BF16) for m in tri)

    log_a = []
    for d in range(N_DIR):
        sm_ref, wgk_ref, bgk_ref = ins[d][3:6]
        gk = _mm_x3(sm_ref[...], wgk_ref[...]) + bgk_ref[...]
        log_a.append((jnp.minimum(gk, 0.0) - jnp.log1p(jnp.exp(-jnp.abs(gk)))) * (1.0 / GLA_GATE_TAU))

    n_chunks = GLA_ROWS // c
    for step in range(n_chunks):
        for d in range(N_DIR):
            reverse = d == 1
            q_ref, k_ref, v_ref = ins[d][:3]
            o_ref = outs[d][0]
            lo = (n_chunks - 1 - step if reverse else step) * c
            b_all = _mm_sel_left(tri_bf[d], log_a[d][lo:lo + c])
            for h in range(GLA_HEADS):
                kc = slice(h * GLA_DK, (h + 1) * GLA_DK)
                vc = slice(h * GLA_DV, (h + 1) * GLA_DV)
                b = b_all[:, kc]
                b_end = b[0:1] if reverse else b[c - 1:c]
                mid = c // 2 if reverse else c // 2 - 1
                b_mid = b[mid:mid + 1]
                q = q_ref[lo:lo + c, kc].astype(F32) * (GLA_DK ** -0.5)
                k = k_ref[lo:lo + c, kc].astype(F32)
                v = v_ref[lo:lo + c, vc]
                q_m = (q * jnp.exp(b - b_mid)).astype(BF16)
                k_m = (k * jnp.exp(b_mid - b)).astype(BF16)
                q_e = (q * jnp.exp(b)).astype(BF16)
                att = jnp.where(tri[d], _mm_nt(q_m, k_m), 0.0).astype(BF16)
                st = st_scr[d, h]
                o_ref[lo:lo + c, vc] = (_mm(att, v) + _mm_nt(q_e, st.astype(BF16))).astype(BF16)
                k_end = (k * jnp.exp(b_end - b)).astype(BF16)
                st_scr[d, h] = st * jnp.exp(b_end) + _mm_tn(v, k_end)

    @pl.when(emit_ref[i] == 1)
    def _():
        for d in range(N_DIR):
            for h in range(GLA_HEADS):
                outs[d][1][0, h] = st_scr[d, h].T


def _gla_scan(proj_main, proj_small, wgk_pads, bgks, s0s):
    sched_f = _scan_schedule(GLA_ROWS, False)
    sched_b = _scan_schedule(GLA_ROWS, True)
    for a, b in zip(sched_f[1:5], sched_b[1:5]):
        assert np.array_equal(a, b)
    n_steps = M_TOK // GLA_ROWS
    t = GLA_ROWS
    q_blk, k_blk, v_blk = C_Q // GLA_KEY_WIDTH, C_K // GLA_KEY_WIDTH, C_V // GLA_WIDTH
    state_blk = (1, GLA_HEADS, GLA_DK, GLA_DV)

    def dir_in_specs(d):
        blk_of = lambda refs: refs[d]
        return [
            pl.BlockSpec((t, GLA_KEY_WIDTH), lambda i, *r: (blk_of(r)[i], q_blk)),
            pl.BlockSpec((t, GLA_KEY_WIDTH), lambda i, *r: (blk_of(r)[i], k_blk)),
            pl.BlockSpec((t, GLA_WIDTH), lambda i, *r: (blk_of(r)[i], v_blk)),
            pl.BlockSpec((t, LANES), lambda i, *r: (blk_of(r)[i], 0)),
            pl.BlockSpec((LANES, GLA_KEY_WIDTH), lambda i, *r: (0, 0)),
            pl.BlockSpec((1, GLA_KEY_WIDTH), lambda i, *r: (0, 0)),
            pl.BlockSpec(state_blk, lambda i, *r: (r[3][i], 0, 0, 0)),
        ]

    def dir_out_specs(d):
        blk_of = lambda refs: refs[d]
        return [pl.BlockSpec((t, GLA_WIDTH), lambda i, *r: (blk_of(r)[i], 0)),
                pl.BlockSpec(state_blk, lambda i, *r: (r[4][i], 0, 0, 0))]

    dir_out_shape = [jax.ShapeDtypeStruct((M_TOK, GLA_WIDTH), BF16),
                     jax.ShapeDtypeStruct((N_CTX_SEQ,) + state_blk[1:], F32)]
    operands = []
    for d in range(N_DIR):
        operands += [proj_main, proj_main, proj_main, proj_small, wgk_pads[d], bgks[d], s0s[d]]
    o_f, s_f, o_b, s_b = pl.pallas_call(
        _gla_kernel,
        out_shape=tuple(dir_out_shape * N_DIR),
        grid_spec=pltpu.PrefetchScalarGridSpec(
            num_scalar_prefetch=6,
            grid=(n_steps,),
            in_specs=dir_in_specs(0) + dir_in_specs(1),
            out_specs=tuple(dir_out_specs(0) + dir_out_specs(1)),
            scratch_shapes=[pltpu.VMEM((N_DIR, GLA_HEADS, GLA_DV, GLA_DK), F32)]),
        compiler_params=_params(("arbitrary",)),
        name="gla_scan",
    )(*(jnp.asarray(a) for a in (sched_f[0], sched_b[0]) + sched_f[1:5]), *operands)
    return (o_f, s_f), (o_b, s_b)


def _ssd_kernel(blk_ref, flag_ref, s0i_ref, soi_ref, emit_ref, hp_ref, hn_ref,
                xbc_ref, xprev_ref, xnext_ref, sm_ref, cw_ref, cb_ref,
                dtb_r_ref, nea_r_ref, dtb_c_ref, nea_c_ref, dsk_ref, expand_ref, s0_ref,
                y_ref, so_ref, st_scr, *, reverse, lane0, add_skip):
    i = pl.program_id(0)
    flag = flag_ref[i]
    t = SSD_CHUNK

    @pl.when(flag == 1)
    def _():
        st_scr[...] = jnp.zeros_like(st_scr)

    @pl.when(flag == 2)
    def _():
        for g in range(SSD_GROUPS):
            st_scr[g] = s0_ref[0, g].T

    xbc = xbc_ref[...].astype(F32)
    last = SUBLANES_BF16 - 1
    prev = jnp.where(hp_ref[i] == 1, xprev_ref[last:last + 1, :].astype(F32), 0.0)
    nxt = jnp.where(hn_ref[i] == 1, xnext_ref[0:1, :].astype(F32), 0.0)
    row = lax.broadcasted_iota(jnp.int32, xbc.shape, 0)
    x_m1 = jnp.where(row == 0, prev, pltpu.roll(xbc, 1, 0))
    x_p1 = jnp.where(row == t - 1, nxt, pltpu.roll(xbc, t - 1, 0))
    act = _silu(x_m1 * cw_ref[0:1, :] + xbc * cw_ref[1:2, :] + x_p1 * cw_ref[2:3, :] + cb_ref[...])
    xs = act[:, :SSD_WIDTH]
    bm = act[:, SSD_WIDTH:SSD_WIDTH + SSD_GROUPS * SSD_STATE]
    cm = act[:, SSD_WIDTH + SSD_GROUPS * SSD_STATE:]

    sm = sm_ref[...]
    dt = _softplus(sm + dtb_r_ref[...])
    a = dt * nea_r_ref[...]
    a_t = _softplus(sm.T + dtb_c_ref[...]) * nea_c_ref[...]

    r_id = lax.broadcasted_iota(jnp.int32, (t, t), 0)
    c_id = lax.broadcasted_iota(jnp.int32, (t, t), 1)
    tri = (c_id >= r_id) if reverse else (c_id <= r_id)
    tri_bf = jnp.where(tri, 1.0, 0.0).astype(BF16)
    tri_t_bf = jnp.where((r_id >= c_id) if reverse else (r_id <= c_id), 1.0, 0.0).astype(BF16)
    cs = _mm_sel_left(tri_bf, a)
    cs_t = _mm_sel_right(a_t, tri_t_bf)

    expand = expand_ref[...]
    dt_x = _mm_sel_right(dt, expand)
    cs_x = _mm_sel_right(cs, expand)
    cs_end_x = cs_x[0:1] if reverse else cs_x[t - 1:t]
    x_in = xs * dt_x
    x_bf = x_in.astype(BF16)
    x_w = (x_in * jnp.exp(cs_end_x - cs_x)).astype(BF16)
    decay_in = jnp.exp(cs_x)
    decay_end = jnp.exp(cs_end_x)

    for g in range(SSD_GROUPS):
        gc = slice(g * SSD_STATE, (g + 1) * SSD_STATE)
        gw = slice(g * SSD_GROUP_WIDTH, (g + 1) * SSD_GROUP_WIDTH)
        c_g = cm[:, gc].astype(BF16)
        b_g = bm[:, gc].astype(BF16)
        cb = _mm_nt(c_g, b_g)
        st = st_scr[g]
        y_state = _mm(c_g, st.astype(BF16)) * decay_in[:, gw]
        for r in range(SSD_HPG):
            hh = g * SSD_HPG + r
            hc = slice(hh * SSD_HEAD_DIM, (hh + 1) * SSD_HEAD_DIM)
            lane = lane0 + hh
            seg = cs[:, lane:lane + 1] - cs_t[lane:lane + 1, :]
            lm = jnp.exp(jnp.where(tri, seg, -jnp.inf))
            y_h = _mm((cb * lm).astype(BF16), x_bf[:, hc]) + y_state[:, r * SSD_HEAD_DIM:(r + 1) * SSD_HEAD_DIM]
            if add_skip:
                y_h = y_h + xs[:, hc] * dsk_ref[:, hc]
            y_ref[:, hc] = y_h.astype(BF16)
        st_scr[g] = st * decay_end[:, gw] + _mm_tn(b_g, x_w[:, gw])

    @pl.when(emit_ref[i] == 1)
    def _():
        for g in range(SSD_GROUPS):
            so_ref[0, g] = st_scr[g].T


def _ssd_scan(proj_main, proj_small, conv_w, conv_b, dtb_r, nea_r, dtb_c, nea_c, d_skip_x, s0,
              reverse, direction):
    sched = tuple(jnp.asarray(a) for a in _scan_schedule(SSD_CHUNK, reverse))
    t = SSD_CHUNK
    n_steps = M_TOK // t
    xbc_blk = C_XBC // SSD_CONV_CH
    halo = SUBLANES_BF16
    rb = t // halo
    n_rb = M_TOK // halo
    state_blk = (1, SSD_GROUPS, SSD_GROUP_WIDTH, SSD_STATE)
    lane0 = SM_DT + direction * SSD_HEADS
    expand = (np.arange(LANES)[:, None] - lane0 == np.arange(SSD_WIDTH)[None, :] // SSD_HEAD_DIM)
    expand = jnp.asarray(expand.astype(np.float32)).astype(BF16)
    return pl.pallas_call(
        functools.partial(_ssd_kernel, reverse=reverse, lane0=lane0,
                          add_skip=not reverse),
        out_shape=(jax.ShapeDtypeStruct((M_TOK, SSD_WIDTH), BF16),
                   jax.ShapeDtypeStruct((N_CTX_SEQ,) + state_blk[1:], F32)),
        grid_spec=pltpu.PrefetchScalarGridSpec(
            num_scalar_prefetch=7,
            grid=(n_steps,),
            in_specs=[
                pl.BlockSpec((t, SSD_CONV_CH), lambda i, blk, *_: (blk[i], xbc_blk)),
                pl.BlockSpec((halo, SSD_CONV_CH),
                             lambda i, blk, *_: (jnp.maximum(blk[i] * rb - 1, 0), xbc_blk)),
                pl.BlockSpec((halo, SSD_CONV_CH),
                             lambda i, blk, *_: (jnp.minimum((blk[i] + 1) * rb, n_rb - 1), xbc_blk)),
                pl.BlockSpec((t, LANES), lambda i, blk, *_: (blk[i], 0)),
                pl.BlockSpec((SUBLANES, SSD_CONV_CH), lambda i, *_: (0, 0)),
                pl.BlockSpec((1, SSD_CONV_CH), lambda i, *_: (0, 0)),
                pl.BlockSpec((1, LANES), lambda i, *_: (0, 0)),
                pl.BlockSpec((1, LANES), lambda i, *_: (0, 0)),
                pl.BlockSpec((LANES, 1), lambda i, *_: (0, 0)),
                pl.BlockSpec((LANES, 1), lambda i, *_: (0, 0)),
                pl.BlockSpec((1, SSD_WIDTH), lambda i, *_: (0, 0)),
                pl.BlockSpec((LANES, SSD_WIDTH), lambda i, *_: (0, 0)),
                pl.BlockSpec(state_blk, lambda i, blk, flag, s0i, *_: (s0i[i], 0, 0, 0)),
            ],
            out_specs=(
                pl.BlockSpec((t, SSD_WIDTH), lambda i, blk, *_: (blk[i], 0)),
                pl.BlockSpec(state_blk, lambda i, blk, flag, s0i, soi, *_: (soi[i], 0, 0, 0)),
            ),
            scratch_shapes=[pltpu.VMEM((SSD_GROUPS, SSD_STATE, SSD_GROUP_WIDTH), F32)]),
        compiler_params=_params(("arbitrary",)),
        name="ssd_bwd" if reverse else "ssd_fwd",
    )(*sched, proj_main, proj_main, proj_main, proj_small, conv_w, conv_b,
      dtb_r, nea_r, dtb_c, nea_c, d_skip_x, expand, s0)


def _outproj_kernel(of_ref, ob_ref, gout_ref, yf_ref, yb_ref, z_ref, xc_ref, xl_ref, pos_ref, mod_ref,
                    ggla_ref, gssd_ref, gpost_ref, w_ref, o_ref, *, n_ctx):
    o = of_ref[...].astype(F32) + ob_ref[...].astype(F32)
    gate = _silu(gout_ref[...].astype(F32))
    parts = []
    for h in range(GLA_HEADS):
        hc = slice(h * GLA_DV, (h + 1) * GLA_DV)
        parts.append((_rms(o[:, hc], ggla_ref[...]) * gate[:, hc]).astype(BF16))
    y = (yf_ref[...].astype(F32) + yb_ref[...].astype(F32)) * _silu(z_ref[...].astype(F32))
    for g in range(SSD_GROUPS):
        gw = slice(g * SSD_GROUP_WIDTH, (g + 1) * SSD_GROUP_WIDTH)
        parts.append(_rms(y[:, gw], gssd_ref[:, gw]).astype(BF16))
    acc = None
    col = 0
    for p in parts:
        term = _mm(p, w_ref[col:col + p.shape[1], :])
        acc = term if acc is None else acc + term
        col += p.shape[1]
    delta = mod_ref[0, 2:3, :] * _rms(acc, gpost_ref[...])
    is_ctx = pl.program_id(0) < n_ctx

    @pl.when(is_ctx)
    def _():
        o_ref[...] = xc_ref[...] + delta

    @pl.when(jnp.logical_not(is_ctx))
    def _():
        o_ref[...] = xl_ref[...] + pos_ref[...] + delta


def _out_proj(o_f, o_b, y_f, y_b, proj_main, x_ctx, x_lat, pos, mods, g_gla, g_ssd, g_post, w_out):
    tm = TM_OUT
    row = lambda i: (i, 0)
    const = lambda i: (0, 0)
    n_ctx, tok_specs = _token_specs(tm)
    return pl.pallas_call(
        functools.partial(_outproj_kernel, n_ctx=n_ctx),
        out_shape=jax.ShapeDtypeStruct((M_TOK, D_MODEL), F32),
        grid=(M_TOK // tm,),
        in_specs=[pl.BlockSpec((tm, GLA_WIDTH), row),
                  pl.BlockSpec((tm, GLA_WIDTH), row),
                  pl.BlockSpec((tm, GLA_WIDTH), lambda i: (i, C_GOUT // GLA_WIDTH)),
                  pl.BlockSpec((tm, SSD_WIDTH), row),
                  pl.BlockSpec((tm, SSD_WIDTH), row),
                  pl.BlockSpec((tm, SSD_WIDTH), lambda i: (i, C_Z // SSD_WIDTH))] + tok_specs + [
                  pl.BlockSpec((1, SUBLANES, D_MODEL), lambda i: (i * tm // ROWS_PER_MOD, 0, 0)),
                  pl.BlockSpec((1, GLA_DV), const),
                  pl.BlockSpec((1, SSD_WIDTH), const),
                  pl.BlockSpec((1, D_MODEL), const),
                  pl.BlockSpec((D_MODEL, D_MODEL), const)],
        out_specs=pl.BlockSpec((tm, D_MODEL), row),
        compiler_params=_params(("arbitrary",)),
        name="out_proj",
    )(o_f, o_b, proj_main, y_f, y_b, proj_main, x_ctx, x_lat, pos, mods, g_gla, g_ssd, g_post, w_out)


def _router_kernel(x_ref, mod_ref, g_ref, wr_hi_ref, wr_lo_ref, br_ref,
                   h_ref, idx_ref, wt_ref, rank_ref, cnt_ref, cnt_scr):
    i = pl.program_id(0)
    tm = TM_ROUTE

    @pl.when(i == 0)
    def _():
        cnt_scr[...] = jnp.zeros_like(cnt_scr)

    h = _rms(x_ref[...], g_ref[...]) * (1.0 + mod_ref[0, 4:5, :]) + mod_ref[0, 3:4, :]
    h_ref[...] = h
    h_hi, h_lo = _split2(h)
    wr_hi = wr_hi_ref[...]
    logits = _mm_nt(wr_hi, h_hi) + _mm_nt(wr_hi, h_lo) + _mm_nt(wr_lo_ref[...], h_hi)
    scores = _sigmoid(logits)
    sel = scores + br_ref[...]
    neg = -jnp.inf

    def first_argmax(x, ids, n):
        m = jnp.max(x, axis=0, keepdims=True)
        return m, jnp.min(jnp.where(x == m, ids, float(n)), axis=0, keepdims=True)

    ids_g = lax.broadcasted_iota(jnp.int32, (GROUP_SIZE, tm), 0).astype(F32)
    grp = []
    for g in range(N_EXPERT_GROUPS):
        xg = sel[g * GROUP_SIZE:(g + 1) * GROUP_SIZE]
        m1, a1 = first_argmax(xg, ids_g, GROUP_SIZE)
        m2 = jnp.max(jnp.where(ids_g == a1, neg, xg), axis=0, keepdims=True)
        grp.append(m1 + m2)
    gsc = jnp.concatenate(grp, axis=0)
    ids_8 = lax.broadcasted_iota(jnp.int32, (N_EXPERT_GROUPS, tm), 0).astype(F32)
    keep = jnp.zeros((N_EXPERT_GROUPS, tm), F32)
    for _ in range(TOPK_GROUPS):
        _, a = first_argmax(gsc, ids_8, N_EXPERT_GROUPS)
        pick = ids_8 == a
        keep = jnp.where(pick, 1.0, keep)
        gsc = jnp.where(pick, neg, gsc)
    selm = jnp.concatenate(
        [jnp.where(keep[g:g + 1] > 0.5, sel[g * GROUP_SIZE:(g + 1) * GROUP_SIZE], neg)
         for g in range(N_EXPERT_GROUPS)], axis=0)

    ids_e = lax.broadcasted_iota(jnp.int32, (N_EXPERTS, tm), 0).astype(F32)
    picks, wts = [], []
    chosen = jnp.zeros((N_EXPERTS, tm), F32)
    for _ in range(TOP_K):
        _, a = first_argmax(selm, ids_e, N_EXPERTS)
        hit = ids_e == a
        picks.append(a)
        wts.append(jnp.sum(jnp.where(hit, scores, 0.0), axis=0, keepdims=True))
        chosen = jnp.where(hit, 1.0, chosen)
        selm = jnp.where(hit, neg, selm)
    w = jnp.concatenate(wts, axis=0)
    w = w / jnp.sum(w, axis=0, keepdims=True) * ROUTED_SCALE
    idx_ref[...] = jnp.concatenate(picks, axis=0).astype(jnp.int32)
    wt_ref[...] = jnp.concatenate([w, jnp.zeros((LANES - TOP_K, tm), F32)], axis=0).T

    t_r = lax.broadcasted_iota(jnp.int32, (tm, tm), 0)
    t_c = lax.broadcasted_iota(jnp.int32, (tm, tm), 1)
    before = jnp.where(t_r < t_c, 1.0, 0.0).astype(BF16)
    base = _mm(chosen.astype(BF16), before) + cnt_scr[...]
    rank_ref[...] = jnp.concatenate(
        [jnp.sum(jnp.where(ids_e == a, base, 0.0), axis=0, keepdims=True) for a in picks],
        axis=0).astype(jnp.int32)
    cnt_scr[...] = cnt_scr[...] + jnp.sum(chosen, axis=1, keepdims=True)
    cnt_ref[...] = jnp.broadcast_to(cnt_scr[...], cnt_ref.shape)


def _router(x1, mods, g_pre, wr_hi, wr_lo, b_router):
    tm = TM_ROUTE
    const = lambda i: (0, 0)
    return pl.pallas_call(
        _router_kernel,
        out_shape=(jax.ShapeDtypeStruct((M_TOK, D_MODEL), F32),
                   jax.ShapeDtypeStruct((TOP_K, M_TOK), jnp.int32),
                   jax.ShapeDtypeStruct((M_TOK, LANES), F32),
                   jax.ShapeDtypeStruct((TOP_K, M_TOK), jnp.int32),
                   jax.ShapeDtypeStruct((N_EXPERTS, LANES), F32)),
        grid=(M_TOK // tm,),
        in_specs=[pl.BlockSpec((tm, D_MODEL), lambda i: (i, 0)),
                  pl.BlockSpec((1, SUBLANES, D_MODEL), lambda i: (i * tm // ROWS_PER_MOD, 0, 0)),
                  pl.BlockSpec((1, D_MODEL), const),
                  pl.BlockSpec((N_EXPERTS, D_MODEL), const),
                  pl.BlockSpec((N_EXPERTS, D_MODEL), const),
                  pl.BlockSpec((N_EXPERTS, 1), const)],
        out_specs=(pl.BlockSpec((tm, D_MODEL), lambda i: (i, 0)),
                   pl.BlockSpec((TOP_K, tm), lambda i: (0, i)),
                   pl.BlockSpec((tm, LANES), lambda i: (i, 0)),
                   pl.BlockSpec((TOP_K, tm), lambda i: (0, i)),
                   pl.BlockSpec((N_EXPERTS, LANES), const)),
        scratch_shapes=[pltpu.VMEM((N_EXPERTS, 1), F32)],
        compiler_params=_params(("arbitrary",)),
        name="router",
    )(x1, mods, g_pre, wr_hi, wr_lo, b_router)


def _dispatch_kernel(fill_start_ref, fill_len_ref, nb_ref, dest_ref, h_ref, xs_hbm, zero_scr, sem, zsem):
    @pl.when(pl.program_id(0) == 0)
    def _():
        zero_scr[...] = jnp.zeros_like(zero_scr)

        def for_each_fill(act):
            def pad_body(e, carry):
                start = fill_start_ref[e]
                length = fill_len_ref[e]
                head = jnp.minimum((-start) & (SUBLANES - 1), length)
                for j in range(SUBLANES - 1):
                    @pl.when(j < head)
                    def _():
                        act(pltpu.make_async_copy(zero_scr.at[pl.ds(0, 1)],
                                                  xs_hbm.at[pl.ds(start + j, 1)], zsem))
                body_start = start + head
                body_len = length - head
                for bit in (64, 32, 16, 8):
                    @pl.when((body_len & bit) != 0)
                    def _():
                        off = pl.multiple_of(body_start + (body_len & jnp.int32(~(2 * bit - 1))), SUBLANES)
                        act(pltpu.make_async_copy(zero_scr.at[pl.ds(0, bit)],
                                                  xs_hbm.at[pl.ds(off, bit)], zsem))
                return carry

            def tail_body(b, carry):
                act(pltpu.make_async_copy(zero_scr, xs_hbm.at[pl.ds(b * MOE_BLK, MOE_BLK)], zsem))
                return carry

            lax.fori_loop(0, N_EXPERTS, pad_body, 0)
            lax.fori_loop(nb_ref[0], N_MOE_BLOCKS, tail_body, 0)

        for_each_fill(lambda cp: cp.start())
        for_each_fill(lambda cp: cp.wait())

    def body(t, carry):
        for k in range(TOP_K):
            pltpu.make_async_copy(h_ref.at[pl.ds(t, 1)],
                                  xs_hbm.at[pl.ds(dest_ref[k, t], 1)], sem).start(priority=k % 2)
        return carry

    lax.fori_loop(0, T_DISPATCH, body, 0)
    pltpu.make_async_copy(xs_hbm.at[pl.ds(0, T_DISPATCH * TOP_K)],
                          xs_hbm.at[pl.ds(0, T_DISPATCH * TOP_K)], sem).wait()


def _dispatch(fill_start, fill_len, n_blk, dest, h):
    return pl.pallas_call(
        _dispatch_kernel,
        out_shape=jax.ShapeDtypeStruct((N_SLOTS, D_MODEL), F32),
        grid_spec=pltpu.PrefetchScalarGridSpec(
            num_scalar_prefetch=3,
            grid=(M_TOK // T_DISPATCH,),
            in_specs=[pl.BlockSpec((TOP_K, T_DISPATCH), lambda i, *_: (0, i), memory_space=pltpu.SMEM),
                      pl.BlockSpec((T_DISPATCH, D_MODEL), lambda i, *_: (i, 0))],
            out_specs=pl.BlockSpec(memory_space=pl.ANY),
            scratch_shapes=[pltpu.VMEM((MOE_BLK, D_MODEL), F32),
                            pltpu.SemaphoreType.DMA, pltpu.SemaphoreType.DMA]),
        compiler_params=_params(("arbitrary",)),
        name="moe_dispatch",
    )(fill_start, fill_len, n_blk, dest, h)


def _expert_weight_copies(e, s, w_hbm, w_f32, sem):
    return [pltpu.make_async_copy(w_hbm[j].at[e], w_f32[j].at[s], sem.at[s, j]) for j in range(3)]


def _expert_kernel(be_ref, bs_ref, nb_ref, first_ref, next_ref, next2_ref, slot_ref,
                   x_ref, wg_hbm, wu_hbm, wd_hbm, y_ref,
                   wg_f32, wu_f32, wd_f32, wg_scr, wu_scr, wd_scr, sem):
    b = pl.program_id(0)
    w_hbm = (wg_hbm, wu_hbm, wd_hbm)
    w_f32 = (wg_f32, wu_f32, wd_f32)

    @pl.when(b == 0)
    def _():
        for cp in _expert_weight_copies(be_ref[0], 0, w_hbm, w_f32, sem):
            cp.start()

        @pl.when(next_ref[0] >= 0)
        def _():
            for cp in _expert_weight_copies(next_ref[0], 1, w_hbm, w_f32, sem):
                cp.start()

    @pl.when(b < nb_ref[0])
    def _():
        @pl.when(first_ref[b] == 1)
        def _():
            s = slot_ref[b]
            for cp in _expert_weight_copies(be_ref[b], s, w_hbm, w_f32, sem):
                cp.wait()

            @pl.when(next2_ref[b] >= 0)
            def _():
                for cp in _expert_weight_copies(next2_ref[b], lax.rem(s + 2, W_SLOTS), w_hbm, w_f32, sem):
                    cp.start()

            wg_scr[...] = wg_f32[s].astype(BF16)
            wu_scr[...] = wu_f32[s].astype(BF16)
            wd_scr[...] = wd_f32[s].astype(BF16)

        x = x_ref[...].astype(BF16)
        act = _silu(_mm(x, wg_scr[...])) * _mm(x, wu_scr[...])
        y_ref[...] = _mm(act.astype(BF16), wd_scr[...])

    @pl.when(b >= nb_ref[0])
    def _():
        y_ref[...] = jnp.zeros_like(y_ref)


def _experts(blk_expert, blk_src, n_blk, blk_first, blk_next, blk_next2, blk_slot, x_sorted, w_g, w_u, w_d):
    hbm = pl.BlockSpec(memory_space=pl.ANY)
    return pl.pallas_call(
        _expert_kernel,
        out_shape=jax.ShapeDtypeStruct((N_SLOTS, D_MODEL), F32),
        grid_spec=pltpu.PrefetchScalarGridSpec(
            num_scalar_prefetch=7,
            grid=(N_MOE_BLOCKS,),
            in_specs=[pl.BlockSpec((MOE_BLK, D_MODEL), lambda b, be, bs, *_: (bs[b], 0)), hbm, hbm, hbm],
            out_specs=pl.BlockSpec((MOE_BLK, D_MODEL), lambda b, *_: (b, 0)),
            scratch_shapes=[pltpu.VMEM((W_SLOTS, D_MODEL, EXPERT_FF), F32),
                            pltpu.VMEM((W_SLOTS, D_MODEL, EXPERT_FF), F32),
                            pltpu.VMEM((W_SLOTS, EXPERT_FF, D_MODEL), F32),
                            pltpu.VMEM((D_MODEL, EXPERT_FF), BF16),
                            pltpu.VMEM((D_MODEL, EXPERT_FF), BF16),
                            pltpu.VMEM((EXPERT_FF, D_MODEL), BF16),
                            pltpu.SemaphoreType.DMA((W_SLOTS, 3))]),
        compiler_params=_params(("arbitrary",), vmem=VMEM_LIMIT_EXPERTS),
        name="moe_experts",
    )(blk_expert, blk_src, n_blk, blk_first, blk_next, blk_next2, blk_slot, x_sorted, w_g, w_u, w_d)


def _combine_copy(y_hbm, buf, sem, slot, k, t, src_row):
    return pltpu.make_async_copy(y_hbm.at[pl.ds(src_row, 1)], buf.at[slot, k, pl.ds(t, 1)], sem.at[slot])


def _combine_kernel(dest_ref, dest_next_ref, y_hbm, wt_ref, h_ref, x_ref, mod_ref, g_ref,
                    wsg_ref, wsu_ref, wsd_ref, oc_ref, ol_ref, buf, sem, *, n_ctx):
    i = pl.program_id(0)
    n = pl.num_programs(0)
    slot = i % 2

    def issue(d_ref, s):
        def body(t, carry):
            for k in range(TOP_K):
                _combine_copy(y_hbm, buf, sem, s, k, t, d_ref[k, t]).start(priority=k % 2)
            return carry
        lax.fori_loop(0, T_COMBINE, body, 0)

    @pl.when(i == 0)
    def _():
        issue(dest_ref, 0)

    for s in range(2):
        @pl.when((i + 1 < n) & (1 - slot == s))
        def _():
            issue(dest_next_ref, s)

    pltpu.make_async_copy(buf.at[slot], buf.at[slot], sem.at[slot]).wait()

    wt = wt_ref[...]
    acc = buf[slot, 0] * wt[:, 0:1]
    for k in range(1, TOP_K):
        acc = acc + buf[slot, k] * wt[:, k:k + 1]
    h = h_ref[...].astype(BF16)
    act = _silu(_mm(h, wsg_ref[...])) * _mm(h, wsu_ref[...])
    y = acc + _mm(act.astype(BF16), wsd_ref[...])
    out = x_ref[...] + mod_ref[0, 5:6, :] * _rms(y, g_ref[...])

    @pl.when(i < n_ctx)
    def _():
        oc_ref[...] = out

    @pl.when(i >= n_ctx)
    def _():
        ol_ref[...] = out


def _combine(dest, y_sorted, wt_tok, h, x1, mods, g_post, ws_g, ws_u, ws_d):
    tc = T_COMBINE
    n = M_TOK // tc
    n_ctx = N_CTX_TOK // tc
    row = lambda i: (i, 0)
    const = lambda i: (0, 0)
    return pl.pallas_call(
        functools.partial(_combine_kernel, n_ctx=n_ctx),
        out_shape=(jax.ShapeDtypeStruct((N_CTX_TOK, D_MODEL), F32),
                   jax.ShapeDtypeStruct((M_TOK - N_CTX_TOK, D_MODEL), F32)),
        grid=(n,),
        in_specs=[pl.BlockSpec((TOP_K, tc), lambda i: (0, i), memory_space=pltpu.SMEM),
                  pl.BlockSpec((TOP_K, tc), lambda i: (0, jnp.minimum(i + 1, n - 1)),
                               memory_space=pltpu.SMEM),
                  pl.BlockSpec(memory_space=pl.ANY),
                  pl.BlockSpec((tc, LANES), row),
                  pl.BlockSpec((tc, D_MODEL), row),
                  pl.BlockSpec((tc, D_MODEL), row),
                  pl.BlockSpec((1, SUBLANES, D_MODEL), lambda i: (i * tc // ROWS_PER_MOD, 0, 0)),
                  pl.BlockSpec((1, D_MODEL), const),
                  pl.BlockSpec((D_MODEL, EXPERT_FF), const),
                  pl.BlockSpec((D_MODEL, EXPERT_FF), const),
                  pl.BlockSpec((EXPERT_FF, D_MODEL), const)],
        out_specs=(pl.BlockSpec((tc, D_MODEL), lambda i: (jnp.minimum(i, n_ctx - 1), 0)),
                   pl.BlockSpec((tc, D_MODEL), lambda i: (jnp.maximum(i - n_ctx, 0), 0))),
        scratch_shapes=[pltpu.VMEM((2, TOP_K, tc, D_MODEL), F32),
                        pltpu.SemaphoreType.DMA((2,))],
        compiler_params=_params(("arbitrary",)),
        name="moe_combine",
    )(dest, dest, y_sorted, wt_tok, h, x1, mods, g_post, ws_g, ws_u, ws_d)


def _grid_pos_embed(n_tokens):
    rows = n_tokens // GRID_W
    half = D_MODEL // 2
    quarter = half // 2
    omega = 1.0 / (10000.0 ** (jnp.arange(quarter, dtype=F32) / quarter))

    def axis_embed(pos):
        ang = pos.astype(F32)[:, None] * omega
        return jnp.concatenate([jnp.sin(ang), jnp.cos(ang)], axis=-1)

    e_row = axis_embed(jnp.arange(rows))
    e_col = axis_embed(jnp.arange(GRID_W))
    emb = jnp.concatenate([jnp.broadcast_to(e_row[:, None], (rows, GRID_W, half)),
                           jnp.broadcast_to(e_col[None], (rows, GRID_W, half))], axis=-1)
    return emb.reshape(rows * GRID_W, D_MODEL)


def _lane_row(v, lane0):
    return jnp.zeros((1, LANES), F32).at[0, lane0:lane0 + v.shape[0]].set(v)


def kernel(x_prompt, x_sample, state_gla, state_ssd, c, c_ctx, w_ada, b_ada, g_mix_pre, g_mix_post, w_in, w_gk_up, b_gk, g_gla_norm, conv_w, conv_b, dt_bias, a_log, d_skip, g_ssd_norm, w_out, g_ffn_pre, g_ffn_post, w_router, b_router, w_exp_gate, w_exp_up, w_exp_down, w_sh_gate, w_sh_up, w_sh_down):
    assert x_prompt.shape == (N_CTX_SEQ, CTX_LEN, D_MODEL) and x_sample.shape == (N_LAT_SEQ, LAT_LEN, D_MODEL)
    assert w_ada.shape[0] == 1, "single layer"
    l = 0

    w_in_l = w_in[l]
    o_q = 0
    o_k = o_q + GLA_KEY_WIDTH
    o_v = o_k + GLA_KEY_WIDTH
    o_gout = o_v + GLA_WIDTH
    o_lr = o_gout + GLA_WIDTH
    o_z = o_lr + N_DIR * GLA_GATE_RANK
    o_xbc = o_z + SSD_WIDTH
    o_dt = o_xbc + SSD_CONV_CH
    o_end = o_dt + N_DIR * SSD_HEADS
    assert o_end == w_in_l.shape[1]
    w_main = jnp.concatenate([w_in_l[:, o_gout:o_lr], w_in_l[:, o_z:o_xbc], w_in_l[:, o_v:o_gout],
                              w_in_l[:, o_xbc:o_dt], w_in_l[:, o_q:o_k], w_in_l[:, o_k:o_v]],
                             axis=1).astype(BF16)
    w_small = jnp.concatenate([w_in_l[:, o_lr:o_z], w_in_l[:, o_dt:o_end],
                               jnp.zeros((D_MODEL, LANES - SM_DT - N_DIR * SSD_HEADS), F32)], axis=1)
    w_out_bf = w_out[l].astype(BF16)
    wr_t = w_router[l].T
    wr_hi = wr_t.astype(BF16)
    wr_lo = (wr_t - wr_hi.astype(F32)).astype(BF16)
    conv_w8 = jnp.zeros((SUBLANES, SSD_CONV_CH), F32).at[:conv_w.shape[1]].set(conv_w[l])
    d_skip_x = jnp.repeat(d_skip[l], SSD_HEAD_DIM)[None, :]
    g_gla = g_gla_norm[l][None, :]
    g_ssd = g_ssd_norm[l][None, :]

    cvecs = jnp.zeros((SUBLANES, D_MODEL), F32).at[0].set(c_ctx).at[1:1 + N_LAT_SEQ].set(c)
    mod_flat = _ada_mod(cvecs, w_ada[l], b_ada[l][None, :])
    mods = jnp.zeros((N_MOD, SUBLANES, D_MODEL), F32).at[:, :6].set(
        mod_flat[:N_MOD].reshape(N_MOD, 6, D_MODEL))

    x_ctx = x_prompt.reshape(N_CTX_TOK, D_MODEL)
    x_lat = x_sample.reshape(N_LAT_SEQ * LAT_LEN, D_MODEL)
    pos = _grid_pos_embed(LAT_LEN)

    proj_main, proj_small = _in_proj(x_ctx, x_lat, pos, mods, g_mix_pre[l][None, :], w_main, w_small)

    wgk_pads = [jnp.zeros((LANES, GLA_KEY_WIDTH), F32).at[
        SM_LR + d * GLA_GATE_RANK:SM_LR + (d + 1) * GLA_GATE_RANK].set(w_gk_up[l, d]) for d in range(N_DIR)]
    gla_out = _gla_scan(proj_main, proj_small, wgk_pads, [b_gk[l, d][None, :] for d in range(N_DIR)],
                        [state_gla[:, l, d] for d in range(N_DIR)])
    o_dir = [gla_out[d][0] for d in range(N_DIR)]
    s_gla_dir = [gla_out[d][1] for d in range(N_DIR)]
    y_dir, s_ssd_dir = [], []
    for d in range(N_DIR):
        rev = d == 1
        lane0 = SM_DT + d * SSD_HEADS
        dtb_r = _lane_row(dt_bias[l, d], lane0)
        nea_r = _lane_row(-jnp.exp(a_log[l, d]), lane0)
        s0_ssd = state_ssd[:, l, d].reshape(N_LAT_SEQ, SSD_GROUPS, SSD_GROUP_WIDTH, SSD_STATE)
        y_d, t_d = _ssd_scan(proj_main, proj_small, conv_w8, conv_b[l][None, :],
                             dtb_r, nea_r, dtb_r.T, nea_r.T, d_skip_x, s0_ssd, rev, d)
        y_dir.append(y_d)
        s_ssd_dir.append(t_d.reshape(N_CTX_SEQ, SSD_HEADS, SSD_HEAD_DIM, SSD_STATE))
    new_state_gla = jnp.stack(s_gla_dir, axis=1)[:, None]
    new_state_ssd = jnp.stack(s_ssd_dir, axis=1)[:, None]

    x1 = _out_proj(o_dir[0], o_dir[1], y_dir[0], y_dir[1], proj_main, x_ctx, x_lat, pos, mods,
                   g_gla, g_ssd, g_mix_post[l][None, :], w_out_bf)

    h_ffn, idx, wt_tok, rank, cnt = _router(x1, mods, g_ffn_pre[l][None, :], wr_hi, wr_lo,
                                            b_router[l][:, None])
    i32 = jnp.int32
    e_ids = jnp.arange(N_EXPERTS, dtype=i32)
    counts = cnt[:, 0].astype(i32)
    padded = (counts + MOE_BLK - 1) // MOE_BLK * MOE_BLK
    pad_end = jnp.sum(jnp.where(e_ids[None, :] <= e_ids[:, None], padded[None, :], 0), axis=1)
    pad_start = pad_end - padded
    slot0 = jnp.sum(jnp.where(idx[:, :, None] == e_ids, pad_start, 0), axis=-1)
    dest = slot0 + rank
    n_blk = pad_end[-1] // MOE_BLK
    blk_ids = jnp.arange(N_MOE_BLOCKS, dtype=i32)
    blk_src = jnp.minimum(blk_ids, n_blk - 1)
    blk_expert = jnp.minimum(jnp.sum((pad_end[None, :] <= (blk_src * MOE_BLK)[:, None]).astype(i32), axis=1),
                             N_EXPERTS - 1)
    blk_hot = blk_expert[:, None] == e_ids
    nonempty = counts > 0
    ordinal = jnp.sum(jnp.where((e_ids[None, :] < e_ids[:, None]) & nonempty[None, :], 1, 0), axis=1)
    next_e = jnp.min(jnp.where((e_ids[None, :] > e_ids[:, None]) & nonempty[None, :], e_ids[None, :],
                               N_EXPERTS), axis=1)
    next_e = jnp.where(next_e == N_EXPERTS, -1, next_e)
    blk_first = jnp.concatenate([jnp.ones((1,), i32), (blk_expert[1:] != blk_expert[:-1]).astype(i32)])
    next2_e = jnp.where(next_e >= 0,
                        jnp.sum(jnp.where(e_ids[None, :] == next_e[:, None], next_e[None, :], 0), axis=1), -1)
    blk_next = jnp.sum(jnp.where(blk_hot, next_e, 0), axis=1).astype(i32)
    blk_next2 = jnp.sum(jnp.where(blk_hot, next2_e, 0), axis=1).astype(i32)
    blk_slot = (jnp.sum(jnp.where(blk_hot, ordinal, 0), axis=1) % W_SLOTS).astype(i32)
    n_blk_arr = n_blk.astype(i32)[None]
    x_sorted = _dispatch((pad_start + counts).astype(i32), (padded - counts).astype(i32),
                         n_blk_arr, dest, h_ffn)
    y_sorted = _experts(blk_expert.astype(i32), blk_src.astype(i32), n_blk_arr, blk_first, blk_next,
                        blk_next2, blk_slot, x_sorted, w_exp_gate[l], w_exp_up[l], w_exp_down[l])
    out_ctx, out_lat = _combine(dest, y_sorted, wt_tok, h_ffn, x1, mods, g_ffn_post[l][None, :],
                                w_sh_gate[l].astype(BF16), w_sh_up[l].astype(BF16),
                                w_sh_down[l].astype(BF16))
    return (out_ctx.reshape(N_CTX_SEQ, CTX_LEN, D_MODEL), out_lat.reshape(N_LAT_SEQ, LAT_LEN, D_MODEL),
            new_state_gla, new_state_ssd)
```
